```python
import math
import jax, jax.numpy as jnp
from jax import lax
import numpy as np

D_MODEL = 1024
BATCH = 4
SEQ = 4096
DEPTH = 2

GRID_W = 64
N_MEM = 256
BRANCH_W = 256
N_BRANCH = 5
NA_HEADS = 4
NA_HD = 64
NA_KH = 8
NA_KW = 16
DA_HEADS = 4
DA_QK = 32
DA_V = 64
Q_BLOCK = 128
ALIBI_BASE = 8.0
POOL_WINDOWS = (2, 4, 8, 16)
POOL_GROUPS = 4
POOL_GC = 64
SG_CHUNK = 128
SG_GROUPS = 4
SG_GC = 64
MEM_HEADS = 4
MEM_HD = 64
N_SLABS = 15
SLAB_COLS = N_SLABS * BRANCH_W
IN_COLS = SLAB_COLS + N_BRANCH * D_MODEL
EPS = 1e-6

kernel_name = "gated_parallel_hybrid_encoder"


def rmsnorm(x, g):
    xf = x.astype(jnp.float32)
    y = xf * lax.rsqrt(jnp.mean(xf * xf, axis=-1, keepdims=True) + EPS)
    return (y * g.astype(jnp.float32)).astype(x.dtype)


def alibi_slopes(n_heads):
    return jnp.asarray(np.array([2.0 ** (-ALIBI_BASE * (h + 1) / n_heads) for h in range(n_heads)], dtype=np.float32))


def neighbourhood_attention(q, k, v, rpb):
    B, T, H, hd = q.shape
    R = T // GRID_W
    KH = min(NA_KH, R)
    rows = np.arange(R)
    rs = np.clip(rows - KH // 2, 0, R - KH)
    row_idx = rs[:, None] + np.arange(KH)[None, :]
    cols = np.arange(GRID_W)
    cs = np.clip(cols - NA_KW // 2, 0, GRID_W - NA_KW)
    col_mask = (cols[None, :] >= cs[:, None]) & (cols[None, :] < cs[:, None] + NA_KW)
    roff = row_idx - rows[:, None] + NA_KH - 1
    coff = np.clip(cols[None, :] - cols[:, None], -(NA_KW - 1), NA_KW - 1) + NA_KW - 1
    bias = rpb.astype(jnp.float32)[:, roff[:, :, None, None], coff[None, None, :, :]]
    bias = bias.transpose(0, 1, 3, 2, 4)
    qg = q.reshape(B, R, GRID_W, H, hd)
    kg = k.reshape(B, R, GRID_W, H, hd)[:, row_idx]
    vg = v.reshape(B, R, GRID_W, H, hd)[:, row_idx]
    s = jnp.einsum('brqhd,brikhd->bhrqik', qg, kg).astype(jnp.float32) * (hd ** -0.5)
    s = s + bias[None]
    s = jnp.where(jnp.asarray(col_mask)[:, None, :], s, jnp.finfo(jnp.float32).min)
    p = jax.nn.softmax(s.reshape(B, H, R, GRID_W, KH * GRID_W), axis=-1)
    p = p.reshape(B, H, R, GRID_W, KH, GRID_W)
    o = jnp.einsum('bhrqik,brikhd->brqhd', p, vg.astype(jnp.float32))
    return o.reshape(B, T, H, hd).astype(q.dtype)


def diff_attention(q, k, v, lam):
    B, T, H, _, d = q.shape
    nb = T // Q_BLOCK
    slopes = alibi_slopes(H)
    qb = q.reshape(B, nb, Q_BLOCK, H, 2, d).transpose(1, 0, 2, 3, 4, 5)
    kpos = jnp.arange(T)
    vf = v.astype(jnp.float32)

    def block(args):
        qi, start = args
        s = jnp.einsum('bqhmd,bkhmd->bhmqk', qi, k).astype(jnp.float32) * (d ** -0.5)
        qpos = start + jnp.arange(Q_BLOCK)
        dist = jnp.abs(qpos[:, None] - kpos[None, :]).astype(jnp.float32)
        s = s - slopes[None, :, None, None, None] * dist
        p = jax.nn.softmax(s, axis=-1)
        a = p[:, :, 0] - lam * p[:, :, 1]
        return jnp.einsum('bhqk,bkhe->bqhe', a, vf)

    out = lax.map(block, (qb, jnp.arange(nb) * Q_BLOCK))
    return out.transpose(1, 0, 2, 3, 4).reshape(B, T, H, v.shape[-1]).astype(q.dtype)


def multiscale_pool_mixer(xc, c_w, c_scale):
    B, T, C = xc.shape
    xf = xc.astype(jnp.float32)
    csum = jnp.concatenate([jnp.zeros((B, 1, C), jnp.float32), jnp.cumsum(xf, axis=1)], axis=1)
    t = np.arange(T)
    outs = []
    for g, w in enumerate(POOL_WINDOWS):
        lo = np.clip(t - w // 2, 0, T - 1)
        hi = np.clip(t - w // 2 + w - 1, 0, T - 1)
        cnt = jnp.asarray((hi - lo + 1).astype(np.float32))
        sl = slice(g * POOL_GC, (g + 1) * POOL_GC)
        wsum = csum[:, hi + 1, sl] - csum[:, lo, sl]
        outs.append(wsum / cnt[None, :, None] - xf[:, :, sl])
    dlt = jnp.stack(outs, axis=2)
    y = jnp.einsum('btgc,gce->btge', dlt, c_w.astype(jnp.float32)).reshape(B, T, C)
    return (y * c_scale.astype(jnp.float32)).astype(xc.dtype)


def spatial_gating(u, v, ln_g, ln_b, ws, bs):
    B, T, C = v.shape
    vf = v.astype(jnp.float32)
    mu = jnp.mean(vf, axis=-1, keepdims=True)
    var = jnp.mean((vf - mu) ** 2, axis=-1, keepdims=True)
    vn = (vf - mu) * lax.rsqrt(var + EPS) * ln_g.astype(jnp.float32) + ln_b.astype(jnp.float32)
    vc = vn.reshape(B, T // SG_CHUNK, SG_CHUNK, SG_GROUPS, SG_GC)
    mixed = jnp.einsum('gps,bnsgc->bnpgc', ws.astype(jnp.float32), vc)
    mixed = mixed + bs.astype(jnp.float32).T[None, None, :, :, None]
    return (u.astype(jnp.float32) * mixed.reshape(B, T, C)).astype(u.dtype)


def memory_attention(q, k, v):
    s = jnp.einsum('bthd,bmhd->bhtm', q, k).astype(jnp.float32) * (q.shape[-1] ** -0.5)
    p = jax.nn.softmax(s, axis=-1)
    return jnp.einsum('bhtm,bmhd->bthd', p, v.astype(jnp.float32)).astype(q.dtype)


def hybrid_layer(x, mem, layer_idx, norm_g, w_in, b_gate, a_qn_g, a_kn_g, a_rpb,
                 b_qn_g, b_kn_g, b_lam_q1, b_lam_k1, b_lam_q2, b_lam_k2, b_sub_g,
                 c_w, c_scale, d_ln_g, d_ln_b, d_ws, d_bs,
                 m_norm_g, m_wkv, m_qn_g, m_kn_g, w_branch, w_out):
    B, T, _ = x.shape
    h = rmsnorm(x, norm_g)
    proj = h @ w_in
    (a_q, a_k, a_v, a_z, b_q, b_k, b_v, b_z, c_x, c_z,
     d_u, d_v, d_z, m_q, m_z) = jnp.split(proj[..., :SLAB_COLS], N_SLABS, axis=-1)

    qa = rmsnorm(a_q.reshape(B, T, NA_HEADS, NA_HD), a_qn_g)
    ka = rmsnorm(a_k.reshape(B, T, NA_HEADS, NA_HD), a_kn_g)
    va = a_v.reshape(B, T, NA_HEADS, NA_HD)
    y_a = neighbourhood_attention(qa, ka, va, a_rpb).reshape(B, T, BRANCH_W) * jax.nn.silu(a_z)

    qb = rmsnorm(b_q.reshape(B, T, DA_HEADS, 2, DA_QK), b_qn_g)
    kb = rmsnorm(b_k.reshape(B, T, DA_HEADS, 2, DA_QK), b_kn_g)
    vb = b_v.reshape(B, T, DA_HEADS, DA_V)
    lam_init = 0.8 - 0.6 * math.exp(-0.3 * layer_idx)
    lam = (jnp.exp(jnp.sum(b_lam_q1.astype(jnp.float32) * b_lam_k1.astype(jnp.float32)))
           - jnp.exp(jnp.sum(b_lam_q2.astype(jnp.float32) * b_lam_k2.astype(jnp.float32))) + lam_init)
    ob = diff_attention(qb, kb, vb, lam)
    ob = rmsnorm(ob, b_sub_g) * (1.0 - lam_init)
    y_b = ob.reshape(B, T, BRANCH_W) * jax.nn.silu(b_z)

    y_c = multiscale_pool_mixer(c_x, c_w, c_scale) * jax.nn.silu(c_z)

    y_d = spatial_gating(jax.nn.gelu(d_u), jax.nn.gelu(d_v), d_ln_g, d_ln_b, d_ws, d_bs) * jax.nn.silu(d_z)

    mkv = rmsnorm(mem, m_norm_g) @ m_wkv
    mk, mv = jnp.split(mkv, 2, axis=-1)
    M = mem.shape[1]
    qm = rmsnorm(m_q.reshape(B, T, MEM_HEADS, MEM_HD), m_qn_g)
    km = rmsnorm(mk.reshape(B, M, MEM_HEADS, MEM_HD), m_kn_g)
    vm = mv.reshape(B, M, MEM_HEADS, MEM_HD)
    y_m = memory_attention(qm, km, vm).reshape(B, T, BRANCH_W) * jax.nn.silu(m_z)

    merged = jnp.zeros_like(x)
    for i, y in enumerate((y_a, y_b, y_c, y_d, y_m)):
        g_logit = proj[..., SLAB_COLS + i * D_MODEL: SLAB_COLS + (i + 1) * D_MODEL] + b_gate[i]
        merged = merged + jax.nn.sigmoid(g_logit.astype(jnp.float32)).astype(x.dtype) * (y @ w_branch[i])
    return x + merged @ w_out


def setup_inputs(seed: int = 0) -> dict:
    key = jax.random.key(seed)
    ks = jax.random.split(key, 32)
    f32 = jnp.float32
    L, D = DEPTH, D_MODEL

    def nrm(k, shape, scale):
        return jax.random.normal(k, shape, f32) * scale

    return {
        "x": nrm(ks[0], (BATCH, SEQ, D), 1.0),
        "mem": nrm(ks[1], (BATCH, N_MEM, D), 1.0),
        "norm_g": 1.0 + nrm(ks[2], (L, D), 0.05),
        "w_in": nrm(ks[3], (L, D, IN_COLS), D ** -0.5),
        "b_gate": nrm(ks[4], (L, N_BRANCH, D), 0.01),
        "a_qn_g": 1.0 + nrm(ks[5], (L, NA_HD), 0.05),
        "a_kn_g": 1.0 + nrm(ks[6], (L, NA_HD), 0.05),
        "a_rpb": nrm(ks[7], (L, NA_HEADS, 2 * NA_KH - 1, 2 * NA_KW - 1), 0.1),
        "b_qn_g": 1.0 + nrm(ks[8], (L, DA_QK), 0.05),
        "b_kn_g": 1.0 + nrm(ks[9], (L, DA_QK), 0.05),
        "b_lam_q1": nrm(ks[10], (L, DA_QK), 0.1),
        "b_lam_k1": nrm(ks[11], (L, DA_QK), 0.1),
        "b_lam_q2": nrm(ks[12], (L, DA_QK), 0.1),
        "b_lam_k2": nrm(ks[13], (L, DA_QK), 0.1),
        "b_sub_g": 1.0 + nrm(ks[14], (L, DA_V), 0.05),
        "c_w": nrm(ks[15], (L, POOL_GROUPS, POOL_GC, POOL_GC), POOL_GC ** -0.5),
        "c_scale": 1.0 + nrm(ks[16], (L, POOL_GROUPS * POOL_GC), 0.1),
        "d_ln_g": 1.0 + nrm(ks[17], (L, SG_GROUPS * SG_GC), 0.05),
        "d_ln_b": nrm(ks[18], (L, SG_GROUPS * SG_GC), 0.02),
        "d_ws": nrm(ks[19], (L, SG_GROUPS, SG_CHUNK, SG_CHUNK), SG_CHUNK ** -0.5),
        "d_bs": 1.0 + nrm(ks[20], (L, SG_GROUPS, SG_CHUNK), 0.01),
        "m_norm_g": 1.0 + nrm(ks[21], (L, D), 0.05),
        "m_wkv": nrm(ks[22], (L, D, 2 * BRANCH_W), D ** -0.5),
        "m_qn_g": 1.0 + nrm(ks[23], (L, MEM_HD), 0.05),
        "m_kn_g": 1.0 + nrm(ks[24], (L, MEM_HD), 0.05),
        "w_branch": nrm(ks[25], (L, N_BRANCH, BRANCH_W, D), BRANCH_W ** -0.5),
        "w_out": nrm(ks[26], (L, D, D), (2.0 * D) ** -0.5),
    }


def reference(x, mem, norm_g, w_in, b_gate, a_qn_g, a_kn_g, a_rpb,
              b_qn_g, b_kn_g, b_lam_q1, b_lam_k1, b_lam_q2, b_lam_k2, b_sub_g,
              c_w, c_scale, d_ln_g, d_ln_b, d_ws, d_bs,
              m_norm_g, m_wkv, m_qn_g, m_kn_g, w_branch, w_out):
    for l in range(DEPTH):
        x = hybrid_layer(x, mem, l, norm_g[l], w_in[l], b_gate[l], a_qn_g[l], a_kn_g[l], a_rpb[l],
                         b_qn_g[l], b_kn_g[l], b_lam_q1[l], b_lam_k1[l], b_lam_q2[l], b_lam_k2[l], b_sub_g[l],
                         c_w[l], c_scale[l], d_ln_g[l], d_ln_b[l], d_ws[l], d_bs[l],
                         m_norm_g[l], m_wkv[l], m_qn_g[l], m_kn_g[l], w_branch[l], w_out[l])
    return x
```

```python
import functools
import math

import numpy as np
import jax
import jax.numpy as jnp
from jax import lax
from jax.experimental import pallas as pl
from jax.experimental.pallas import tpu as pltpu

F32 = jnp.float32
BF = jnp.bfloat16

D_MODEL = 1024
DEPTH = 2
GRID_W = 64
BRANCH_W = 256
N_BRANCH = 5
NA_HEADS = 4
NA_HD = 64
NA_KH = 8
NA_KW = 16
DA_HEADS = 4
DA_QK = 32
DA_V = 64
ALIBI_BASE = 8.0
POOL_WINDOWS = (2, 4, 8, 16)
POOL_GC = 64
SG_CHUNK = 128
SG_GROUPS = 4
SG_GC = 64
MEM_HEADS = 4
MEM_HD = 64
N_SLABS = 15
SLAB_COLS = N_SLABS * BRANCH_W
EPS = 1e-6

(S_AQ, S_AK, S_AV, S_AZ, S_BQ, S_BK, S_BV, S_BZ, S_CX, S_CZ,
 S_DU, S_DV, S_DZ, S_MQ, S_MZ) = range(N_SLABS)

(G_AQ, G_AK, G_BQ, G_BK, G_MQ, G_LNG, G_LNB) = range(7)

VMEM_LIMIT_BYTES = 56 * 1024 * 1024
NEG_BIG = -1e30

ALIBI_SLOPES = tuple(2.0 ** (-ALIBI_BASE * (h + 1) / DA_HEADS) for h in range(DA_HEADS))
X_HI, X_LO, X_ONE_A, X_ONE_B = 64, 65, 66, 67
DA_LANES = 128


def _cparams(*sem):
    return pltpu.CompilerParams(dimension_semantics=sem, vmem_limit_bytes=VMEM_LIMIT_BYTES)


def _resident(shape, index_map):
    return pl.BlockSpec(shape, index_map, pipeline_mode=pl.Buffered(1))


def _dot(a, b):
    return jnp.dot(a, b, preferred_element_type=F32)


def _dot_nt(a, b):
    return lax.dot_general(a, b, (((1,), (1,)), ((), ())), preferred_element_type=F32)


def _lane_group(shape, log2_group):
    return lax.broadcasted_iota(jnp.int32, shape, len(shape) - 1) >> log2_group


def _lane_mask(shape, lo, hi, dtype):
    lane = lax.broadcasted_iota(jnp.int32, shape, len(shape) - 1)
    return jnp.where((lane >= lo) & (lane < hi), 1.0, 0.0).astype(dtype)


def _group_mean_sq(xf, group):
    n = xf.shape[-1]
    sh = int(math.log2(group))
    r = lax.broadcasted_iota(jnp.int32, (n, n), 0) >> sh
    c = lax.broadcasted_iota(jnp.int32, (n, n), 1) >> sh
    pm = jnp.where(r == c, 1.0 / group, 0.0).astype(BF)
    xx = xf * xf
    hi = xx.astype(BF)
    lo = (xx - hi.astype(F32)).astype(BF)
    return _dot(hi, pm) + _dot(lo, pm)


def _sigmoid(x):
    return 1.0 / (1.0 + jnp.exp(-x))


def _silu(x):
    return x * _sigmoid(x)


def _gelu_tanh(x):
    return 0.5 * x * (1.0 + jnp.tanh(math.sqrt(2.0 / math.pi) * (x + 0.044715 * (x * x * x))))


def _rmsnorm_rows(x, g):
    ms = jnp.mean(x * x, axis=-1, keepdims=True)
    return x * lax.rsqrt(ms + EPS) * g


def _proj_kernel(x_ref, ng_ref, w_ref, gains_ref, o_ref):
    h = _rmsnorm_rows(x_ref[...], ng_ref[...]).astype(BF)

    def gnorm(r, group, row):
        return r * lax.rsqrt(_group_mean_sq(r, group) + EPS) * gains_ref[row:row + 1, :]

    for j in range(N_SLABS):
        r = _dot(h, w_ref[:, j * BRANCH_W:(j + 1) * BRANCH_W])
        if j == S_AQ:
            r = gnorm(r, NA_HD, G_AQ)
        elif j == S_AK:
            r = gnorm(r, NA_HD, G_AK)
        elif j == S_BQ:
            r = gnorm(r, DA_QK, G_BQ)
        elif j == S_BK:
            r = gnorm(r, DA_QK, G_BK)
        elif j == S_MQ:
            r = gnorm(r, MEM_HD, G_MQ)
        elif j in (S_AZ, S_BZ, S_CZ, S_DZ, S_MZ):
            r = _silu(r)
        elif j == S_DU:
            r = _gelu_tanh(r)
        elif j == S_DV:
            v = _gelu_tanh(r)
            mu = jnp.mean(v, axis=-1, keepdims=True)
            vc = v - mu
            var = jnp.mean(vc * vc, axis=-1, keepdims=True)
            r = vc * lax.rsqrt(var + EPS) * gains_ref[G_LNG:G_LNG + 1, :] + gains_ref[G_LNB:G_LNB + 1, :]
        o_ref[:, j * BRANCH_W:(j + 1) * BRANCH_W] = r.astype(BF)


def _proj(x2, norm_g, w_slab, gains, tm=512):
    n = x2.shape[0]
    return pl.pallas_call(
        _proj_kernel,
        grid=(n // tm,),
        in_specs=[
            pl.BlockSpec((tm, D_MODEL), lambda i: (i, 0)),
            _resident((1, D_MODEL), lambda i: (0, 0)),
            _resident((D_MODEL, SLAB_COLS), lambda i: (0, 0)),
            _resident((8, BRANCH_W), lambda i: (0, 0)),
        ],
        out_specs=pl.BlockSpec((tm, SLAB_COLS), lambda i: (i, 0)),
        out_shape=jax.ShapeDtypeStruct((n, SLAB_COLS), BF),
        compiler_params=_cparams("parallel"),
        name="proj",
    )(x2, norm_g, w_slab, gains)


def _memkv_kernel(mem_ref, g_ref, w_ref, kg_ref, k_ref, v_ref):
    h = _rmsnorm_rows(mem_ref[0], g_ref[...]).astype(BF)
    k = _dot(h, w_ref[:, :BRANCH_W])
    v = _dot(h, w_ref[:, BRANCH_W:])
    k = k * lax.rsqrt(_group_mean_sq(k, MEM_HD) + EPS) * kg_ref[...]
    k_ref[0] = k.astype(BF)
    v_ref[0] = v.astype(BF)


def _memkv(mem, m_norm_g, m_wkv, kg):
    b, m, _ = mem.shape
    return pl.pallas_call(
        _memkv_kernel,
        grid=(b,),
        in_specs=[
            pl.BlockSpec((1, m, D_MODEL), lambda i: (i, 0, 0)),
            _resident((1, D_MODEL), lambda i: (0, 0)),
            _resident((D_MODEL, 2 * BRANCH_W), lambda i: (0, 0)),
            _resident((1, BRANCH_W), lambda i: (0, 0)),
        ],
        out_specs=[pl.BlockSpec((1, m, BRANCH_W), lambda i: (i, 0, 0)),
                   pl.BlockSpec((1, m, BRANCH_W), lambda i: (i, 0, 0))],
        out_shape=[jax.ShapeDtypeStruct((b, m, BRANCH_W), BF)] * 2,
        compiler_params=_cparams("parallel"),
        name="memkv",
    )(mem, m_norm_g, m_wkv, kg)


NA_ROWS_PER_STEP = 8
NA_WIN = NA_KH * GRID_W


def _na_kernel(q_ref, k_ref, v_ref, z_ref, bias_ref, o_ref):
    n_rows = k_ref.shape[1] // GRID_W
    rb = pl.program_id(1)
    head_of_lane = _lane_group((GRID_W, BRANCH_W), 6)

    def row(rr, carry):
        r = rb * NA_ROWS_PER_STEP + rr
        rs = jnp.clip(r - NA_KH // 2, 0, n_rows - NA_KH)
        cls = jnp.where(r < NA_KH // 2, r, jnp.where(r > n_rows - NA_KH // 2, r - (n_rows - NA_KH), NA_KH // 2))
        w0 = pl.multiple_of(rs * GRID_W, GRID_W)
        kwin = k_ref[0, pl.ds(w0, NA_WIN), :]
        vwin = v_ref[0, pl.ds(w0, NA_WIN), :]
        t0 = pl.multiple_of(rr * GRID_W, GRID_W)
        q = q_ref[0, pl.ds(t0, GRID_W), :]
        o = jnp.zeros((GRID_W, BRANCH_W), F32)
        for h in range(NA_HEADS):
            qm = q * _lane_mask(q.shape, h * NA_HD, (h + 1) * NA_HD, BF)
            s = _dot_nt(qm, kwin) + bias_ref[cls, h]
            m = jnp.max(s, axis=-1, keepdims=True)
            p = jnp.exp(s - m)
            l = jnp.sum(p, axis=-1, keepdims=True)
            oh = _dot(p.astype(BF), vwin) / l
            o = jnp.where(head_of_lane == h, oh, o)
        y = o * z_ref[0, pl.ds(t0, GRID_W), :].astype(F32)
        o_ref[0, pl.ds(t0, GRID_W), :] = y.astype(BF)
        return carry

    lax.fori_loop(0, NA_ROWS_PER_STEP, row, 0)


def _na(p3, bias):
    b, t, _ = p3.shape
    tq = NA_ROWS_PER_STEP * GRID_W
    return pl.pallas_call(
        _na_kernel,
        grid=(b, t // tq),
        in_specs=[
            pl.BlockSpec((1, tq, BRANCH_W), lambda i, j: (i, j, S_AQ)),
            pl.BlockSpec((1, t, BRANCH_W), lambda i, j: (i, 0, S_AK)),
            pl.BlockSpec((1, t, BRANCH_W), lambda i, j: (i, 0, S_AV)),
            pl.BlockSpec((1, tq, BRANCH_W), lambda i, j: (i, j, S_AZ)),
            _resident(bias.shape, lambda i, j: (0, 0, 0, 0)),
        ],
        out_specs=pl.BlockSpec((1, tq, BRANCH_W), lambda i, j: (i, j, 0)),
        out_shape=jax.ShapeDtypeStruct((b, t, BRANCH_W), BF),
        compiler_params=_cparams("parallel", "arbitrary"),
        name="na",
    )(p3, p3, p3, p3, bias)


def _na_bias_table(rpb, n_rows):
    half = NA_KH // 2
    rows = np.array([0, 1, 2, 3, half] + [n_rows - half + 1 + i for i in range(half - 1)])
    rs = np.clip(rows - half, 0, n_rows - NA_KH)
    roff = rs[:, None] + np.arange(NA_KH)[None, :] - rows[:, None] + NA_KH - 1
    cols = np.arange(GRID_W)
    cs = np.clip(cols - NA_KW // 2, 0, GRID_W - NA_KW)
    col_mask = (cols[None, :] >= cs[:, None]) & (cols[None, :] < cs[:, None] + NA_KW)
    coff = np.clip(cols[None, :] - cols[:, None], -(NA_KW - 1), NA_KW - 1) + NA_KW - 1
    tbl = rpb.astype(F32)[:, roff[:, :, None, None], coff[None, None, :, :]]
    tbl = jnp.where(jnp.asarray(col_mask)[None, None, None], tbl, NEG_BIG)
    tbl = tbl.transpose(1, 0, 3, 2, 4)
    return tbl.reshape(len(rows), NA_HEADS, GRID_W, NA_WIN)


def _dprep_kernel(q_ref, k_ref, v_ref, q4_ref, kt_ref, v4_ref):
    tt = q_ref.shape[1]
    t0 = pl.program_id(1) * tt
    q = q_ref[0]
    k = k_ref[0]
    v = v_ref[0]
    src = lax.broadcasted_iota(jnp.int32, (BRANCH_W, DA_LANES), 0)
    dst = lax.broadcasted_iota(jnp.int32, (BRANCH_W, DA_LANES), 1)
    src_t = lax.broadcasted_iota(jnp.int32, (DA_LANES, BRANCH_W), 1)
    dst_t = lax.broadcasted_iota(jnp.int32, (DA_LANES, BRANCH_W), 0)
    krow = lax.broadcasted_iota(jnp.int32, (DA_LANES, tt), 0)
    kpos = t0 + lax.broadcasted_iota(jnp.int32, (DA_LANES, tt), 1)
    for h in range(DA_HEADS):
        sel = jnp.where((src == h * DA_V + dst) & (dst < DA_V), 1.0, 0.0).astype(BF)
        sel_t = jnp.where((src_t == h * DA_V + dst_t) & (dst_t < DA_V), 1.0, 0.0).astype(BF)
        q4_ref[0, h] = _dot(q, sel).astype(BF)
        v4_ref[0, h] = _dot(v, sel).astype(BF)
        kt = _dot_nt(sel_t, k)
        slope = ALIBI_SLOPES[h]
        kt = jnp.where(krow == X_HI, (kpos >> 6).astype(F32) * (slope * 64.0), kt)
        kt = jnp.where(krow == X_LO, (kpos & 63).astype(F32) * slope, kt)
        kt = jnp.where((krow == X_ONE_A) | (krow == X_ONE_B), 1.0, kt)
        kt_ref[0, h] = kt.astype(BF)


def _dprep(p3, tt=512):
    b, t, _ = p3.shape
    return pl.pallas_call(
        _dprep_kernel,
        grid=(b, t // tt),
        in_specs=[
            pl.BlockSpec((1, tt, BRANCH_W), lambda i, j: (i, j, S_BQ)),
            pl.BlockSpec((1, tt, BRANCH_W), lambda i, j: (i, j, S_BK)),
            pl.BlockSpec((1, tt, BRANCH_W), lambda i, j: (i, j, S_BV)),
        ],
        out_specs=[
            pl.BlockSpec((1, DA_HEADS, tt, DA_LANES), lambda i, j: (i, 0, j, 0)),
            pl.BlockSpec((1, DA_HEADS, DA_LANES, tt), lambda i, j: (i, 0, 0, j)),
            pl.BlockSpec((1, DA_HEADS, tt, DA_LANES), lambda i, j: (i, 0, j, 0)),
        ],
        out_shape=[
            jax.ShapeDtypeStruct((b, DA_HEADS, t, DA_LANES), BF),
            jax.ShapeDtypeStruct((b, DA_HEADS, DA_LANES, t), BF),
            jax.ShapeDtypeStruct((b, DA_HEADS, t, DA_LANES), BF),
        ],
        compiler_params=_cparams("parallel", "parallel"),
        name="dprep",
    )(p3, p3, p3)


def _da_kernel(lam_init, q_ref, kt_ref, v_ref, z_ref, lamv_ref, subg_ref, o_ref, m_scr, l_scr, acc_scr):
    bq = q_ref.shape[2]
    bk = bq
    nk = kt_ref.shape[3] // bk
    qi = pl.program_id(1)
    lane = lax.broadcasted_iota(jnp.int32, (bq, DA_LANES), 1)
    qpos = qi * bq + lax.broadcasted_iota(jnp.int32, (bq, DA_LANES), 0)
    rel = (lax.broadcasted_iota(jnp.int32, (bq, bk), 0) - lax.broadcasted_iota(jnp.int32, (bq, bk), 1))
    dist = jnp.abs(rel).astype(F32)

    lv = lamv_ref[...]
    lam = (jnp.exp(jnp.sum(lv[0:1] * lv[1:2], axis=-1, keepdims=True))
           - jnp.exp(jnp.sum(lv[2:3] * lv[3:4], axis=-1, keepdims=True)) + lam_init)

    def step(h, kb, qv, diag_bias):
        k0 = pl.multiple_of(kb * bk, bk)
        kt = kt_ref[0, h, :, pl.ds(k0, bk)]
        vv = v_ref[0, h, pl.ds(k0, bk), :]
        for m in range(2):
            s = _dot(qv[m], kt)
            if diag_bias is not None:
                s = s - diag_bias
            m_old = m_scr[m]
            m_new = jnp.maximum(m_old, jnp.max(s, axis=-1, keepdims=True))
            alpha = jnp.exp(m_old - m_new)
            p = jnp.exp(s - m_new)
            l_scr[m] = alpha * l_scr[m] + jnp.sum(p, axis=-1, keepdims=True)
            acc_scr[m] = alpha * acc_scr[m] + _dot(p.astype(BF), vv)
            m_scr[m] = m_new

    heads = []
    for h in range(DA_HEADS):
        slope = ALIBI_SLOPES[h]
        qb = q_ref[0, h]
        qm = [qb * _lane_mask(qb.shape, m * DA_QK, (m + 1) * DA_QK, BF) for m in range(2)]
        ih = (qpos >> 6).astype(F32) * (slope * 64.0)
        il = (qpos & 63).astype(F32) * slope
        ex = jnp.where((lane == X_HI) | (lane == X_LO), 1.0,
                       jnp.where(lane == X_ONE_A, -ih, jnp.where(lane == X_ONE_B, -il, 0.0))).astype(BF)
        q_left = [qm[m] + ex for m in range(2)]
        q_right = [qm[m] - ex for m in range(2)]

        m_scr[...] = jnp.full(m_scr.shape, NEG_BIG, F32)
        l_scr[...] = jnp.zeros(l_scr.shape, F32)
        acc_scr[...] = jnp.zeros(acc_scr.shape, F32)

        def left(kb, c, h=h, qv=q_left):
            step(h, kb, qv, None)
            return c

        def right(kb, c, h=h, qv=q_right):
            step(h, kb, qv, None)
            return c

        lax.fori_loop(0, qi, left, 0)
        step(h, qi, qm, dist * slope)
        lax.fori_loop(qi + 1, nk, right, 0)

        o0 = acc_scr[0] / l_scr[0]
        o1 = acc_scr[1] / l_scr[1]
        heads.append(o0 - lam * o1)

    low = lane < DA_V
    full = jnp.concatenate(
        [jnp.where(low, heads[0], pltpu.roll(heads[1], DA_V, 1)),
         jnp.where(low, heads[2], pltpu.roll(heads[3], DA_V, 1))], axis=-1)
    y = full * lax.rsqrt(_group_mean_sq(full, DA_V) + EPS) * subg_ref[...]
    o_ref[0] = (y * z_ref[0].astype(F32)).astype(BF)


def _da(q4, kt4, v4, p3, lamv, subg, lam_init, bq=512):
    b, _, t, _ = q4.shape
    return pl.pallas_call(
        functools.partial(_da_kernel, lam_init),
        grid=(b, t // bq),
        in_specs=[
            pl.BlockSpec((1, DA_HEADS, bq, DA_LANES), lambda i, j: (i, 0, j, 0)),
            pl.BlockSpec((1, DA_HEADS, DA_LANES, t), lambda i, j: (i, 0, 0, 0)),
            pl.BlockSpec((1, DA_HEADS, t, DA_LANES), lambda i, j: (i, 0, 0, 0)),
            pl.BlockSpec((1, bq, BRANCH_W), lambda i, j: (i, j, S_BZ)),
            _resident((4, DA_QK), lambda i, j: (0, 0)),
            _resident((1, BRANCH_W), lambda i, j: (0, 0)),
        ],
        out_specs=pl.BlockSpec((1, bq, BRANCH_W), lambda i, j: (i, j, 0)),
        out_shape=jax.ShapeDtypeStruct((b, t, BRANCH_W), BF),
        scratch_shapes=[
            pltpu.VMEM((2, bq, 1), F32),
            pltpu.VMEM((2, bq, 1), F32),
            pltpu.VMEM((2, bq, DA_LANES), F32),
        ],
        compiler_params=_cparams("parallel", "arbitrary"),
        name="da",
    )(q4, kt4, v4, p3, lamv, subg)


POOL_HALO = 16


def _pool_kernel(x_ref, z_ref, band_ref, cw_ref, cs_ref, o_ref):
    tt = z_ref.shape[1]
    t_total = x_ref.shape[1]
    nt = t_total // tt
    ti = pl.program_id(1)
    t0 = pl.multiple_of(ti * tt, tt)
    xm = x_ref[0, pl.ds(t0, tt), :]
    p0 = pl.multiple_of(jnp.maximum(t0 - POOL_HALO, 0), POOL_HALO)
    n0 = pl.multiple_of(jnp.minimum(t0 + tt, t_total - POOL_HALO), POOL_HALO)
    has_prev = (ti > 0).astype(F32)
    has_next = (ti < nt - 1).astype(F32)
    prev = (x_ref[0, pl.ds(p0, POOL_HALO), :].astype(F32) * has_prev).astype(BF)
    nxt = (x_ref[0, pl.ds(n0, POOL_HALO), :].astype(F32) * has_next).astype(BF)
    xcat = jnp.concatenate([prev, xm, nxt], axis=0)

    group = _lane_group((tt, BRANCH_W), 6)
    wsum = jnp.zeros((tt, BRANCH_W), F32)
    for g in range(len(POOL_WINDOWS)):
        wsum = jnp.where(group == g, _dot(band_ref[g], xcat), wsum)

    tpos = t0 + lax.broadcasted_iota(jnp.int32, (tt, BRANCH_W), 0)
    half = jnp.left_shift(1, group)
    lo = jnp.maximum(tpos - half, 0)
    hi = jnp.minimum(tpos + half - 1, t_total - 1)
    cnt = (hi - lo + 1).astype(F32)
    dlt = wsum / cnt - xm.astype(F32)
    y = _dot(dlt.astype(BF), cw_ref[...]) * cs_ref[...]
    o_ref[0] = (y * z_ref[0].astype(F32)).astype(BF)


def _pool_band(tt):
    band = np.zeros((len(POOL_WINDOWS), tt, tt + 2 * POOL_HALO), np.float32)
    t = np.arange(tt)[:, None]
    u = np.arange(tt + 2 * POOL_HALO)[None, :] - POOL_HALO
    for g, w in enumerate(POOL_WINDOWS):
        band[g] = ((u >= t - w // 2) & (u <= t - w // 2 + w - 1)).astype(np.float32)
    return jnp.asarray(band, BF)


def _pool(p3, cw_bd, c_scale, tt=256):
    b, t, _ = p3.shape
    band = _pool_band(tt)
    return pl.pallas_call(
        _pool_kernel,
        grid=(b, t // tt),
        in_specs=[
            pl.BlockSpec((1, t, BRANCH_W), lambda i, j: (i, 0, S_CX)),
            pl.BlockSpec((1, tt, BRANCH_W), lambda i, j: (i, j, S_CZ)),
            _resident(band.shape, lambda i, j: (0, 0, 0)),
            _resident((BRANCH_W, BRANCH_W), lambda i, j: (0, 0)),
            _resident((1, BRANCH_W), lambda i, j: (0, 0)),
        ],
        out_specs=pl.BlockSpec((1, tt, BRANCH_W), lambda i, j: (i, j, 0)),
        out_shape=jax.ShapeDtypeStruct((b, t, BRANCH_W), BF),
        compiler_params=_cparams("parallel", "arbitrary"),
        name="pool",
    )(p3, p3, band, cw_bd, c_scale)


def _sg_kernel(u_ref, v_ref, z_ref, ws_ref, bias_ref, o_ref):
    tt = u_ref.shape[0]
    group = _lane_group((SG_CHUNK, BRANCH_W), 6)
    for c in range(tt // SG_CHUNK):
        rows = slice(c * SG_CHUNK, (c + 1) * SG_CHUNK)
        vc = v_ref[rows, :]
        mix = jnp.zeros((SG_CHUNK, BRANCH_W), F32)
        for g in range(SG_GROUPS):
            mix = jnp.where(group == g, _dot(ws_ref[g], vc), mix)
        y = u_ref[rows, :].astype(F32) * (mix + bias_ref[...]) * z_ref[rows, :].astype(F32)
        o_ref[rows, :] = y.astype(BF)


def _sg(p2, ws, sg_bias, tt=512):
    n = p2.shape[0]
    return pl.pallas_call(
        _sg_kernel,
        grid=(n // tt,),
        in_specs=[
            pl.BlockSpec((tt, BRANCH_W), lambda i: (i, S_DU)),
            pl.BlockSpec((tt, BRANCH_W), lambda i: (i, S_DV)),
            pl.BlockSpec((tt, BRANCH_W), lambda i: (i, S_DZ)),
            _resident((SG_GROUPS, SG_CHUNK, SG_CHUNK), lambda i: (0, 0, 0)),
            _resident((SG_CHUNK, BRANCH_W), lambda i: (0, 0)),
        ],
        out_specs=pl.BlockSpec((tt, BRANCH_W), lambda i: (i, 0)),
        out_shape=jax.ShapeDtypeStruct((n, BRANCH_W), BF),
        compiler_params=_cparams("parallel"),
        name="sg",
    )(p2, p2, p2, ws, sg_bias)


def _mem_kernel(q_ref, z_ref, k_ref, v_ref, o_ref):
    tt = q_ref.shape[1]
    q = q_ref[0]
    k = k_ref[0]
    v = v_ref[0]
    head_of_lane = _lane_group((tt, BRANCH_W), 6)
    o = jnp.zeros((tt, BRANCH_W), F32)
    for h in range(MEM_HEADS):
        qm = q * _lane_mask(q.shape, h * MEM_HD, (h + 1) * MEM_HD, BF)
        s = _dot_nt(qm, k)
        m = jnp.max(s, axis=-1, keepdims=True)
        p = jnp.exp(s - m)
        l = jnp.sum(p, axis=-1, keepdims=True)
        oh = _dot(p.astype(BF), v) / l
        o = jnp.where(head_of_lane == h, oh, o)
    o_ref[0] = (o * z_ref[0].astype(F32)).astype(BF)


def _mem_attn(p3, mk, mv, tt=512):
    b, t, _ = p3.shape
    m = mk.shape[1]
    return pl.pallas_call(
        _mem_kernel,
        grid=(b, t // tt),
        in_specs=[
            pl.BlockSpec((1, tt, BRANCH_W), lambda i, j: (i, j, S_MQ)),
            pl.BlockSpec((1, tt, BRANCH_W), lambda i, j: (i, j, S_MZ)),
            pl.BlockSpec((1, m, BRANCH_W), lambda i, j: (i, 0, 0)),
            pl.BlockSpec((1, m, BRANCH_W), lambda i, j: (i, 0, 0)),
        ],
        out_specs=pl.BlockSpec((1, tt, BRANCH_W), lambda i, j: (i, j, 0)),
        out_shape=jax.ShapeDtypeStruct((b, t, BRANCH_W), BF),
        compiler_params=_cparams("parallel", "arbitrary"),
        name="mem",
    )(p3, p3, mk, mv)


def _final_kernel(x_ref, ng_ref, ya_ref, yb_ref, yc_ref, yd_ref, ym_ref,
                  wg_ref, bg_ref, wb_ref, wo_ref, o_ref):
    x = x_ref[...]
    h = _rmsnorm_rows(x, ng_ref[...]).astype(BF)
    merged = jnp.zeros(x.shape, F32)
    for i, y_ref in enumerate((ya_ref, yb_ref, yc_ref, yd_ref, ym_ref)):
        logit = _dot(h, wg_ref[:, i * D_MODEL:(i + 1) * D_MODEL]) + bg_ref[i:i + 1, :]
        merged = merged + _sigmoid(logit) * _dot(y_ref[...], wb_ref[i])
    o_ref[...] = x + _dot(merged.astype(BF), wo_ref[...])


def _final(x2, norm_g, ys, w_gate, b_gate, w_branch, w_out, tm=512):
    n = x2.shape[0]
    yspec = pl.BlockSpec((tm, BRANCH_W), lambda i: (i, 0))
    return pl.pallas_call(
        _final_kernel,
        grid=(n // tm,),
        in_specs=[
            pl.BlockSpec((tm, D_MODEL), lambda i: (i, 0)),
            _resident((1, D_MODEL), lambda i: (0, 0)),
            yspec, yspec, yspec, yspec, yspec,
            _resident((D_MODEL, N_BRANCH * D_MODEL), lambda i: (0, 0)),
            _resident((8, D_MODEL), lambda i: (0, 0)),
            _resident((N_BRANCH, BRANCH_W, D_MODEL), lambda i: (0, 0, 0)),
            _resident((D_MODEL, D_MODEL), lambda i: (0, 0)),
        ],
        out_specs=pl.BlockSpec((tm, D_MODEL), lambda i: (i, 0)),
        out_shape=jax.ShapeDtypeStruct((n, D_MODEL), F32),
        compiler_params=_cparams("parallel"),
        name="final",
    )(x2, norm_g, *ys, w_gate, b_gate, w_branch, w_out)


def _tile4(g):
    return jnp.tile(g.astype(F32), BRANCH_W // g.shape[0])


def _layer(x, mem, layer_idx, norm_g, w_in, b_gate, a_qn_g, a_kn_g, a_rpb,
           b_qn_g, b_kn_g, b_lam_q1, b_lam_k1, b_lam_q2, b_lam_k2, b_sub_g,
           c_w, c_scale, d_ln_g, d_ln_b, d_ws, d_bs,
           m_norm_g, m_wkv, m_qn_g, m_kn_g, w_branch, w_out):
    b, t, d = x.shape
    n = b * t
    x2 = x.reshape(n, d)
    ng = norm_g.reshape(1, d).astype(F32)

    gains = jnp.zeros((8, BRANCH_W), F32)
    gains = gains.at[G_AQ].set(_tile4(a_qn_g) * NA_HD ** -0.5)
    gains = gains.at[G_AK].set(_tile4(a_kn_g))
    gains = gains.at[G_BQ].set(_tile4(b_qn_g) * DA_QK ** -0.5)
    gains = gains.at[G_BK].set(_tile4(b_kn_g))
    gains = gains.at[G_MQ].set(_tile4(m_qn_g) * MEM_HD ** -0.5)
    gains = gains.at[G_LNG].set(d_ln_g.astype(F32))
    gains = gains.at[G_LNB].set(d_ln_b.astype(F32))

    p2 = _proj(x2, ng, w_in[:, :SLAB_COLS].astype(BF), gains)
    p3 = p2.reshape(b, t, SLAB_COLS)

    y_a = _na(p3, _na_bias_table(a_rpb, t // GRID_W))

    lam_init = 0.8 - 0.6 * math.exp(-0.3 * layer_idx)
    q4, kt4, v4 = _dprep(p3)
    lamv = jnp.stack([b_lam_q1, b_lam_k1, b_lam_q2, b_lam_k2]).astype(F32)
    subg = (_tile4(b_sub_g) * (1.0 - lam_init)).reshape(1, BRANCH_W)
    y_b = _da(q4, kt4, v4, p3, lamv, subg, lam_init)

    cw_bd = jnp.zeros((BRANCH_W, BRANCH_W), F32)
    for g in range(len(POOL_WINDOWS)):
        cw_bd = cw_bd.at[g * POOL_GC:(g + 1) * POOL_GC, g * POOL_GC:(g + 1) * POOL_GC].set(c_w[g].astype(F32))
    y_c = _pool(p3, cw_bd.astype(BF), c_scale.reshape(1, BRANCH_W).astype(F32))

    sg_bias = jnp.repeat(d_bs.astype(F32).T, SG_GC, axis=1)
    y_d = _sg(p2, d_ws.astype(BF), sg_bias)

    mk, mv = _memkv(mem, m_norm_g.reshape(1, d).astype(F32), m_wkv.astype(BF),
                    _tile4(m_kn_g).reshape(1, BRANCH_W))
    y_m = _mem_attn(p3, mk, mv)

    bg = jnp.zeros((8, d), F32).at[:N_BRANCH].set(b_gate.astype(F32))
    ys = [y.reshape(n, BRANCH_W) for y in (y_a, y_b, y_c, y_d, y_m)]
    out = _final(x2, ng, ys, w_in[:, SLAB_COLS:].astype(BF), bg,
                 w_branch.astype(BF), w_out.astype(BF))
    return out.reshape(b, t, d)


def kernel(x, mem, norm_g, w_in, b_gate, a_qn_g, a_kn_g, a_rpb, b_qn_g, b_kn_g, b_lam_q1, b_lam_k1, b_lam_q2, b_lam_k2, b_sub_g, c_w, c_scale, d_ln_g, d_ln_b, d_ws, d_bs, m_norm_g, m_wkv, m_qn_g, m_kn_g, w_branch, w_out):
    for l in range(DEPTH):
        x = _layer(x, mem, l, norm_g[l], w_in[l], b_gate[l], a_qn_g[l], a_kn_g[l], a_rpb[l],
                   b_qn_g[l], b_kn_g[l], b_lam_q1[l], b_lam_k1[l], b_lam_q2[l], b_lam_k2[l], b_sub_g[l],
                   c_w[l], c_scale[l], d_ln_g[l], d_ln_b[l], d_ws[l], d_bs[l],
                   m_norm_g[l], m_wkv[l], m_qn_g[l], m_kn_g[l], w_branch[l], w_out[l])
    return x
```

```python
import functools
import math

import numpy as np
import jax
import jax.numpy as jnp
from jax import lax
from jax.experimental import pallas as pl
from jax.experimental.pallas import tpu as pltpu

F32 = jnp.float32
BF = jnp.bfloat16

D_MODEL = 1024
DEPTH = 2
GRID_W = 64
BRANCH_W = 256
N_BRANCH = 5
NA_HEADS = 4
NA_HD = 64
NA_KH = 8
NA_KW = 16
DA_HEADS = 4
DA_QK = 32
DA_V = 64
ALIBI_BASE = 8.0
POOL_WINDOWS = (2, 4, 8, 16)
POOL_GC = 64
SG_CHUNK = 128
SG_GROUPS = 4
SG_GC = 64
MEM_HEADS = 4
MEM_HD = 64
N_SLABS = 15
SLAB_COLS = N_SLABS * BRANCH_W
EPS = 1e-6

(S_AQ, S_AK, S_AV, S_AZ, S_BQ, S_BK, S_BV, S_BZ, S_CX, S_CZ,
 S_DU, S_DV, S_DZ, S_MQ, S_MZ) = range(N_SLABS)

(G_AQ, G_AK, G_BQ, G_BK, G_MQ, G_LNG, G_LNB) = range(7)

VMEM_LIMIT_BYTES = 56 * 1024 * 1024
NEG_BIG = -1e30

ALIBI_SLOPES = tuple(2.0 ** (-ALIBI_BASE * (h + 1) / DA_HEADS) for h in range(DA_HEADS))
X_HI, X_LO, X_ONE_A, X_ONE_B = 64, 65, 66, 67
DA_LANES = 128


def _cparams(*sem):
    return pltpu.CompilerParams(dimension_semantics=sem, vmem_limit_bytes=VMEM_LIMIT_BYTES)


def _resident(shape, index_map):
    return pl.BlockSpec(shape, index_map, pipeline_mode=pl.Buffered(1))


def _dot(a, b):
    return jnp.dot(a, b, preferred_element_type=F32)


def _dot_nt(a, b):
    return lax.dot_general(a, b, (((1,), (1,)), ((), ())), preferred_element_type=F32)


def _lane_group(shape, log2_group):
    return lax.broadcasted_iota(jnp.int32, shape, len(shape) - 1) >> log2_group


def _lane_mask(shape, lo, hi, dtype):
    lane = lax.broadcasted_iota(jnp.int32, shape, len(shape) - 1)
    return jnp.where((lane >= lo) & (lane < hi), 1.0, 0.0).astype(dtype)


def _group_mean_sq(xf, group):
    n = xf.shape[-1]
    sh = int(math.log2(group))
    r = lax.broadcasted_iota(jnp.int32, (n, n), 0) >> sh
    c = lax.broadcasted_iota(jnp.int32, (n, n), 1) >> sh
    pm = jnp.where(r == c, 1.0 / group, 0.0).astype(BF)
    xx = xf * xf
    hi = xx.astype(BF)
    lo = (xx - hi.astype(F32)).astype(BF)
    return _dot(hi, pm) + _dot(lo, pm)


def _sigmoid(x):
    return 1.0 / (1.0 + jnp.exp(-x))


def _silu(x):
    return x * _sigmoid(x)


def _gelu_tanh(x):
    return 0.5 * x * (1.0 + jnp.tanh(math.sqrt(2.0 / math.pi) * (x + 0.044715 * (x * x * x))))


def _rmsnorm_rows(x, g):
    ms = jnp.mean(x * x, axis=-1, keepdims=True)
    return x * lax.rsqrt(ms + EPS) * g


def _proj_kernel(x_ref, ng_ref, w_ref, gains_ref, o_ref):
    h = _rmsnorm_rows(x_ref[...], ng_ref[...]).astype(BF)

    def gnorm(r, group, row):
        return r * lax.rsqrt(_group_mean_sq(r, group) + EPS) * gains_ref[row:row + 1, :]

    for j in range(N_SLABS):
        r = _dot(h, w_ref[:, j * BRANCH_W:(j + 1) * BRANCH_W])
        if j == S_AQ:
            r = gnorm(r, NA_HD, G_AQ)
        elif j == S_AK:
            r = gnorm(r, NA_HD, G_AK)
        elif j == S_BQ:
            r = gnorm(r, DA_QK, G_BQ)
        elif j == S_BK:
            r = gnorm(r, DA_QK, G_BK)
        elif j == S_MQ:
            r = gnorm(r, MEM_HD, G_MQ)
        elif j in (S_AZ, S_BZ, S_CZ, S_DZ, S_MZ):
            r = _silu(r)
        elif j == S_DU:
            r = _gelu_tanh(r)
        elif j == S_DV:
            v = _gelu_tanh(r)
            mu = jnp.mean(v, axis=-1, keepdims=True)
            vc = v - mu
            var = jnp.mean(vc * vc, axis=-1, keepdims=True)
            r = vc * lax.rsqrt(var + EPS) * gains_ref[G_LNG:G_LNG + 1, :] + gains_ref[G_LNB:G_LNB + 1, :]
        o_ref[:, j * BRANCH_W:(j + 1) * BRANCH_W] = r.astype(BF)


def _proj(x2, norm_g, w_slab, gains, tm=512):
    n = x2.shape[0]
    return pl.pallas_call(
        _proj_kernel,
        grid=(n // tm,),
        in_specs=[
            pl.BlockSpec((tm, D_MODEL), lambda i: (i, 0)),
            _resident((1, D_MODEL), lambda i: (0, 0)),
            _resident((D_MODEL, SLAB_COLS), lambda i: (0, 0)),
            _resident((8, BRANCH_W), lambda i: (0, 0)),
        ],
        out_specs=pl.BlockSpec((tm, SLAB_COLS), lambda i: (i, 0)),
        out_shape=jax.ShapeDtypeStruct((n, SLAB_COLS), BF),
        compiler_params=_cparams("parallel"),
        name="proj",
    )(x2, norm_g, w_slab, gains)


def _memkv_kernel(mem_ref, g_ref, w_ref, kg_ref, k_ref, v_ref):
    h = _rmsnorm_rows(mem_ref[0], g_ref[...]).astype(BF)
    k = _dot(h, w_ref[:, :BRANCH_W])
    v = _dot(h, w_ref[:, BRANCH_W:])
    k = k * lax.rsqrt(_group_mean_sq(k, MEM_HD) + EPS) * kg_ref[...]
    k_ref[0] = k.astype(BF)
    v_ref[0] = v.astype(BF)


def _memkv(mem, m_norm_g, m_wkv, kg):
    b, m, _ = mem.shape
    return pl.pallas_call(
        _memkv_kernel,
        grid=(b,),
        in_specs=[
            pl.BlockSpec((1, m, D_MODEL), lambda i: (i, 0, 0)),
            _resident((1, D_MODEL), lambda i: (0, 0)),
            _resident((D_MODEL, 2 * BRANCH_W), lambda i: (0, 0)),
            _resident((1, BRANCH_W), lambda i: (0, 0)),
        ],
        out_specs=[pl.BlockSpec((1, m, BRANCH_W), lambda i: (i, 0, 0)),
                   pl.BlockSpec((1, m, BRANCH_W), lambda i: (i, 0, 0))],
        out_shape=[jax.ShapeDtypeStruct((b, m, BRANCH_W), BF)] * 2,
        compiler_params=_cparams("parallel"),
        name="memkv",
    )(mem, m_norm_g, m_wkv, kg)


NA_ROWS_PER_STEP = 8
NA_WIN = NA_KH * GRID_W


def _na_kernel(q_ref, k_ref, v_ref, z_ref, bias_ref, o_ref):
    n_rows = k_ref.shape[1] // GRID_W
    rb = pl.program_id(1)
    head_of_lane = _lane_group((GRID_W, BRANCH_W), 6)

    def row(rr, carry):
        r = rb * NA_ROWS_PER_STEP + rr
        rs = jnp.clip(r - NA_KH // 2, 0, n_rows - NA_KH)
        cls = jnp.where(r < NA_KH // 2, r, jnp.where(r > n_rows - NA_KH // 2, r - (n_rows - NA_KH), NA_KH // 2))
        w0 = pl.multiple_of(rs * GRID_W, GRID_W)
        kwin = k_ref[0, pl.ds(w0, NA_WIN), :]
        vwin = v_ref[0, pl.ds(w0, NA_WIN), :]
        t0 = pl.multiple_of(rr * GRID_W, GRID_W)
        q = q_ref[0, pl.ds(t0, GRID_W), :]
        o = jnp.zeros((GRID_W, BRANCH_W), F32)
        for h in range(NA_HEADS):
            qm = q * _lane_mask(q.shape, h * NA_HD, (h + 1) * NA_HD, BF)
            s = _dot_nt(qm, kwin) + bias_ref[cls, h]
            m = jnp.max(s, axis=-1, keepdims=True)
            p = jnp.exp(s - m)
            l = jnp.sum(p, axis=-1, keepdims=True)
            oh = _dot(p.astype(BF), vwin) / l
            o = jnp.where(head_of_lane == h, oh, o)
        y = o * z_ref[0, pl.ds(t0, GRID_W), :].astype(F32)
        o_ref[0, pl.ds(t0, GRID_W), :] = y.astype(BF)
        return carry

    lax.fori_loop(0, NA_ROWS_PER_STEP, row, 0)


def _na(p3, bias):
    b, t, _ = p3.shape
    tq = NA_ROWS_PER_STEP * GRID_W
    return pl.pallas_call(
        _na_kernel,
        grid=(b, t // tq),
        in_specs=[
            pl.BlockSpec((1, tq, BRANCH_W), lambda i, j: (i, j, S_AQ)),
            pl.BlockSpec((1, t, BRANCH_W), lambda i, j: (i, 0, S_AK)),
            pl.BlockSpec((1, t, BRANCH_W), lambda i, j: (i, 0, S_AV)),
            pl.BlockSpec((1, tq, BRANCH_W), lambda i, j: (i, j, S_AZ)),
            _resident(bias.shape, lambda i, j: (0, 0, 0, 0)),
        ],
        out_specs=pl.BlockSpec((1, tq, BRANCH_W), lambda i, j: (i, j, 0)),
        out_shape=jax.ShapeDtypeStruct((b, t, BRANCH_W), BF),
        compiler_params=_cparams("parallel", "arbitrary"),
        name="na",
    )(p3, p3, p3, p3, bias)


NA_REL_ROWS = 2 * NA_KH - 1
NA_REL_COLS = 2 * NA_KW - 1
NA_CLASSES = NA_KH


def _na_class_roff(n_rows):
    half = NA_KH // 2
    rows = np.array(list(range(half)) + [half] + [n_rows - half + 1 + i for i in range(half - 1)])
    rs = np.clip(rows - half, 0, n_rows - NA_KH)
    return [int(v) for v in rs - rows + NA_KH - 1]


def _nabias_kernel(roff0, rpb_ref, o_ref, f_scr):
    h = pl.program_id(0)
    shape = (GRID_W, 2 * GRID_W)
    q = lax.broadcasted_iota(jnp.int32, shape, 0)
    lane = lax.broadcasted_iota(jnp.int32, shape, 1)
    k = lane & (GRID_W - 1)
    coff = k - q + (NA_KW - 1)
    cs = jnp.clip(q - NA_KW // 2, 0, GRID_W - NA_KW)
    valid = (k >= cs) & (k < cs + NA_KW)
    first = lane < GRID_W

    def tile(r, carry):
        base = (h * NA_REL_ROWS + r) * NA_REL_COLS
        acc = jnp.zeros(shape, F32)
        for s in range(NA_REL_COLS):
            acc = jnp.where(coff == s, jnp.where(first, rpb_ref[base + s], rpb_ref[base + NA_REL_COLS + s]), acc)
        f_scr[r] = jnp.where(valid, acc, NEG_BIG)
        return carry

    lax.fori_loop(0, NA_REL_ROWS - 1, tile, 0)
    for c in range(NA_CLASSES):
        for p in range(NA_KH // 2):
            o_ref[c, 0, :, p * 2 * GRID_W:(p + 1) * 2 * GRID_W] = f_scr[roff0[c] + 2 * p]


def _na_bias_table(rpb, n_rows):
    return pl.pallas_call(
        functools.partial(_nabias_kernel, _na_class_roff(n_rows)),
        grid=(NA_HEADS,),
        in_specs=[pl.BlockSpec(memory_space=pltpu.SMEM)],
        out_specs=pl.BlockSpec((NA_CLASSES, 1, GRID_W, NA_WIN), lambda h: (0, h, 0, 0)),
        out_shape=jax.ShapeDtypeStruct((NA_CLASSES, NA_HEADS, GRID_W, NA_WIN), F32),
        scratch_shapes=[pltpu.VMEM((NA_REL_ROWS - 1, GRID_W, 2 * GRID_W), F32)],
        compiler_params=_cparams("arbitrary"),
        name="nabias",
    )(rpb.astype(F32).reshape(-1))


def _dprep_kernel(q_ref, k_ref, v_ref, q4_ref, kt_ref, v4_ref):
    tt = q_ref.shape[1]
    t0 = pl.program_id(1) * tt
    q = q_ref[0]
    k = k_ref[0]
    v = v_ref[0]
    src = lax.broadcasted_iota(jnp.int32, (BRANCH_W, DA_LANES), 0)
    dst = lax.broadcasted_iota(jnp.int32, (BRANCH_W, DA_LANES), 1)
    src_t = lax.broadcasted_iota(jnp.int32, (DA_LANES, BRANCH_W), 1)
    dst_t = lax.broadcasted_iota(jnp.int32, (DA_LANES, BRANCH_W), 0)
    krow = lax.broadcasted_iota(jnp.int32, (DA_LANES, tt), 0)
    kpos = t0 + lax.broadcasted_iota(jnp.int32, (DA_LANES, tt), 1)
    for h in range(DA_HEADS):
        sel = jnp.where((src == h * DA_V + dst) & (dst < DA_V), 1.0, 0.0).astype(BF)
        sel_t = jnp.where((src_t == h * DA_V + dst_t) & (dst_t < DA_V), 1.0, 0.0).astype(BF)
        q4_ref[0, h] = _dot(q, sel).astype(BF)
        v4_ref[0, h] = _dot(v, sel).astype(BF)
        kt = _dot_nt(sel_t, k)
        slope = ALIBI_SLOPES[h]
        kt = jnp.where(krow == X_HI, (kpos >> 6).astype(F32) * (slope * 64.0), kt)
        kt = jnp.where(krow == X_LO, (kpos & 63).astype(F32) * slope, kt)
        kt = jnp.where((krow == X_ONE_A) | (krow == X_ONE_B), 1.0, kt)
        kt_ref[0, h] = kt.astype(BF)


def _dprep(p3, tt=512):
    b, t, _ = p3.shape
    return pl.pallas_call(
        _dprep_kernel,
        grid=(b, t // tt),
        in_specs=[
            pl.BlockSpec((1, tt, BRANCH_W), lambda i, j: (i, j, S_BQ)),
            pl.BlockSpec((1, tt, BRANCH_W), lambda i, j: (i, j, S_BK)),
            pl.BlockSpec((1, tt, BRANCH_W), lambda i, j: (i, j, S_BV)),
        ],
        out_specs=[
            pl.BlockSpec((1, DA_HEADS, tt, DA_LANES), lambda i, j: (i, 0, j, 0)),
            pl.BlockSpec((1, DA_HEADS, DA_LANES, tt), lambda i, j: (i, 0, 0, j)),
            pl.BlockSpec((1, DA_HEADS, tt, DA_LANES), lambda i, j: (i, 0, j, 0)),
        ],
        out_shape=[
            jax.ShapeDtypeStruct((b, DA_HEADS, t, DA_LANES), BF),
            jax.ShapeDtypeStruct((b, DA_HEADS, DA_LANES, t), BF),
            jax.ShapeDtypeStruct((b, DA_HEADS, t, DA_LANES), BF),
        ],
        compiler_params=_cparams("parallel", "parallel"),
        name="dprep",
    )(p3, p3, p3)


def _da_kernel(lam_init, q_ref, kt_ref, v_ref, z_ref, lamv_ref, subg_ref, o_ref, m_scr, l_scr, acc_scr):
    bq = q_ref.shape[2]
    bk = bq
    nk = kt_ref.shape[3] // bk
    qi = pl.program_id(1)
    lane = lax.broadcasted_iota(jnp.int32, (bq, DA_LANES), 1)
    qpos = qi * bq + lax.broadcasted_iota(jnp.int32, (bq, DA_LANES), 0)
    rel = (lax.broadcasted_iota(jnp.int32, (bq, bk), 0) - lax.broadcasted_iota(jnp.int32, (bq, bk), 1))
    dist = jnp.abs(rel).astype(F32)

    lv = lamv_ref[...]
    lam = (jnp.exp(jnp.sum(lv[0:1] * lv[1:2], axis=-1, keepdims=True))
           - jnp.exp(jnp.sum(lv[2:3] * lv[3:4], axis=-1, keepdims=True)) + lam_init)

    def step(h, kb, qv, diag_bias):
        k0 = pl.multiple_of(kb * bk, bk)
        kt = kt_ref[0, h, :, pl.ds(k0, bk)]
        vv = v_ref[0, h, pl.ds(k0, bk), :]
        for m in range(2):
            s = _dot(qv[m], kt)
            if diag_bias is not None:
                s = s - diag_bias
            m_old = m_scr[m]
            m_new = jnp.maximum(m_old, jnp.max(s, axis=-1, keepdims=True))
            alpha = jnp.exp(m_old - m_new)
            p = jnp.exp(s - m_new)
            l_scr[m] = alpha * l_scr[m] + jnp.sum(p, axis=-1, keepdims=True)
            acc_scr[m] = alpha * acc_scr[m] + _dot(p.astype(BF), vv)
            m_scr[m] = m_new

    heads = []
    for h in range(DA_HEADS):
        slope = ALIBI_SLOPES[h]
        qb = q_ref[0, h]
        qm = [qb * _lane_mask(qb.shape, m * DA_QK, (m + 1) * DA_QK, BF) for m in range(2)]
        ih = (qpos >> 6).astype(F32) * (slope * 64.0)
        il = (qpos & 63).astype(F32) * slope
        ex = jnp.where((lane == X_HI) | (lane == X_LO), 1.0,
                       jnp.where(lane == X_ONE_A, -ih, jnp.where(lane == X_ONE_B, -il, 0.0))).astype(BF)
        q_left = [qm[m] + ex for m in range(2)]
        q_right = [qm[m] - ex for m in range(2)]

        m_scr[...] = jnp.full(m_scr.shape, NEG_BIG, F32)
        l_scr[...] = jnp.zeros(l_scr.shape, F32)
        acc_scr[...] = jnp.zeros(acc_scr.shape, F32)

        def left(kb, c, h=h, qv=q_left):
            step(h, kb, qv, None)
            return c

        def right(kb, c, h=h, qv=q_right):
            step(h, kb, qv, None)
            return c

        lax.fori_loop(0, qi, left, 0)
        step(h, qi, qm, dist * slope)
        lax.fori_loop(qi + 1, nk, right, 0)

        o0 = acc_scr[0] / l_scr[0]
        o1 = acc_scr[1] / l_scr[1]
        heads.append(o0 - lam * o1)

    low = lane < DA_V
    full = jnp.concatenate(
        [jnp.where(low, heads[0], pltpu.roll(heads[1], DA_V, 1)),
         jnp.where(low, heads[2], pltpu.roll(heads[3], DA_V, 1))], axis=-1)
    y = full * lax.rsqrt(_group_mean_sq(full, DA_V) + EPS) * subg_ref[...]
    o_ref[0] = (y * z_ref[0].astype(F32)).astype(BF)


def _da(q4, kt4, v4, p3, lamv, subg, lam_init, bq=512):
    b, _, t, _ = q4.shape
    return pl.pallas_call(
        functools.partial(_da_kernel, lam_init),
        grid=(b, t // bq),
        in_specs=[
            pl.BlockSpec((1, DA_HEADS, bq, DA_LANES), lambda i, j: (i, 0, j, 0)),
            pl.BlockSpec((1, DA_HEADS, DA_LANES, t), lambda i, j: (i, 0, 0, 0)),
            pl.BlockSpec((1, DA_HEADS, t, DA_LANES), lambda i, j: (i, 0, 0, 0)),
            pl.BlockSpec((1, bq, BRANCH_W), lambda i, j: (i, j, S_BZ)),
            _resident((4, DA_QK), lambda i, j: (0, 0)),
            _resident((1, BRANCH_W), lambda i, j: (0, 0)),
        ],
        out_specs=pl.BlockSpec((1, bq, BRANCH_W), lambda i, j: (i, j, 0)),
        out_shape=jax.ShapeDtypeStruct((b, t, BRANCH_W), BF),
        scratch_shapes=[
            pltpu.VMEM((2, bq, 1), F32),
            pltpu.VMEM((2, bq, 1), F32),
            pltpu.VMEM((2, bq, DA_LANES), F32),
        ],
        compiler_params=_cparams("parallel", "arbitrary"),
        name="da",
    )(q4, kt4, v4, p3, lamv, subg)


POOL_HALO = 16


def _pool_kernel(x_ref, z_ref, band_ref, cw_ref, cs_ref, o_ref):
    tt = z_ref.shape[1]
    t_total = x_ref.shape[1]
    nt = t_total // tt
    ti = pl.program_id(1)
    t0 = pl.multiple_of(ti * tt, tt)
    xm = x_ref[0, pl.ds(t0, tt), :]
    p0 = pl.multiple_of(jnp.maximum(t0 - POOL_HALO, 0), POOL_HALO)
    n0 = pl.multiple_of(jnp.minimum(t0 + tt, t_total - POOL_HALO), POOL_HALO)
    has_prev = (ti > 0).astype(F32)
    has_next = (ti < nt - 1).astype(F32)
    prev = (x_ref[0, pl.ds(p0, POOL_HALO), :].astype(F32) * has_prev).astype(BF)
    nxt = (x_ref[0, pl.ds(n0, POOL_HALO), :].astype(F32) * has_next).astype(BF)
    xcat = jnp.concatenate([prev, xm, nxt], axis=0)

    group = _lane_group((tt, BRANCH_W), 6)
    wsum = jnp.zeros((tt, BRANCH_W), F32)
    for g in range(len(POOL_WINDOWS)):
        wsum = jnp.where(group == g, _dot(band_ref[g], xcat), wsum)

    tpos = t0 + lax.broadcasted_iota(jnp.int32, (tt, BRANCH_W), 0)
    half = jnp.left_shift(1, group)
    lo = jnp.maximum(tpos - half, 0)
    hi = jnp.minimum(tpos + half - 1, t_total - 1)
    cnt = (hi - lo + 1).astype(F32)
    dlt = wsum / cnt - xm.astype(F32)
    y = _dot(dlt.astype(BF), cw_ref[...]) * cs_ref[...]
    o_ref[0] = (y * z_ref[0].astype(F32)).astype(BF)


def _pool_band(tt):
    band = np.zeros((len(POOL_WINDOWS), tt, tt + 2 * POOL_HALO), np.float32)
    t = np.arange(tt)[:, None]
    u = np.arange(tt + 2 * POOL_HALO)[None, :] - POOL_HALO
    for g, w in enumerate(POOL_WINDOWS):
        band[g] = ((u >= t - w // 2) & (u <= t - w // 2 + w - 1)).astype(np.float32)
    return jnp.asarray(band, BF)


def _pool(p3, cw_bd, c_scale, tt=256):
    b, t, _ = p3.shape
    band = _pool_band(tt)
    return pl.pallas_call(
        _pool_kernel,
        grid=(b, t // tt),
        in_specs=[
            pl.BlockSpec((1, t, BRANCH_W), lambda i, j: (i, 0, S_CX)),
            pl.BlockSpec((1, tt, BRANCH_W), lambda i, j: (i, j, S_CZ)),
            _resident(band.shape, lambda i, j: (0, 0, 0)),
            _resident((BRANCH_W, BRANCH_W), lambda i, j: (0, 0)),
            _resident((1, BRANCH_W), lambda i, j: (0, 0)),
        ],
        out_specs=pl.BlockSpec((1, tt, BRANCH_W), lambda i, j: (i, j, 0)),
        out_shape=jax.ShapeDtypeStruct((b, t, BRANCH_W), BF),
        compiler_params=_cparams("parallel", "arbitrary"),
        name="pool",
    )(p3, p3, band, cw_bd, c_scale)


def _sg_kernel(u_ref, v_ref, z_ref, ws_ref, bias_ref, o_ref):
    tt = u_ref.shape[0]
    group = _lane_group((SG_CHUNK, BRANCH_W), 6)
    for c in range(tt // SG_CHUNK):
        rows = slice(c * SG_CHUNK, (c + 1) * SG_CHUNK)
        vc = v_ref[rows, :]
        mix = jnp.zeros((SG_CHUNK, BRANCH_W), F32)
        for g in range(SG_GROUPS):
            mix = jnp.where(group == g, _dot(ws_ref[g], vc), mix)
        y = u_ref[rows, :].astype(F32) * (mix + bias_ref[...]) * z_ref[rows, :].astype(F32)
        o_ref[rows, :] = y.astype(BF)


def _sg(p2, ws, sg_bias, tt=512):
    n = p2.shape[0]
    return pl.pallas_call(
        _sg_kernel,
        grid=(n // tt,),
        in_specs=[
            pl.BlockSpec((tt, BRANCH_W), lambda i: (i, S_DU)),
            pl.BlockSpec((tt, BRANCH_W), lambda i: (i, S_DV)),
            pl.BlockSpec((tt, BRANCH_W), lambda i: (i, S_DZ)),
            _resident((SG_GROUPS, SG_CHUNK, SG_CHUNK), lambda i: (0, 0, 0)),
            _resident((SG_CHUNK, BRANCH_W), lambda i: (0, 0)),
        ],
        out_specs=pl.BlockSpec((tt, BRANCH_W), lambda i: (i, 0)),
        out_shape=jax.ShapeDtypeStruct((n, BRANCH_W), BF),
        compiler_params=_cparams("parallel"),
        name="sg",
    )(p2, p2, p2, ws, sg_bias)


def _mem_kernel(q_ref, z_ref, k_ref, v_ref, o_ref):
    tt = q_ref.shape[1]
    q = q_ref[0]
    k = k_ref[0]
    v = v_ref[0]
    head_of_lane = _lane_group((tt, BRANCH_W), 6)
    o = jnp.zeros((tt, BRANCH_W), F32)
    for h in range(MEM_HEADS):
        qm = q * _lane_mask(q.shape, h * MEM_HD, (h + 1) * MEM_HD, BF)
        s = _dot_nt(qm, k)
        m = jnp.max(s, axis=-1, keepdims=True)
        p = jnp.exp(s - m)
        l = jnp.sum(p, axis=-1, keepdims=True)
        oh = _dot(p.astype(BF), v) / l
        o = jnp.where(head_of_lane == h, oh, o)
    o_ref[0] = (o * z_ref[0].astype(F32)).astype(BF)


def _mem_attn(p3, mk, mv, tt=512):
    b, t, _ = p3.shape
    m = mk.shape[1]
    return pl.pallas_call(
        _mem_kernel,
        grid=(b, t // tt),
        in_specs=[
            pl.BlockSpec((1, tt, BRANCH_W), lambda i, j: (i, j, S_MQ)),
            pl.BlockSpec((1, tt, BRANCH_W), lambda i, j: (i, j, S_MZ)),
            pl.BlockSpec((1, m, BRANCH_W), lambda i, j: (i, 0, 0)),
            pl.BlockSpec((1, m, BRANCH_W), lambda i, j: (i, 0, 0)),
        ],
        out_specs=pl.BlockSpec((1, tt, BRANCH_W), lambda i, j: (i, j, 0)),
        out_shape=jax.ShapeDtypeStruct((b, t, BRANCH_W), BF),
        compiler_params=_cparams("parallel", "arbitrary"),
        name="mem",
    )(p3, p3, mk, mv)


def _final_kernel(x_ref, ng_ref, ya_ref, yb_ref, yc_ref, yd_ref, ym_ref,
                  wg_ref, bg_ref, wb_ref, wo_ref, o_ref):
    x = x_ref[...]
    h = _rmsnorm_rows(x, ng_ref[...]).astype(BF)
    merged = jnp.zeros(x.shape, F32)
    for i, y_ref in enumerate((ya_ref, yb_ref, yc_ref, yd_ref, ym_ref)):
        logit = _dot(h, wg_ref[:, i * D_MODEL:(i + 1) * D_MODEL]) + bg_ref[i:i + 1, :]
        merged = merged + _sigmoid(logit) * _dot(y_ref[...], wb_ref[i])
    o_ref[...] = x + _dot(merged.astype(BF), wo_ref[...])


def _final(x2, norm_g, ys, w_gate, b_gate, w_branch, w_out, tm=512):
    n = x2.shape[0]
    yspec = pl.BlockSpec((tm, BRANCH_W), lambda i: (i, 0))
    return pl.pallas_call(
        _final_kernel,
        grid=(n // tm,),
        in_specs=[
            pl.BlockSpec((tm, D_MODEL), lambda i: (i, 0)),
            _resident((1, D_MODEL), lambda i: (0, 0)),
            yspec, yspec, yspec, yspec, yspec,
            _resident((D_MODEL, N_BRANCH * D_MODEL), lambda i: (0, 0)),
            _resident((8, D_MODEL), lambda i: (0, 0)),
            _resident((N_BRANCH, BRANCH_W, D_MODEL), lambda i: (0, 0, 0)),
            _resident((D_MODEL, D_MODEL), lambda i: (0, 0)),
        ],
        out_specs=pl.BlockSpec((tm, D_MODEL), lambda i: (i, 0)),
        out_shape=jax.ShapeDtypeStruct((n, D_MODEL), F32),
        compiler_params=_cparams("parallel"),
        name="final",
    )(x2, norm_g, *ys, w_gate, b_gate, w_branch, w_out)


def _tile4(g):
    return jnp.tile(g.astype(F32), BRANCH_W // g.shape[0])


def _layer(x, mem, layer_idx, norm_g, w_in, b_gate, a_qn_g, a_kn_g, a_rpb,
           b_qn_g, b_kn_g, b_lam_q1, b_lam_k1, b_lam_q2, b_lam_k2, b_sub_g,
           c_w, c_scale, d_ln_g, d_ln_b, d_ws, d_bs,
           m_norm_g, m_wkv, m_qn_g, m_kn_g, w_branch, w_out):
    b, t, d = x.shape
    n = b * t
    x2 = x.reshape(n, d)
    ng = norm_g.reshape(1, d).astype(F32)

    gains = jnp.zeros((8, BRANCH_W), F32)
    gains = gains.at[G_AQ].set(_tile4(a_qn_g) * NA_HD ** -0.5)
    gains = gains.at[G_AK].set(_tile4(a_kn_g))
    gains = gains.at[G_BQ].set(_tile4(b_qn_g) * DA_QK ** -0.5)
    gains = gains.at[G_BK].set(_tile4(b_kn_g))
    gains = gains.at[G_MQ].set(_tile4(m_qn_g) * MEM_HD ** -0.5)
    gains = gains.at[G_LNG].set(d_ln_g.astype(F32))
    gains = gains.at[G_LNB].set(d_ln_b.astype(F32))

    p2 = _proj(x2, ng, w_in[:, :SLAB_COLS].astype(BF), gains)
    p3 = p2.reshape(b, t, SLAB_COLS)

    y_a = _na(p3, _na_bias_table(a_rpb, t // GRID_W))

    lam_init = 0.8 - 0.6 * math.exp(-0.3 * layer_idx)
    q4, kt4, v4 = _dprep(p3)
    lamv = jnp.stack([b_lam_q1, b_lam_k1, b_lam_q2, b_lam_k2]).astype(F32)
    subg = (_tile4(b_sub_g) * (1.0 - lam_init)).reshape(1, BRANCH_W)
    y_b = _da(q4, kt4, v4, p3, lamv, subg, lam_init)

    cw_bd = jnp.zeros((BRANCH_W, BRANCH_W), F32)
    for g in range(len(POOL_WINDOWS)):
        cw_bd = cw_bd.at[g * POOL_GC:(g + 1) * POOL_GC, g * POOL_GC:(g + 1) * POOL_GC].set(c_w[g].astype(F32))
    y_c = _pool(p3, cw_bd.astype(BF), c_scale.reshape(1, BRANCH_W).astype(F32))

    sg_bias = jnp.repeat(d_bs.astype(F32).T, SG_GC, axis=1)
    y_d = _sg(p2, d_ws.astype(BF), sg_bias)

    mk, mv = _memkv(mem, m_norm_g.reshape(1, d).astype(F32), m_wkv.astype(BF),
                    _tile4(m_kn_g).reshape(1, BRANCH_W))
    y_m = _mem_attn(p3, mk, mv)

    bg = jnp.zeros((8, d), F32).at[:N_BRANCH].set(b_gate.astype(F32))
    ys = [y.reshape(n, BRANCH_W) for y in (y_a, y_b, y_c, y_d, y_m)]
    out = _final(x2, ng, ys, w_in[:, SLAB_COLS:].astype(BF), bg,
                 w_branch.astype(BF), w_out.astype(BF))
    return out.reshape(b, t, d)


def kernel(x, mem, norm_g, w_in, b_gate, a_qn_g, a_kn_g, a_rpb, b_qn_g, b_kn_g, b_lam_q1, b_lam_k1, b_lam_q2, b_lam_k2, b_sub_g, c_w, c_scale, d_ln_g, d_ln_b, d_ws, d_bs, m_norm_g, m_wkv, m_qn_g, m_kn_g, w_branch, w_out):
    for l in range(DEPTH):
        x = _layer(x, mem, l, norm_g[l], w_in[l], b_gate[l], a_qn_g[l], a_kn_g[l], a_rpb[l],
                   b_qn_g[l], b_kn_g[l], b_lam_q1[l], b_lam_k1[l], b_lam_q2[l], b_lam_k2[l], b_sub_g[l],
                   c_w[l], c_scale[l], d_ln_g[l], d_ln_b[l], d_ws[l], d_bs[l],
                   m_norm_g[l], m_wkv[l], m_qn_g[l], m_kn_g[l], w_branch[l], w_out[l])
    return x
```

```python
import functools
import math

import numpy as np
import jax
import jax.numpy as jnp
from jax import lax
from jax.experimental import pallas as pl
from jax.experimental.pallas import tpu as pltpu

F32 = jnp.float32
BF = jnp.bfloat16

D_MODEL = 1024
DEPTH = 2
GRID_W = 64
BRANCH_W = 256
N_BRANCH = 5
NA_HEADS = 4
NA_HD = 64
NA_KH = 8
NA_KW = 16
DA_HEADS = 4
DA_QK = 32
DA_V = 64
ALIBI_BASE = 8.0
POOL_WINDOWS = (2, 4, 8, 16)
POOL_GC = 64
SG_CHUNK = 128
SG_GROUPS = 4
SG_GC = 64
MEM_HEADS = 4
MEM_HD = 64
N_SLABS = 15
SLAB_COLS = N_SLABS * BRANCH_W
EPS = 1e-6

(S_AQ, S_AK, S_AV, S_AZ, S_BQ, S_BK, S_BV, S_BZ, S_CX, S_CZ,
 S_DU, S_DV, S_DZ, S_MQ, S_MZ) = range(N_SLABS)

(G_AQ, G_AK, G_BQ, G_BK, G_MQ, G_LNG, G_LNB) = range(7)

VMEM_LIMIT_BYTES = 56 * 1024 * 1024
NEG_BIG = -1e30

ALIBI_SLOPES = tuple(2.0 ** (-ALIBI_BASE * (h + 1) / DA_HEADS) for h in range(DA_HEADS))
X_HI, X_LO, X_ONE_A, X_ONE_B = 64, 65, 66, 67
DA_LANES = 128


def _cparams(*sem):
    return pltpu.CompilerParams(dimension_semantics=sem, vmem_limit_bytes=VMEM_LIMIT_BYTES)


def _resident(shape, index_map):
    return pl.BlockSpec(shape, index_map, pipeline_mode=pl.Buffered(1))


def _dot(a, b):
    return jnp.dot(a, b, preferred_element_type=F32)


def _dot_nt(a, b):
    return lax.dot_general(a, b, (((1,), (1,)), ((), ())), preferred_element_type=F32)


def _lane_group(shape, log2_group):
    return lax.broadcasted_iota(jnp.int32, shape, len(shape) - 1) >> log2_group


def _lane_mask(shape, lo, hi, dtype):
    lane = lax.broadcasted_iota(jnp.int32, shape, len(shape) - 1)
    return jnp.where((lane >= lo) & (lane < hi), 1.0, 0.0).astype(dtype)


def _group_mean_sq(xf, group):
    n = xf.shape[-1]
    sh = int(math.log2(group))
    r = lax.broadcasted_iota(jnp.int32, (n, n), 0) >> sh
    c = lax.broadcasted_iota(jnp.int32, (n, n), 1) >> sh
    pm = jnp.where(r == c, 1.0 / group, 0.0).astype(BF)
    xx = xf * xf
    hi = xx.astype(BF)
    lo = (xx - hi.astype(F32)).astype(BF)
    return _dot(hi, pm) + _dot(lo, pm)


def _sigmoid(x):
    return 1.0 / (1.0 + jnp.exp(-x))


def _silu(x):
    return x * _sigmoid(x)


def _gelu_tanh(x):
    return 0.5 * x * (1.0 + jnp.tanh(math.sqrt(2.0 / math.pi) * (x + 0.044715 * (x * x * x))))


def _rmsnorm_rows(x, g):
    ms = jnp.mean(x * x, axis=-1, keepdims=True)
    return x * lax.rsqrt(ms + EPS) * g


def _proj_kernel(x_ref, ng_ref, w_ref, gains_ref, o_ref):
    h = _rmsnorm_rows(x_ref[...], ng_ref[...]).astype(BF)

    def gnorm(r, group, row):
        return r * lax.rsqrt(_group_mean_sq(r, group) + EPS) * gains_ref[row:row + 1, :]

    for j in range(N_SLABS):
        r = _dot(h, w_ref[:, j * BRANCH_W:(j + 1) * BRANCH_W])
        if j == S_AQ:
            r = gnorm(r, NA_HD, G_AQ)
        elif j == S_AK:
            r = gnorm(r, NA_HD, G_AK)
        elif j == S_BQ:
            r = gnorm(r, DA_QK, G_BQ)
        elif j == S_BK:
            r = gnorm(r, DA_QK, G_BK)
        elif j == S_MQ:
            r = gnorm(r, MEM_HD, G_MQ)
        elif j in (S_AZ, S_BZ, S_CZ, S_DZ, S_MZ):
            r = _silu(r)
        elif j == S_DU:
            r = _gelu_tanh(r)
        elif j == S_DV:
            v = _gelu_tanh(r)
            mu = jnp.mean(v, axis=-1, keepdims=True)
            vc = v - mu
            var = jnp.mean(vc * vc, axis=-1, keepdims=True)
            r = vc * lax.rsqrt(var + EPS) * gains_ref[G_LNG:G_LNG + 1, :] + gains_ref[G_LNB:G_LNB + 1, :]
        o_ref[:, j * BRANCH_W:(j + 1) * BRANCH_W] = r.astype(BF)


def _proj(x2, norm_g, w_slab, gains, tm=512):
    n = x2.shape[0]
    return pl.pallas_call(
        _proj_kernel,
        grid=(n // tm,),
        in_specs=[
            pl.BlockSpec((tm, D_MODEL), lambda i: (i, 0)),
            _resident((1, D_MODEL), lambda i: (0, 0)),
            _resident((D_MODEL, SLAB_COLS), lambda i: (0, 0)),
            _resident((8, BRANCH_W), lambda i: (0, 0)),
        ],
        out_specs=pl.BlockSpec((tm, SLAB_COLS), lambda i: (i, 0)),
        out_shape=jax.ShapeDtypeStruct((n, SLAB_COLS), BF),
        compiler_params=_cparams("parallel"),
        name="proj",
    )(x2, norm_g, w_slab, gains)


def _memkv_kernel(mem_ref, g_ref, w_ref, kg_ref, k_ref, v_ref):
    h = _rmsnorm_rows(mem_ref[0], g_ref[...]).astype(BF)
    k = _dot(h, w_ref[:, :BRANCH_W])
    v = _dot(h, w_ref[:, BRANCH_W:])
    k = k * lax.rsqrt(_group_mean_sq(k, MEM_HD) + EPS) * kg_ref[...]
    k_ref[0] = k.astype(BF)
    v_ref[0] = v.astype(BF)


def _memkv(mem, m_norm_g, m_wkv, kg):
    b, m, _ = mem.shape
    return pl.pallas_call(
        _memkv_kernel,
        grid=(b,),
        in_specs=[
            pl.BlockSpec((1, m, D_MODEL), lambda i: (i, 0, 0)),
            _resident((1, D_MODEL), lambda i: (0, 0)),
            _resident((D_MODEL, 2 * BRANCH_W), lambda i: (0, 0)),
            _resident((1, BRANCH_W), lambda i: (0, 0)),
        ],
        out_specs=[pl.BlockSpec((1, m, BRANCH_W), lambda i: (i, 0, 0)),
                   pl.BlockSpec((1, m, BRANCH_W), lambda i: (i, 0, 0))],
        out_shape=[jax.ShapeDtypeStruct((b, m, BRANCH_W), BF)] * 2,
        compiler_params=_cparams("parallel"),
        name="memkv",
    )(mem, m_norm_g, m_wkv, kg)


NA_ROWS_PER_STEP = 8
NA_WIN = NA_KH * GRID_W
NA_LOOKAHEAD = 2


def _na_kernel(q_ref, k_ref, v_ref, z_ref, bias_ref, o_ref):
    n_rows = k_ref.shape[1] // GRID_W
    rb = pl.program_id(1)
    head_of_lane = _lane_group((GRID_W, BRANCH_W), 6)
    masks = [_lane_mask((GRID_W, BRANCH_W), h * NA_HD, (h + 1) * NA_HD, BF) for h in range(NA_HEADS)]

    def window(rr):
        r = rb * NA_ROWS_PER_STEP + rr
        rs = jnp.clip(r - NA_KH // 2, 0, n_rows - NA_KH)
        cls = jnp.where(r < NA_KH // 2, r, jnp.where(r > n_rows - NA_KH // 2, r - (n_rows - NA_KH), NA_KH // 2))
        return pl.multiple_of(rs * GRID_W, GRID_W), cls

    def scores(rr):
        w0, cls = window(rr)
        q = q_ref[0, rr * GRID_W:(rr + 1) * GRID_W, :]
        qs = jnp.concatenate([q * masks[h] for h in range(NA_HEADS)], axis=0)
        return _dot_nt(qs, k_ref[0, pl.ds(w0, NA_WIN), :]) + bias_ref[cls]

    pending = [scores(rr) for rr in range(NA_LOOKAHEAD)]
    for rr in range(NA_ROWS_PER_STEP):
        if rr + NA_LOOKAHEAD < NA_ROWS_PER_STEP:
            pending.append(scores(rr + NA_LOOKAHEAD))
        s = pending.pop(0)
        w0, _ = window(rr)
        m = jnp.max(s, axis=-1, keepdims=True)
        p = jnp.exp(s - m)
        l = jnp.sum(p, axis=-1, keepdims=True)
        pv = _dot(p.astype(BF), v_ref[0, pl.ds(w0, NA_WIN), :]) / l
        o = jnp.zeros((GRID_W, BRANCH_W), F32)
        for h in range(NA_HEADS):
            o = jnp.where(head_of_lane == h, pv[h * GRID_W:(h + 1) * GRID_W], o)
        rows = slice(rr * GRID_W, (rr + 1) * GRID_W)
        o_ref[0, rows, :] = (o * z_ref[0, rows, :].astype(F32)).astype(BF)


def _na(p3, bias):
    b, t, _ = p3.shape
    tq = NA_ROWS_PER_STEP * GRID_W
    return pl.pallas_call(
        _na_kernel,
        grid=(b, t // tq),
        in_specs=[
            pl.BlockSpec((1, tq, BRANCH_W), lambda i, j: (i, j, S_AQ)),
            pl.BlockSpec((1, t, BRANCH_W), lambda i, j: (i, 0, S_AK)),
            pl.BlockSpec((1, t, BRANCH_W), lambda i, j: (i, 0, S_AV)),
            pl.BlockSpec((1, tq, BRANCH_W), lambda i, j: (i, j, S_AZ)),
            _resident(bias.shape, lambda i, j: (0, 0, 0)),
        ],
        out_specs=pl.BlockSpec((1, tq, BRANCH_W), lambda i, j: (i, j, 0)),
        out_shape=jax.ShapeDtypeStruct((b, t, BRANCH_W), BF),
        compiler_params=_cparams("parallel", "arbitrary"),
        name="na",
    )(p3, p3, p3, p3, bias)


NA_REL_ROWS = 2 * NA_KH - 1
NA_REL_COLS = 2 * NA_KW - 1
NA_CLASSES = NA_KH


def _na_class_roff(n_rows):
    half = NA_KH // 2
    rows = np.array(list(range(half)) + [half] + [n_rows - half + 1 + i for i in range(half - 1)])
    rs = np.clip(rows - half, 0, n_rows - NA_KH)
    return [int(v) for v in rs - rows + NA_KH - 1]


def _nabias_kernel(roff0, rpb_ref, o_ref, f_scr):
    h = pl.program_id(0)
    shape = (GRID_W, 2 * GRID_W)
    q = lax.broadcasted_iota(jnp.int32, shape, 0)
    lane = lax.broadcasted_iota(jnp.int32, shape, 1)
    k = lane & (GRID_W - 1)
    coff = k - q + (NA_KW - 1)
    cs = jnp.clip(q - NA_KW // 2, 0, GRID_W - NA_KW)
    valid = (k >= cs) & (k < cs + NA_KW)
    first = lane < GRID_W

    def tile(r, carry):
        base = (h * NA_REL_ROWS + r) * NA_REL_COLS
        acc = jnp.zeros(shape, F32)
        for s in range(NA_REL_COLS):
            acc = jnp.where(coff == s, jnp.where(first, rpb_ref[base + s], rpb_ref[base + NA_REL_COLS + s]), acc)
        f_scr[r] = jnp.where(valid, acc, NEG_BIG)
        return carry

    lax.fori_loop(0, NA_REL_ROWS - 1, tile, 0)
    for c in range(NA_CLASSES):
        for p in range(NA_KH // 2):
            o_ref[c, 0, :, p * 2 * GRID_W:(p + 1) * 2 * GRID_W] = f_scr[roff0[c] + 2 * p]


def _na_bias_table(rpb, n_rows):
    return pl.pallas_call(
        functools.partial(_nabias_kernel, _na_class_roff(n_rows)),
        grid=(NA_HEADS,),
        in_specs=[pl.BlockSpec(memory_space=pltpu.SMEM)],
        out_specs=pl.BlockSpec((NA_CLASSES, 1, GRID_W, NA_WIN), lambda h: (0, h, 0, 0)),
        out_shape=jax.ShapeDtypeStruct((NA_CLASSES, NA_HEADS, GRID_W, NA_WIN), F32),
        scratch_shapes=[pltpu.VMEM((NA_REL_ROWS - 1, GRID_W, 2 * GRID_W), F32)],
        compiler_params=_cparams("arbitrary"),
        name="nabias",
    )(rpb.astype(F32).reshape(-1)).reshape(NA_CLASSES, NA_HEADS * GRID_W, NA_WIN)


DA_VT_ROWS = 80
X_SUM = DA_V


def _dprep_kernel(q_ref, k_ref, v_ref, qt_ref, k4_ref, vt_ref):
    tt = q_ref.shape[1]
    t0 = pl.program_id(1) * tt
    q = q_ref[0]
    k = k_ref[0]
    v = v_ref[0]
    src = lax.broadcasted_iota(jnp.int32, (BRANCH_W, DA_LANES), 0)
    dst = lax.broadcasted_iota(jnp.int32, (BRANCH_W, DA_LANES), 1)
    src_t = lax.broadcasted_iota(jnp.int32, (DA_LANES, BRANCH_W), 1)
    dst_t = lax.broadcasted_iota(jnp.int32, (DA_LANES, BRANCH_W), 0)
    klane = lax.broadcasted_iota(jnp.int32, (tt, DA_LANES), 1)
    kpos = t0 + lax.broadcasted_iota(jnp.int32, (tt, DA_LANES), 0)
    vrow = lax.broadcasted_iota(jnp.int32, (DA_VT_ROWS, tt), 0)
    for h in range(DA_HEADS):
        sel = jnp.where((src == h * DA_V + dst) & (dst < DA_V), 1.0, 0.0).astype(BF)
        sel_t = jnp.where((src_t == h * DA_V + dst_t) & (dst_t < DA_V), 1.0, 0.0).astype(BF)
        qt_ref[0, h] = _dot_nt(sel_t, q).astype(BF)
        vt = _dot_nt(sel_t[:DA_VT_ROWS], v)
        vt_ref[0, h] = jnp.where(vrow == X_SUM, 1.0, vt).astype(BF)
        slope = ALIBI_SLOPES[h]
        k4 = _dot(k, sel)
        k4 = jnp.where(klane == X_HI, (kpos >> 6).astype(F32) * (slope * 64.0), k4)
        k4 = jnp.where(klane == X_LO, (kpos & 63).astype(F32) * slope, k4)
        k4 = jnp.where((klane == X_ONE_A) | (klane == X_ONE_B), 1.0, k4)
        k4_ref[0, h] = k4.astype(BF)


def _dprep(p3, tt=512):
    b, t, _ = p3.shape
    return pl.pallas_call(
        _dprep_kernel,
        grid=(b, t // tt),
        in_specs=[
            pl.BlockSpec((1, tt, BRANCH_W), lambda i, j: (i, j, S_BQ)),
            pl.BlockSpec((1, tt, BRANCH_W), lambda i, j: (i, j, S_BK)),
            pl.BlockSpec((1, tt, BRANCH_W), lambda i, j: (i, j, S_BV)),
        ],
        out_specs=[
            pl.BlockSpec((1, DA_HEADS, DA_LANES, tt), lambda i, j: (i, 0, 0, j)),
            pl.BlockSpec((1, DA_HEADS, tt, DA_LANES), lambda i, j: (i, 0, j, 0)),
            pl.BlockSpec((1, DA_HEADS, DA_VT_ROWS, tt), lambda i, j: (i, 0, 0, j)),
        ],
        out_shape=[
            jax.ShapeDtypeStruct((b, DA_HEADS, DA_LANES, t), BF),
            jax.ShapeDtypeStruct((b, DA_HEADS, t, DA_LANES), BF),
            jax.ShapeDtypeStruct((b, DA_HEADS, DA_VT_ROWS, t), BF),
        ],
        compiler_params=_cparams("parallel", "parallel"),
        name="dprep",
    )(p3, p3, p3)


DA_QCOLS = 256
DA_KROWS = 128
DA_LOOKAHEAD = 8


def _da_kernel(lam_init, qt_ref, k_ref, vt_ref, z_ref, lamv_ref, subg_ref, o_ref, m_scr, acc_scr):
    bq = qt_ref.shape[3]
    bk = bq
    nk = k_ref.shape[2] // bk
    qi = pl.program_id(1)
    row = lax.broadcasted_iota(jnp.int32, (DA_LANES, bq), 0)
    qpos = qi * bq + lax.broadcasted_iota(jnp.int32, (DA_LANES, bq), 1)
    rel = lax.broadcasted_iota(jnp.int32, (bk, bq), 0) - lax.broadcasted_iota(jnp.int32, (bk, bq), 1)
    dist = jnp.abs(rel).astype(F32)

    lv = lamv_ref[...]
    lam = (jnp.exp(jnp.sum(lv[0:1] * lv[1:2], axis=-1, keepdims=True))
           - jnp.exp(jnp.sum(lv[2:3] * lv[3:4], axis=-1, keepdims=True)) + lam_init)

    def step(h, kb, qv, diag_bias):
        k0 = pl.multiple_of(kb * bk, bk)
        n_c = bq // DA_QCOLS
        chains = [(m, c) for m in range(2) for c in range(n_c)]
        cols = [slice(c * DA_QCOLS, (c + 1) * DA_QCOLS) for c in range(n_c)]
        m_in = [m_scr[m] for m in range(2)]
        acc_in = [acc_scr[m] for m in range(2)]
        m_run = {mc: m_in[mc[0]][:, cols[mc[1]]] for mc in chains}
        acc = {mc: acc_in[mc[0]][:, cols[mc[1]]] for mc in chains}
        tiles = [(kc, mc) for kc in range(bk // DA_KROWS) for mc in chains]

        def scores(tile):
            kc, (m, c) = tile
            r0 = kc * DA_KROWS
            kk = k_ref[0, h, pl.ds(k0 + r0, DA_KROWS), :]
            s = _dot(kk, qv[m][:, cols[c]])
            if diag_bias is not None:
                s = s - diag_bias[r0:r0 + DA_KROWS, cols[c]]
            return s

        pending = [scores(t) for t in tiles[:DA_LOOKAHEAD]]
        for i, (kc, mc) in enumerate(tiles):
            if i + DA_LOOKAHEAD < len(tiles):
                pending.append(scores(tiles[i + DA_LOOKAHEAD]))
            s = pending.pop(0)
            vt = vt_ref[0, h, :, pl.ds(k0 + kc * DA_KROWS, DA_KROWS)]
            m_new = jnp.maximum(m_run[mc], jnp.max(s, axis=0, keepdims=True))
            alpha = jnp.exp(m_run[mc] - m_new)
            p = jnp.exp(s - m_new).astype(BF)
            acc[mc] = alpha * acc[mc] + _dot(vt, p)
            m_run[mc] = m_new
        for m in range(2):
            acc_scr[m] = jnp.concatenate([acc[(m, c)] for c in range(n_c)], axis=-1)
            m_scr[m] = jnp.concatenate([m_run[(m, c)] for c in range(n_c)], axis=-1)

    heads = []
    for h in range(DA_HEADS):
        slope = ALIBI_SLOPES[h]
        qt = qt_ref[0, h]
        qm = [qt * jnp.where((row >= m * DA_QK) & (row < (m + 1) * DA_QK), 1.0, 0.0).astype(BF)
              for m in range(2)]
        ih = (qpos >> 6).astype(F32) * (slope * 64.0)
        il = (qpos & 63).astype(F32) * slope
        ex = jnp.where((row == X_HI) | (row == X_LO), 1.0,
                       jnp.where(row == X_ONE_A, -ih, jnp.where(row == X_ONE_B, -il, 0.0))).astype(BF)
        q_left = [qm[m] + ex for m in range(2)]
        q_right = [qm[m] - ex for m in range(2)]

        m_scr[...] = jnp.full(m_scr.shape, NEG_BIG, F32)
        acc_scr[...] = jnp.zeros(acc_scr.shape, F32)

        def left(kb, c, h=h, qv=q_left):
            step(h, kb, qv, None)
            return c

        def right(kb, c, h=h, qv=q_right):
            step(h, kb, qv, None)
            return c

        lax.fori_loop(0, qi, left, 0)
        step(h, qi, qm, dist * slope)
        lax.fori_loop(qi + 1, nk, right, 0)

        o = [acc_scr[m, :DA_V, :] / acc_scr[m, X_SUM:X_SUM + 1, :] for m in range(2)]
        heads.append(o[0] - lam * o[1])

    full = jnp.concatenate(heads, axis=0).T
    y = full * lax.rsqrt(_group_mean_sq(full, DA_V) + EPS) * subg_ref[...]
    o_ref[0] = (y * z_ref[0].astype(F32)).astype(BF)


def _da(qt4, k4, vt4, p3, lamv, subg, lam_init, bq=512):
    b, _, t, _ = k4.shape
    return pl.pallas_call(
        functools.partial(_da_kernel, lam_init),
        grid=(b, t // bq),
        in_specs=[
            pl.BlockSpec((1, DA_HEADS, DA_LANES, bq), lambda i, j: (i, 0, 0, j)),
            pl.BlockSpec((1, DA_HEADS, t, DA_LANES), lambda i, j: (i, 0, 0, 0)),
            pl.BlockSpec((1, DA_HEADS, DA_VT_ROWS, t), lambda i, j: (i, 0, 0, 0)),
            pl.BlockSpec((1, bq, BRANCH_W), lambda i, j: (i, j, S_BZ)),
            _resident((4, DA_QK), lambda i, j: (0, 0)),
            _resident((1, BRANCH_W), lambda i, j: (0, 0)),
        ],
        out_specs=pl.BlockSpec((1, bq, BRANCH_W), lambda i, j: (i, j, 0)),
        out_shape=jax.ShapeDtypeStruct((b, t, BRANCH_W), BF),
        scratch_shapes=[
            pltpu.VMEM((2, 1, bq), F32),
            pltpu.VMEM((2, DA_VT_ROWS, bq), F32),
        ],
        compiler_params=_cparams("parallel", "arbitrary"),
        name="da",
    )(qt4, k4, vt4, p3, lamv, subg)


POOL_HALO = 16


def _pool_kernel(x_ref, z_ref, band_ref, cw_ref, cs_ref, o_ref):
    tt = z_ref.shape[1]
    t_total = x_ref.shape[1]
    nt = t_total // tt
    ti = pl.program_id(1)
    t0 = pl.multiple_of(ti * tt, tt)
    xm = x_ref[0, pl.ds(t0, tt), :]
    p0 = pl.multiple_of(jnp.maximum(t0 - POOL_HALO, 0), POOL_HALO)
    n0 = pl.multiple_of(jnp.minimum(t0 + tt, t_total - POOL_HALO), POOL_HALO)
    has_prev = (ti > 0).astype(F32)
    has_next = (ti < nt - 1).astype(F32)
    prev = (x_ref[0, pl.ds(p0, POOL_HALO), :].astype(F32) * has_prev).astype(BF)
    nxt = (x_ref[0, pl.ds(n0, POOL_HALO), :].astype(F32) * has_next).astype(BF)
    xcat = jnp.concatenate([prev, xm, nxt], axis=0)

    group = _lane_group((tt, BRANCH_W), 6)
    wsum = jnp.zeros((tt, BRANCH_W), F32)
    for g in range(len(POOL_WINDOWS)):
        wsum = jnp.where(group == g, _dot(band_ref[g], xcat), wsum)

    tpos = t0 + lax.broadcasted_iota(jnp.int32, (tt, BRANCH_W), 0)
    half = jnp.left_shift(1, group)
    lo = jnp.maximum(tpos - half, 0)
    hi = jnp.minimum(tpos + half - 1, t_total - 1)
    cnt = (hi - lo + 1).astype(F32)
    dlt = wsum / cnt - xm.astype(F32)
    y = _dot(dlt.astype(BF), cw_ref[...]) * cs_ref[...]
    o_ref[0] = (y * z_ref[0].astype(F32)).astype(BF)


def _pool_band(tt):
    band = np.zeros((len(POOL_WINDOWS), tt, tt + 2 * POOL_HALO), np.float32)
    t = np.arange(tt)[:, None]
    u = np.arange(tt + 2 * POOL_HALO)[None, :] - POOL_HALO
    for g, w in enumerate(POOL_WINDOWS):
        band[g] = ((u >= t - w // 2) & (u <= t - w // 2 + w - 1)).astype(np.float32)
    return jnp.asarray(band, BF)


def _pool(p3, cw_bd, c_scale, tt=256):
    b, t, _ = p3.shape
    band = _pool_band(tt)
    return pl.pallas_call(
        _pool_kernel,
        grid=(b, t // tt),
        in_specs=[
            pl.BlockSpec((1, t, BRANCH_W), lambda i, j: (i, 0, S_CX)),
            pl.BlockSpec((1, tt, BRANCH_W), lambda i, j: (i, j, S_CZ)),
            _resident(band.shape, lambda i, j: (0, 0, 0)),
            _resident((BRANCH_W, BRANCH_W), lambda i, j: (0, 0)),
            _resident((1, BRANCH_W), lambda i, j: (0, 0)),
        ],
        out_specs=pl.BlockSpec((1, tt, BRANCH_W), lambda i, j: (i, j, 0)),
        out_shape=jax.ShapeDtypeStruct((b, t, BRANCH_W), BF),
        compiler_params=_cparams("parallel", "arbitrary"),
        name="pool",
    )(p3, p3, band, cw_bd, c_scale)


def _sg_kernel(u_ref, v_ref, z_ref, ws_ref, bias_ref, o_ref):
    tt = u_ref.shape[0]
    group = _lane_group((SG_CHUNK, BRANCH_W), 6)
    for c in range(tt // SG_CHUNK):
        rows = slice(c * SG_CHUNK, (c + 1) * SG_CHUNK)
        vc = v_ref[rows, :]
        mix = jnp.zeros((SG_CHUNK, BRANCH_W), F32)
        for g in range(SG_GROUPS):
            mix = jnp.where(group == g, _dot(ws_ref[g], vc), mix)
        y = u_ref[rows, :].astype(F32) * (mix + bias_ref[...]) * z_ref[rows, :].astype(F32)
        o_ref[rows, :] = y.astype(BF)


def _sg(p2, ws, sg_bias, tt=512):
    n = p2.shape[0]
    return pl.pallas_call(
        _sg_kernel,
        grid=(n // tt,),
        in_specs=[
            pl.BlockSpec((tt, BRANCH_W), lambda i: (i, S_DU)),
            pl.BlockSpec((tt, BRANCH_W), lambda i: (i, S_DV)),
            pl.BlockSpec((tt, BRANCH_W), lambda i: (i, S_DZ)),
            _resident((SG_GROUPS, SG_CHUNK, SG_CHUNK), lambda i: (0, 0, 0)),
            _resident((SG_CHUNK, BRANCH_W), lambda i: (0, 0)),
        ],
        out_specs=pl.BlockSpec((tt, BRANCH_W), lambda i: (i, 0)),
        out_shape=jax.ShapeDtypeStruct((n, BRANCH_W), BF),
        compiler_params=_cparams("parallel"),
        name="sg",
    )(p2, p2, p2, ws, sg_bias)


def _mem_kernel(q_ref, z_ref, k_ref, v_ref, o_ref):
    tt = q_ref.shape[1]
    q = q_ref[0]
    k = k_ref[0]
    v = v_ref[0]
    head_of_lane = _lane_group((tt, BRANCH_W), 6)
    o = jnp.zeros((tt, BRANCH_W), F32)
    for h in range(MEM_HEADS):
        qm = q * _lane_mask(q.shape, h * MEM_HD, (h + 1) * MEM_HD, BF)
        s = _dot_nt(qm, k)
        m = jnp.max(s, axis=-1, keepdims=True)
        p = jnp.exp(s - m)
        l = jnp.sum(p, axis=-1, keepdims=True)
        oh = _dot(p.astype(BF), v) / l
        o = jnp.where(head_of_lane == h, oh, o)
    o_ref[0] = (o * z_ref[0].astype(F32)).astype(BF)


def _mem_attn(p3, mk, mv, tt=512):
    b, t, _ = p3.shape
    m = mk.shape[1]
    return pl.pallas_call(
        _mem_kernel,
        grid=(b, t // tt),
        in_specs=[
            pl.BlockSpec((1, tt, BRANCH_W), lambda i, j: (i, j, S_MQ)),
            pl.BlockSpec((1, tt, BRANCH_W), lambda i, j: (i, j, S_MZ)),
            pl.BlockSpec((1, m, BRANCH_W), lambda i, j: (i, 0, 0)),
            pl.BlockSpec((1, m, BRANCH_W), lambda i, j: (i, 0, 0)),
        ],
        out_specs=pl.BlockSpec((1, tt, BRANCH_W), lambda i, j: (i, j, 0)),
        out_shape=jax.ShapeDtypeStruct((b, t, BRANCH_W), BF),
        compiler_params=_cparams("parallel", "arbitrary"),
        name="mem",
    )(p3, p3, mk, mv)


def _final_kernel(x_ref, ng_ref, ya_ref, yb_ref, yc_ref, yd_ref, ym_ref,
                  wg_ref, bg_ref, wb_ref, wo_ref, o_ref):
    x = x_ref[...]
    h = _rmsnorm_rows(x, ng_ref[...]).astype(BF)
    merged = jnp.zeros(x.shape, F32)
    for i, y_ref in enumerate((ya_ref, yb_ref, yc_ref, yd_ref, ym_ref)):
        logit = _dot(h, wg_ref[:, i * D_MODEL:(i + 1) * D_MODEL]) + bg_ref[i:i + 1, :]
        merged = merged + _sigmoid(logit) * _dot(y_ref[...], wb_ref[i])
    o_ref[...] = x + _dot(merged.astype(BF), wo_ref[...])


def _final(x2, norm_g, ys, w_gate, b_gate, w_branch, w_out, tm=512):
    n = x2.shape[0]
    yspec = pl.BlockSpec((tm, BRANCH_W), lambda i: (i, 0))
    return pl.pallas_call(
        _final_kernel,
        grid=(n // tm,),
        in_specs=[
            pl.BlockSpec((tm, D_MODEL), lambda i: (i, 0)),
            _resident((1, D_MODEL), lambda i: (0, 0)),
            yspec, yspec, yspec, yspec, yspec,
            _resident((D_MODEL, N_BRANCH * D_MODEL), lambda i: (0, 0)),
            _resident((8, D_MODEL), lambda i: (0, 0)),
            _resident((N_BRANCH, BRANCH_W, D_MODEL), lambda i: (0, 0, 0)),
            _resident((D_MODEL, D_MODEL), lambda i: (0, 0)),
        ],
        out_specs=pl.BlockSpec((tm, D_MODEL), lambda i: (i, 0)),
        out_shape=jax.ShapeDtypeStruct((n, D_MODEL), F32),
        compiler_params=_cparams("parallel"),
        name="final",
    )(x2, norm_g, *ys, w_gate, b_gate, w_branch, w_out)


def _tile4(g):
    return jnp.tile(g.astype(F32), BRANCH_W // g.shape[0])


def _layer(x, mem, layer_idx, norm_g, w_in, b_gate, a_qn_g, a_kn_g, a_rpb,
           b_qn_g, b_kn_g, b_lam_q1, b_lam_k1, b_lam_q2, b_lam_k2, b_sub_g,
           c_w, c_scale, d_ln_g, d_ln_b, d_ws, d_bs,
           m_norm_g, m_wkv, m_qn_g, m_kn_g, w_branch, w_out):
    b, t, d = x.shape
    n = b * t
    x2 = x.reshape(n, d)
    ng = norm_g.reshape(1, d).astype(F32)

    gains = jnp.zeros((8, BRANCH_W), F32)
    gains = gains.at[G_AQ].set(_tile4(a_qn_g) * NA_HD ** -0.5)
    gains = gains.at[G_AK].set(_tile4(a_kn_g))
    gains = gains.at[G_BQ].set(_tile4(b_qn_g) * DA_QK ** -0.5)
    gains = gains.at[G_BK].set(_tile4(b_kn_g))
    gains = gains.at[G_MQ].set(_tile4(m_qn_g) * MEM_HD ** -0.5)
    gains = gains.at[G_LNG].set(d_ln_g.astype(F32))
    gains = gains.at[G_LNB].set(d_ln_b.astype(F32))

    p2 = _proj(x2, ng, w_in[:, :SLAB_COLS].astype(BF), gains)
    p3 = p2.reshape(b, t, SLAB_COLS)

    y_a = _na(p3, _na_bias_table(a_rpb, t // GRID_W))

    lam_init = 0.8 - 0.6 * math.exp(-0.3 * layer_idx)
    qt4, k4, vt4 = _dprep(p3)
    lamv = jnp.stack([b_lam_q1, b_lam_k1, b_lam_q2, b_lam_k2]).astype(F32)
    subg = (_tile4(b_sub_g) * (1.0 - lam_init)).reshape(1, BRANCH_W)
    y_b = _da(qt4, k4, vt4, p3, lamv, subg, lam_init)

    cw_bd = jnp.zeros((BRANCH_W, BRANCH_W), F32)
    for g in range(len(POOL_WINDOWS)):
        cw_bd = cw_bd.at[g * POOL_GC:(g + 1) * POOL_GC, g * POOL_GC:(g + 1) * POOL_GC].set(c_w[g].astype(F32))
    y_c = _pool(p3, cw_bd.astype(BF), c_scale.reshape(1, BRANCH_W).astype(F32))

    sg_bias = jnp.repeat(d_bs.astype(F32).T, SG_GC, axis=1)
    y_d = _sg(p2, d_ws.astype(BF), sg_bias)

    mk, mv = _memkv(mem, m_norm_g.reshape(1, d).astype(F32), m_wkv.astype(BF),
                    _tile4(m_kn_g).reshape(1, BRANCH_W))
    y_m = _mem_attn(p3, mk, mv)

    bg = jnp.zeros((8, d), F32).at[:N_BRANCH].set(b_gate.astype(F32))
    ys = [y.reshape(n, BRANCH_W) for y in (y_a, y_b, y_c, y_d, y_m)]
    out = _final(x2, ng, ys, w_in[:, SLAB_COLS:].astype(BF), bg,
                 w_branch.astype(BF), w_out.astype(BF))
    return out.reshape(b, t, d)


def kernel(x, mem, norm_g, w_in, b_gate, a_qn_g, a_kn_g, a_rpb, b_qn_g, b_kn_g, b_lam_q1, b_lam_k1, b_lam_q2, b_lam_k2, b_sub_g, c_w, c_scale, d_ln_g, d_ln_b, d_ws, d_bs, m_norm_g, m_wkv, m_qn_g, m_kn_g, w_branch, w_out):
    for l in range(DEPTH):
        x = _layer(x, mem, l, norm_g[l], w_in[l], b_gate[l], a_qn_g[l], a_kn_g[l], a_rpb[l],
                   b_qn_g[l], b_kn_g[l], b_lam_q1[l], b_lam_k1[l], b_lam_q2[l], b_lam_k2[l], b_sub_g[l],
                   c_w[l], c_scale[l], d_ln_g[l], d_ln_b[l], d_ws[l], d_bs[l],
                   m_norm_g[l], m_wkv[l], m_qn_g[l], m_kn_g[l], w_branch[l], w_out[l])
    return x
```

```python
import functools
import math

import numpy as np
import jax
import jax.numpy as jnp
from jax import lax
from jax.experimental import pallas as pl
from jax.experimental.pallas import tpu as pltpu

F32 = jnp.float32
BF = jnp.bfloat16

D_MODEL = 1024
DEPTH = 2
GRID_W = 64
BRANCH_W = 256
N_BRANCH = 5
NA_HEADS = 4
NA_HD = 64
NA_KH = 8
NA_KW = 16
DA_HEADS = 4
DA_QK = 32
DA_V = 64
ALIBI_BASE = 8.0
POOL_WINDOWS = (2, 4, 8, 16)
POOL_GC = 64
SG_CHUNK = 128
SG_GROUPS = 4
SG_GC = 64
MEM_HEADS = 4
MEM_HD = 64
N_SLABS = 15
SLAB_COLS = N_SLABS * BRANCH_W
EPS = 1e-6

(S_AQ, S_AK, S_AV, S_AZ, S_BQ, S_BK, S_BV, S_BZ, S_CX, S_CZ,
 S_DU, S_DV, S_DZ, S_MQ, S_MZ) = range(N_SLABS)

(G_AQ, G_AK, G_BQ, G_BK, G_MQ, G_LNG, G_LNB) = range(7)

VMEM_LIMIT_BYTES = 56 * 1024 * 1024
NEG_BIG = -1e30

ALIBI_SLOPES = tuple(2.0 ** (-ALIBI_BASE * (h + 1) / DA_HEADS) for h in range(DA_HEADS))
DA_LANES = 128


def _bf16_pieces(x, n):
    out = []
    for _ in range(n):
        piece = float(np.asarray(x, np.float32).astype(jnp.bfloat16).astype(np.float32))
        out.append(piece)
        x -= piece
    return tuple(out)


LOG2E = math.log2(math.e)
LOG2E_PIECES = _bf16_pieces(LOG2E, 3)
N_PIECES = len(LOG2E_PIECES)
X_KHI, X_KLO, X_QHI, X_QLO = (DA_V + g * N_PIECES for g in range(4))
X_END = DA_V + 4 * N_PIECES


def _piece_of(idx):
    out = jnp.zeros(idx.shape, F32)
    for g in range(4):
        for p, piece in enumerate(LOG2E_PIECES):
            out = jnp.where(idx == DA_V + g * N_PIECES + p, piece, out)
    return out


def _cparams(*sem):
    return pltpu.CompilerParams(dimension_semantics=sem, vmem_limit_bytes=VMEM_LIMIT_BYTES)


def _resident(shape, index_map):
    return pl.BlockSpec(shape, index_map, pipeline_mode=pl.Buffered(1))


def _dot(a, b):
    return jnp.dot(a, b, preferred_element_type=F32)


def _dot_nt(a, b):
    return lax.dot_general(a, b, (((1,), (1,)), ((), ())), preferred_element_type=F32)


def _lane_group(shape, log2_group):
    return lax.broadcasted_iota(jnp.int32, shape, len(shape) - 1) >> log2_group


def _lane_mask(shape, lo, hi, dtype):
    lane = lax.broadcasted_iota(jnp.int32, shape, len(shape) - 1)
    return jnp.where((lane >= lo) & (lane < hi), 1.0, 0.0).astype(dtype)


def _group_mean_sq(xf, group):
    n = xf.shape[-1]
    sh = int(math.log2(group))
    r = lax.broadcasted_iota(jnp.int32, (n, n), 0) >> sh
    c = lax.broadcasted_iota(jnp.int32, (n, n), 1) >> sh
    pm = jnp.where(r == c, 1.0 / group, 0.0).astype(BF)
    xx = xf * xf
    hi = xx.astype(BF)
    lo = (xx - hi.astype(F32)).astype(BF)
    return _dot(hi, pm) + _dot(lo, pm)


def _sigmoid(x):
    return 1.0 / (1.0 + jnp.exp(-x))


def _silu(x):
    return x * _sigmoid(x)


def _gelu_tanh(x):
    return 0.5 * x * (1.0 + jnp.tanh(math.sqrt(2.0 / math.pi) * (x + 0.044715 * (x * x * x))))


def _rmsnorm_rows(x, g):
    ms = jnp.mean(x * x, axis=-1, keepdims=True)
    return x * lax.rsqrt(ms + EPS) * g


def _proj_kernel(x_ref, ng_ref, w_ref, gains_ref, o_ref):
    h = _rmsnorm_rows(x_ref[...], ng_ref[...]).astype(BF)

    def gnorm(r, group, row):
        return r * lax.rsqrt(_group_mean_sq(r, group) + EPS) * gains_ref[row:row + 1, :]

    for j in range(N_SLABS):
        r = _dot(h, w_ref[:, j * BRANCH_W:(j + 1) * BRANCH_W])
        if j == S_AQ:
            r = gnorm(r, NA_HD, G_AQ)
        elif j == S_AK:
            r = gnorm(r, NA_HD, G_AK)
        elif j == S_BQ:
            r = gnorm(r, DA_QK, G_BQ)
        elif j == S_BK:
            r = gnorm(r, DA_QK, G_BK)
        elif j == S_MQ:
            r = gnorm(r, MEM_HD, G_MQ)
        elif j in (S_AZ, S_BZ, S_CZ, S_DZ, S_MZ):
            r = _silu(r)
        elif j == S_DU:
            r = _gelu_tanh(r)
        elif j == S_DV:
            v = _gelu_tanh(r)
            mu = jnp.mean(v, axis=-1, keepdims=True)
            vc = v - mu
            var = jnp.mean(vc * vc, axis=-1, keepdims=True)
            r = vc * lax.rsqrt(var + EPS) * gains_ref[G_LNG:G_LNG + 1, :] + gains_ref[G_LNB:G_LNB + 1, :]
        o_ref[:, j * BRANCH_W:(j + 1) * BRANCH_W] = r.astype(BF)


def _proj(x2, norm_g, w_slab, gains, tm=512):
    n = x2.shape[0]
    return pl.pallas_call(
        _proj_kernel,
        grid=(n // tm,),
        in_specs=[
            pl.BlockSpec((tm, D_MODEL), lambda i: (i, 0)),
            _resident((1, D_MODEL), lambda i: (0, 0)),
            _resident((D_MODEL, SLAB_COLS), lambda i: (0, 0)),
            _resident((8, BRANCH_W), lambda i: (0, 0)),
        ],
        out_specs=pl.BlockSpec((tm, SLAB_COLS), lambda i: (i, 0)),
        out_shape=jax.ShapeDtypeStruct((n, SLAB_COLS), BF),
        compiler_params=_cparams("parallel"),
        name="proj",
    )(x2, norm_g, w_slab, gains)


def _memkv_kernel(mem_ref, g_ref, w_ref, kg_ref, k_ref, v_ref):
    h = _rmsnorm_rows(mem_ref[0], g_ref[...]).astype(BF)
    k = _dot(h, w_ref[:, :BRANCH_W])
    v = _dot(h, w_ref[:, BRANCH_W:])
    k = k * lax.rsqrt(_group_mean_sq(k, MEM_HD) + EPS) * kg_ref[...]
    k_ref[0] = k.astype(BF)
    v_ref[0] = v.astype(BF)


def _memkv(mem, m_norm_g, m_wkv, kg):
    b, m, _ = mem.shape
    return pl.pallas_call(
        _memkv_kernel,
        grid=(b,),
        in_specs=[
            pl.BlockSpec((1, m, D_MODEL), lambda i: (i, 0, 0)),
            _resident((1, D_MODEL), lambda i: (0, 0)),
            _resident((D_MODEL, 2 * BRANCH_W), lambda i: (0, 0)),
            _resident((1, BRANCH_W), lambda i: (0, 0)),
        ],
        out_specs=[pl.BlockSpec((1, m, BRANCH_W), lambda i: (i, 0, 0)),
                   pl.BlockSpec((1, m, BRANCH_W), lambda i: (i, 0, 0))],
        out_shape=[jax.ShapeDtypeStruct((b, m, BRANCH_W), BF)] * 2,
        compiler_params=_cparams("parallel"),
        name="memkv",
    )(mem, m_norm_g, m_wkv, kg)


NA_ROWS_PER_STEP = 8
NA_WIN = NA_KH * GRID_W
NA_LOOKAHEAD = 2


def _na_kernel(q_ref, k_ref, v_ref, z_ref, bias_ref, o_ref):
    n_rows = k_ref.shape[1] // GRID_W
    rb = pl.program_id(1)
    head_of_lane = _lane_group((GRID_W, BRANCH_W), 6)
    masks = [_lane_mask((GRID_W, BRANCH_W), h * NA_HD, (h + 1) * NA_HD, BF) for h in range(NA_HEADS)]

    def window(rr):
        r = rb * NA_ROWS_PER_STEP + rr
        rs = jnp.clip(r - NA_KH // 2, 0, n_rows - NA_KH)
        cls = jnp.where(r < NA_KH // 2, r, jnp.where(r > n_rows - NA_KH // 2, r - (n_rows - NA_KH), NA_KH // 2))
        return pl.multiple_of(rs * GRID_W, GRID_W), cls

    def scores(rr):
        w0, cls = window(rr)
        q = q_ref[0, rr * GRID_W:(rr + 1) * GRID_W, :]
        qs = jnp.concatenate([q * masks[h] for h in range(NA_HEADS)], axis=0)
        return _dot_nt(qs, k_ref[0, pl.ds(w0, NA_WIN), :]) + bias_ref[cls]

    pending = [scores(rr) for rr in range(NA_LOOKAHEAD)]
    for rr in range(NA_ROWS_PER_STEP):
        if rr + NA_LOOKAHEAD < NA_ROWS_PER_STEP:
            pending.append(scores(rr + NA_LOOKAHEAD))
        s = pending.pop(0)
        w0, _ = window(rr)
        m = jnp.max(s, axis=-1, keepdims=True)
        p = jnp.exp(s - m)
        l = jnp.sum(p, axis=-1, keepdims=True)
        pv = _dot(p.astype(BF), v_ref[0, pl.ds(w0, NA_WIN), :]) / l
        o = jnp.zeros((GRID_W, BRANCH_W), F32)
        for h in range(NA_HEADS):
            o = jnp.where(head_of_lane == h, pv[h * GRID_W:(h + 1) * GRID_W], o)
        rows = slice(rr * GRID_W, (rr + 1) * GRID_W)
        o_ref[0, rows, :] = (o * z_ref[0, rows, :].astype(F32)).astype(BF)


def _na(p3, bias):
    b, t, _ = p3.shape
    tq = NA_ROWS_PER_STEP * GRID_W
    return pl.pallas_call(
        _na_kernel,
        grid=(b, t // tq),
        in_specs=[
            pl.BlockSpec((1, tq, BRANCH_W), lambda i, j: (i, j, S_AQ)),
            pl.BlockSpec((1, t, BRANCH_W), lambda i, j: (i, 0, S_AK)),
            pl.BlockSpec((1, t, BRANCH_W), lambda i, j: (i, 0, S_AV)),
            pl.BlockSpec((1, tq, BRANCH_W), lambda i, j: (i, j, S_AZ)),
            _resident(bias.shape, lambda i, j: (0, 0, 0)),
        ],
        out_specs=pl.BlockSpec((1, tq, BRANCH_W), lambda i, j: (i, j, 0)),
        out_shape=jax.ShapeDtypeStruct((b, t, BRANCH_W), BF),
        compiler_params=_cparams("parallel", "arbitrary"),
        name="na",
    )(p3, p3, p3, p3, bias)


NA_REL_ROWS = 2 * NA_KH - 1
NA_REL_COLS = 2 * NA_KW - 1
NA_CLASSES = NA_KH


def _na_class_roff(n_rows):
    half = NA_KH // 2
    rows = np.array(list(range(half)) + [half] + [n_rows - half + 1 + i for i in range(half - 1)])
    rs = np.clip(rows - half, 0, n_rows - NA_KH)
    return [int(v) for v in rs - rows + NA_KH - 1]


def _nabias_kernel(roff0, rpb_ref, o_ref, f_scr):
    h = pl.program_id(0)
    shape = (GRID_W, 2 * GRID_W)
    q = lax.broadcasted_iota(jnp.int32, shape, 0)
    lane = lax.broadcasted_iota(jnp.int32, shape, 1)
    k = lane & (GRID_W - 1)
    coff = k - q + (NA_KW - 1)
    cs = jnp.clip(q - NA_KW // 2, 0, GRID_W - NA_KW)
    valid = (k >= cs) & (k < cs + NA_KW)
    first = lane < GRID_W

    def tile(r, carry):
        base = (h * NA_REL_ROWS + r) * NA_REL_COLS
        acc = jnp.zeros(shape, F32)
        for s in range(NA_REL_COLS):
            acc = jnp.where(coff == s, jnp.where(first, rpb_ref[base + s], rpb_ref[base + NA_REL_COLS + s]), acc)
        f_scr[r] = jnp.where(valid, acc, NEG_BIG)
        return carry

    lax.fori_loop(0, NA_REL_ROWS - 1, tile, 0)
    for c in range(NA_CLASSES):
        for p in range(NA_KH // 2):
            o_ref[c, 0, :, p * 2 * GRID_W:(p + 1) * 2 * GRID_W] = f_scr[roff0[c] + 2 * p]


def _na_bias_table(rpb, n_rows):
    return pl.pallas_call(
        functools.partial(_nabias_kernel, _na_class_roff(n_rows)),
        grid=(NA_HEADS,),
        in_specs=[pl.BlockSpec(memory_space=pltpu.SMEM)],
        out_specs=pl.BlockSpec((NA_CLASSES, 1, GRID_W, NA_WIN), lambda h: (0, h, 0, 0)),
        out_shape=jax.ShapeDtypeStruct((NA_CLASSES, NA_HEADS, GRID_W, NA_WIN), F32),
        scratch_shapes=[pltpu.VMEM((NA_REL_ROWS - 1, GRID_W, 2 * GRID_W), F32)],
        compiler_params=_cparams("arbitrary"),
        name="nabias",
    )(rpb.astype(F32).reshape(-1)).reshape(NA_CLASSES, NA_HEADS * GRID_W, NA_WIN)


DA_VT_ROWS = 80
X_SUM = DA_V


def _dprep_kernel(q_ref, k_ref, v_ref, qt_ref, k4_ref, vt_ref):
    tt = q_ref.shape[1]
    t0 = pl.program_id(1) * tt
    q = q_ref[0]
    k = k_ref[0]
    v = v_ref[0]
    src = lax.broadcasted_iota(jnp.int32, (BRANCH_W, DA_LANES), 0)
    dst = lax.broadcasted_iota(jnp.int32, (BRANCH_W, DA_LANES), 1)
    src_t = lax.broadcasted_iota(jnp.int32, (DA_LANES, BRANCH_W), 1)
    dst_t = lax.broadcasted_iota(jnp.int32, (DA_LANES, BRANCH_W), 0)
    klane = lax.broadcasted_iota(jnp.int32, (tt, DA_LANES), 1)
    kpos = t0 + lax.broadcasted_iota(jnp.int32, (tt, DA_LANES), 0)
    vrow = lax.broadcasted_iota(jnp.int32, (DA_VT_ROWS, tt), 0)
    for h in range(DA_HEADS):
        sel = jnp.where((src == h * DA_V + dst) & (dst < DA_V), 1.0, 0.0).astype(BF)
        sel_t = jnp.where((src_t == h * DA_V + dst_t) & (dst_t < DA_V), 1.0, 0.0).astype(BF)
        qt_ref[0, h] = _dot_nt(sel_t, q).astype(BF)
        vt = _dot_nt(sel_t[:DA_VT_ROWS], v)
        vt_ref[0, h] = jnp.where(vrow == X_SUM, 1.0, vt).astype(BF)
        slope = ALIBI_SLOPES[h]
        k4 = _dot(k, sel)
        k4 = jnp.where((klane >= X_KHI) & (klane < X_KLO), (kpos >> 6).astype(F32) * (slope * 64.0), k4)
        k4 = jnp.where((klane >= X_KLO) & (klane < X_QHI), (kpos & 63).astype(F32) * slope, k4)
        k4 = jnp.where((klane >= X_QHI) & (klane < X_END), _piece_of(klane), k4)
        k4_ref[0, h] = k4.astype(BF)


def _dprep(p3, tt=512):
    b, t, _ = p3.shape
    return pl.pallas_call(
        _dprep_kernel,
        grid=(b, t // tt),
        in_specs=[
            pl.BlockSpec((1, tt, BRANCH_W), lambda i, j: (i, j, S_BQ)),
            pl.BlockSpec((1, tt, BRANCH_W), lambda i, j: (i, j, S_BK)),
            pl.BlockSpec((1, tt, BRANCH_W), lambda i, j: (i, j, S_BV)),
        ],
        out_specs=[
            pl.BlockSpec((1, DA_HEADS, DA_LANES, tt), lambda i, j: (i, 0, 0, j)),
            pl.BlockSpec((1, DA_HEADS, tt, DA_LANES), lambda i, j: (i, 0, j, 0)),
            pl.BlockSpec((1, DA_HEADS, DA_VT_ROWS, tt), lambda i, j: (i, 0, 0, j)),
        ],
        out_shape=[
            jax.ShapeDtypeStruct((b, DA_HEADS, DA_LANES, t), BF),
            jax.ShapeDtypeStruct((b, DA_HEADS, t, DA_LANES), BF),
            jax.ShapeDtypeStruct((b, DA_HEADS, DA_VT_ROWS, t), BF),
        ],
        compiler_params=_cparams("parallel", "parallel"),
        name="dprep",
    )(p3, p3, p3)


DA_QCOLS = 256
DA_KROWS = 256
DA_LOOKAHEAD = 6
Q_LEFT, Q_DIAG, Q_RIGHT = range(3)


def _da_kernel(lam_init, qt_ref, k_ref, vt_ref, z_ref, lamv_ref, subg_ref, o_ref,
               qv_scr, dist_scr, m_scr, acc_scr):
    bq = qt_ref.shape[3]
    bk = bq
    nk = k_ref.shape[2] // bk
    qi = pl.program_id(1)
    row = lax.broadcasted_iota(jnp.int32, (DA_LANES, bq), 0)
    qpos = qi * bq + lax.broadcasted_iota(jnp.int32, (DA_LANES, bq), 1)

    lv = lamv_ref[...]
    lam = (jnp.exp(jnp.sum(lv[0:1] * lv[1:2], axis=-1, keepdims=True))
           - jnp.exp(jnp.sum(lv[2:3] * lv[3:4], axis=-1, keepdims=True)) + lam_init)

    cols = [slice(c * DA_QCOLS, (c + 1) * DA_QCOLS) for c in range(bq // DA_QCOLS)]
    chains = [(h, m, c) for h in range(DA_HEADS) for m in range(2) for c in range(len(cols))]

    def step(kb, side):
        k0 = pl.multiple_of(kb * bk, bk)
        tiles = [(kc * DA_KROWS, ch) for kc in range(bk // DA_KROWS) for ch in chains]

        def scores(tile):
            r0, (h, m, c) = tile
            s = _dot(k_ref[0, h, pl.ds(k0 + r0, DA_KROWS), :], qv_scr[h, side, m, :, cols[c]])
            if side == Q_DIAG:
                s = s - dist_scr[r0:r0 + DA_KROWS, cols[c]] * (ALIBI_SLOPES[h] * LOG2E)
            return s

        pending = [scores(t) for t in tiles[:DA_LOOKAHEAD]]
        for i, (r0, (h, m, c)) in enumerate(tiles):
            if i + DA_LOOKAHEAD < len(tiles):
                pending.append(scores(tiles[i + DA_LOOKAHEAD]))
            s = pending.pop(0)
            m_old = m_scr[h, m, :, cols[c]]
            m_new = jnp.maximum(m_old, jnp.max(s, axis=0, keepdims=True))
            alpha = jnp.exp2(m_old - m_new)
            p = jnp.exp2(s - m_new).astype(BF)
            pv = _dot(vt_ref[0, h, :, pl.ds(k0 + r0, DA_KROWS)], p)
            acc_scr[h, m, :, cols[c]] = alpha * acc_scr[h, m, :, cols[c]] + pv
            m_scr[h, m, :, cols[c]] = m_new

    rel = lax.broadcasted_iota(jnp.int32, (bk, bq), 0) - lax.broadcasted_iota(jnp.int32, (bk, bq), 1)
    dist_scr[...] = jnp.abs(rel).astype(F32)
    for h in range(DA_HEADS):
        slope = ALIBI_SLOPES[h]
        qt = qt_ref[0, h]
        ih = (qpos >> 6).astype(F32) * (slope * 64.0)
        il = (qpos & 63).astype(F32) * slope
        ex = jnp.where((row >= X_KHI) & (row < X_QHI), _piece_of(row),
                       jnp.where((row >= X_QHI) & (row < X_QLO), -ih,
                                 jnp.where((row >= X_QLO) & (row < X_END), -il, 0.0))).astype(BF)
        for m in range(2):
            qm = qt * jnp.where((row >= m * DA_QK) & (row < (m + 1) * DA_QK), 1.0, 0.0).astype(BF)
            qv_scr[h, Q_LEFT, m] = qm + ex
            qv_scr[h, Q_DIAG, m] = qm
            qv_scr[h, Q_RIGHT, m] = qm - ex
    m_scr[...] = jnp.full(m_scr.shape, NEG_BIG, F32)
    acc_scr[...] = jnp.zeros(acc_scr.shape, F32)

    def left(kb, carry):
        step(kb, Q_LEFT)
        return carry

    def right(kb, carry):
        step(kb, Q_RIGHT)
        return carry

    lax.fori_loop(0, qi, left, 0)
    step(qi, Q_DIAG)
    lax.fori_loop(qi + 1, nk, right, 0)

    heads = []
    for h in range(DA_HEADS):
        o = [acc_scr[h, m, :DA_V, :] / acc_scr[h, m, X_SUM:X_SUM + 1, :] for m in range(2)]
        heads.append(o[0] - lam * o[1])

    full = jnp.concatenate(heads, axis=0).T
    y = full * lax.rsqrt(_group_mean_sq(full, DA_V) + EPS) * subg_ref[...]
    o_ref[0] = (y * z_ref[0].astype(F32)).astype(BF)


def _da(qt4, k4, vt4, p3, lamv, subg, lam_init, bq=512):
    b, _, t, _ = k4.shape
    return pl.pallas_call(
        functools.partial(_da_kernel, lam_init),
        grid=(b, t // bq),
        in_specs=[
            pl.BlockSpec((1, DA_HEADS, DA_LANES, bq), lambda i, j: (i, 0, 0, j)),
            pl.BlockSpec((1, DA_HEADS, t, DA_LANES), lambda i, j: (i, 0, 0, 0)),
            pl.BlockSpec((1, DA_HEADS, DA_VT_ROWS, t), lambda i, j: (i, 0, 0, 0)),
            pl.BlockSpec((1, bq, BRANCH_W), lambda i, j: (i, j, S_BZ)),
            _resident((4, DA_QK), lambda i, j: (0, 0)),
            _resident((1, BRANCH_W), lambda i, j: (0, 0)),
        ],
        out_specs=pl.BlockSpec((1, bq, BRANCH_W), lambda i, j: (i, j, 0)),
        out_shape=jax.ShapeDtypeStruct((b, t, BRANCH_W), BF),
        scratch_shapes=[
            pltpu.VMEM((DA_HEADS, 3, 2, DA_LANES, bq), BF),
            pltpu.VMEM((bq, bq), F32),
            pltpu.VMEM((DA_HEADS, 2, 1, bq), F32),
            pltpu.VMEM((DA_HEADS, 2, DA_VT_ROWS, bq), F32),
        ],
        compiler_params=_cparams("parallel", "arbitrary"),
        name="da",
    )(qt4, k4, vt4, p3, lamv, subg)


POOL_HALO = 16


def _pool_kernel(x_ref, z_ref, band_ref, cw_ref, cs_ref, o_ref):
    tt = z_ref.shape[1]
    t_total = x_ref.shape[1]
    nt = t_total // tt
    ti = pl.program_id(1)
    t0 = pl.multiple_of(ti * tt, tt)
    xm = x_ref[0, pl.ds(t0, tt), :]
    p0 = pl.multiple_of(jnp.maximum(t0 - POOL_HALO, 0), POOL_HALO)
    n0 = pl.multiple_of(jnp.minimum(t0 + tt, t_total - POOL_HALO), POOL_HALO)
    has_prev = (ti > 0).astype(F32)
    has_next = (ti < nt - 1).astype(F32)
    prev = (x_ref[0, pl.ds(p0, POOL_HALO), :].astype(F32) * has_prev).astype(BF)
    nxt = (x_ref[0, pl.ds(n0, POOL_HALO), :].astype(F32) * has_next).astype(BF)
    xcat = jnp.concatenate([prev, xm, nxt], axis=0)

    group = _lane_group((tt, BRANCH_W), 6)
    wsum = jnp.zeros((tt, BRANCH_W), F32)
    for g in range(len(POOL_WINDOWS)):
        wsum = jnp.where(group == g, _dot(band_ref[g], xcat), wsum)

    tpos = t0 + lax.broadcasted_iota(jnp.int32, (tt, BRANCH_W), 0)
    half = jnp.left_shift(1, group)
    lo = jnp.maximum(tpos - half, 0)
    hi = jnp.minimum(tpos + half - 1, t_total - 1)
    cnt = (hi - lo + 1).astype(F32)
    dlt = wsum / cnt - xm.astype(F32)
    y = _dot(dlt.astype(BF), cw_ref[...]) * cs_ref[...]
    o_ref[0] = (y * z_ref[0].astype(F32)).astype(BF)


def _pool_band(tt):
    band = np.zeros((len(POOL_WINDOWS), tt, tt + 2 * POOL_HALO), np.float32)
    t = np.arange(tt)[:, None]
    u = np.arange(tt + 2 * POOL_HALO)[None, :] - POOL_HALO
    for g, w in enumerate(POOL_WINDOWS):
        band[g] = ((u >= t - w // 2) & (u <= t - w // 2 + w - 1)).astype(np.float32)
    return jnp.asarray(band, BF)


def _pool(p3, cw_bd, c_scale, tt=256):
    b, t, _ = p3.shape
    band = _pool_band(tt)
    return pl.pallas_call(
        _pool_kernel,
        grid=(b, t // tt),
        in_specs=[
            pl.BlockSpec((1, t, BRANCH_W), lambda i, j: (i, 0, S_CX)),
            pl.BlockSpec((1, tt, BRANCH_W), lambda i, j: (i, j, S_CZ)),
            _resident(band.shape, lambda i, j: (0, 0, 0)),
            _resident((BRANCH_W, BRANCH_W), lambda i, j: (0, 0)),
            _resident((1, BRANCH_W), lambda i, j: (0, 0)),
        ],
        out_specs=pl.BlockSpec((1, tt, BRANCH_W), lambda i, j: (i, j, 0)),
        out_shape=jax.ShapeDtypeStruct((b, t, BRANCH_W), BF),
        compiler_params=_cparams("parallel", "arbitrary"),
        name="pool",
    )(p3, p3, band, cw_bd, c_scale)


def _sg_kernel(u_ref, v_ref, z_ref, ws_ref, bias_ref, o_ref):
    tt = u_ref.shape[0]
    group = _lane_group((SG_CHUNK, BRANCH_W), 6)
    for c in range(tt // SG_CHUNK):
        rows = slice(c * SG_CHUNK, (c + 1) * SG_CHUNK)
        vc = v_ref[rows, :]
        mix = jnp.zeros((SG_CHUNK, BRANCH_W), F32)
        for g in range(SG_GROUPS):
            mix = jnp.where(group == g, _dot(ws_ref[g], vc), mix)
        y = u_ref[rows, :].astype(F32) * (mix + bias_ref[...]) * z_ref[rows, :].astype(F32)
        o_ref[rows, :] = y.astype(BF)


def _sg(p2, ws, sg_bias, tt=512):
    n = p2.shape[0]
    return pl.pallas_call(
        _sg_kernel,
        grid=(n // tt,),
        in_specs=[
            pl.BlockSpec((tt, BRANCH_W), lambda i: (i, S_DU)),
            pl.BlockSpec((tt, BRANCH_W), lambda i: (i, S_DV)),
            pl.BlockSpec((tt, BRANCH_W), lambda i: (i, S_DZ)),
            _resident((SG_GROUPS, SG_CHUNK, SG_CHUNK), lambda i: (0, 0, 0)),
            _resident((SG_CHUNK, BRANCH_W), lambda i: (0, 0)),
        ],
        out_specs=pl.BlockSpec((tt, BRANCH_W), lambda i: (i, 0)),
        out_shape=jax.ShapeDtypeStruct((n, BRANCH_W), BF),
        compiler_params=_cparams("parallel"),
        name="sg",
    )(p2, p2, p2, ws, sg_bias)


def _mem_kernel(q_ref, z_ref, k_ref, v_ref, o_ref):
    tt = q_ref.shape[1]
    q = q_ref[0]
    k = k_ref[0]
    v = v_ref[0]
    head_of_lane = _lane_group((tt, BRANCH_W), 6)
    o = jnp.zeros((tt, BRANCH_W), F32)
    for h in range(MEM_HEADS):
        qm = q * _lane_mask(q.shape, h * MEM_HD, (h + 1) * MEM_HD, BF)
        s = _dot_nt(qm, k)
        m = jnp.max(s, axis=-1, keepdims=True)
        p = jnp.exp(s - m)
        l = jnp.sum(p, axis=-1, keepdims=True)
        oh = _dot(p.astype(BF), v) / l
        o = jnp.where(head_of_lane == h, oh, o)
    o_ref[0] = (o * z_ref[0].astype(F32)).astype(BF)


def _mem_attn(p3, mk, mv, tt=512):
    b, t, _ = p3.shape
    m = mk.shape[1]
    return pl.pallas_call(
        _mem_kernel,
        grid=(b, t // tt),
        in_specs=[
            pl.BlockSpec((1, tt, BRANCH_W), lambda i, j: (i, j, S_MQ)),
            pl.BlockSpec((1, tt, BRANCH_W), lambda i, j: (i, j, S_MZ)),
            pl.BlockSpec((1, m, BRANCH_W), lambda i, j: (i, 0, 0)),
            pl.BlockSpec((1, m, BRANCH_W), lambda i, j: (i, 0, 0)),
        ],
        out_specs=pl.BlockSpec((1, tt, BRANCH_W), lambda i, j: (i, j, 0)),
        out_shape=jax.ShapeDtypeStruct((b, t, BRANCH_W), BF),
        compiler_params=_cparams("parallel", "arbitrary"),
        name="mem",
    )(p3, p3, mk, mv)


def _final_kernel(x_ref, ng_ref, ya_ref, yb_ref, yc_ref, yd_ref, ym_ref,
                  wg_ref, bg_ref, wb_ref, wo_ref, o_ref):
    x = x_ref[...]
    h = _rmsnorm_rows(x, ng_ref[...]).astype(BF)
    merged = jnp.zeros(x.shape, F32)
    for i, y_ref in enumerate((ya_ref, yb_ref, yc_ref, yd_ref, ym_ref)):
        logit = _dot(h, wg_ref[:, i * D_MODEL:(i + 1) * D_MODEL]) + bg_ref[i:i + 1, :]
        merged = merged + _sigmoid(logit) * _dot(y_ref[...], wb_ref[i])
    o_ref[...] = x + _dot(merged.astype(BF), wo_ref[...])


def _final(x2, norm_g, ys, w_gate, b_gate, w_branch, w_out, tm=512):
    n = x2.shape[0]
    yspec = pl.BlockSpec((tm, BRANCH_W), lambda i: (i, 0))
    return pl.pallas_call(
        _final_kernel,
        grid=(n // tm,),
        in_specs=[
            pl.BlockSpec((tm, D_MODEL), lambda i: (i, 0)),
            _resident((1, D_MODEL), lambda i: (0, 0)),
            yspec, yspec, yspec, yspec, yspec,
            _resident((D_MODEL, N_BRANCH * D_MODEL), lambda i: (0, 0)),
            _resident((8, D_MODEL), lambda i: (0, 0)),
            _resident((N_BRANCH, BRANCH_W, D_MODEL), lambda i: (0, 0, 0)),
            _resident((D_MODEL, D_MODEL), lambda i: (0, 0)),
        ],
        out_specs=pl.BlockSpec((tm, D_MODEL), lambda i: (i, 0)),
        out_shape=jax.ShapeDtypeStruct((n, D_MODEL), F32),
        compiler_params=_cparams("parallel"),
        name="final",
    )(x2, norm_g, *ys, w_gate, b_gate, w_branch, w_out)


def _tile4(g):
    return jnp.tile(g.astype(F32), BRANCH_W // g.shape[0])


def _layer(x, mem, layer_idx, norm_g, w_in, b_gate, a_qn_g, a_kn_g, a_rpb,
           b_qn_g, b_kn_g, b_lam_q1, b_lam_k1, b_lam_q2, b_lam_k2, b_sub_g,
           c_w, c_scale, d_ln_g, d_ln_b, d_ws, d_bs,
           m_norm_g, m_wkv, m_qn_g, m_kn_g, w_branch, w_out):
    b, t, d = x.shape
    n = b * t
    x2 = x.reshape(n, d)
    ng = norm_g.reshape(1, d).astype(F32)

    gains = jnp.zeros((8, BRANCH_W), F32)
    gains = gains.at[G_AQ].set(_tile4(a_qn_g) * NA_HD ** -0.5)
    gains = gains.at[G_AK].set(_tile4(a_kn_g))
    gains = gains.at[G_BQ].set(_tile4(b_qn_g) * (DA_QK ** -0.5 * LOG2E))
    gains = gains.at[G_BK].set(_tile4(b_kn_g))
    gains = gains.at[G_MQ].set(_tile4(m_qn_g) * MEM_HD ** -0.5)
    gains = gains.at[G_LNG].set(d_ln_g.astype(F32))
    gains = gains.at[G_LNB].set(d_ln_b.astype(F32))

    p2 = _proj(x2, ng, w_in[:, :SLAB_COLS].astype(BF), gains)
    p3 = p2.reshape(b, t, SLAB_COLS)

    y_a = _na(p3, _na_bias_table(a_rpb, t // GRID_W))

    lam_init = 0.8 - 0.6 * math.exp(-0.3 * layer_idx)
    qt4, k4, vt4 = _dprep(p3)
    lamv = jnp.stack([b_lam_q1, b_lam_k1, b_lam_q2, b_lam_k2]).astype(F32)
    subg = (_tile4(b_sub_g) * (1.0 - lam_init)).reshape(1, BRANCH_W)
    y_b = _da(qt4, k4, vt4, p3, lamv, subg, lam_init)

    cw_bd = jnp.zeros((BRANCH_W, BRANCH_W), F32)
    for g in range(len(POOL_WINDOWS)):
        cw_bd = cw_bd.at[g * POOL_GC:(g + 1) * POOL_GC, g * POOL_GC:(g + 1) * POOL_GC].set(c_w[g].astype(F32))
    y_c = _pool(p3, cw_bd.astype(BF), c_scale.reshape(1, BRANCH_W).astype(F32))

    sg_bias = jnp.repeat(d_bs.astype(F32).T, SG_GC, axis=1)
    y_d = _sg(p2, d_ws.astype(BF), sg_bias)

    mk, mv = _memkv(mem, m_norm_g.reshape(1, d).astype(F32), m_wkv.astype(BF),
                    _tile4(m_kn_g).reshape(1, BRANCH_W))
    y_m = _mem_attn(p3, mk, mv)

    bg = jnp.zeros((8, d), F32).at[:N_BRANCH].set(b_gate.astype(F32))
    ys = [y.reshape(n, BRANCH_W) for y in (y_a, y_b, y_c, y_d, y_m)]
    out = _final(x2, ng, ys, w_in[:, SLAB_COLS:].astype(BF), bg,
                 w_branch.astype(BF), w_out.astype(BF))
    return out.reshape(b, t, d)


def kernel(x, mem, norm_g, w_in, b_gate, a_qn_g, a_kn_g, a_rpb, b_qn_g, b_kn_g, b_lam_q1, b_lam_k1, b_lam_q2, b_lam_k2, b_sub_g, c_w, c_scale, d_ln_g, d_ln_b, d_ws, d_bs, m_norm_g, m_wkv, m_qn_g, m_kn_g, w_branch, w_out):
    for l in range(DEPTH):
        x = _layer(x, mem, l, norm_g[l], w_in[l], b_gate[l], a_qn_g[l], a_kn_g[l], a_rpb[l],
                   b_qn_g[l], b_kn_g[l], b_lam_q1[l], b_lam_k1[l], b_lam_q2[l], b_lam_k2[l], b_sub_g[l],
                   c_w[l], c_scale[l], d_ln_g[l], d_ln_b[l], d_ws[l], d_bs[l],
                   m_norm_g[l], m_wkv[l], m_qn_g[l], m_kn_g[l], w_branch[l], w_out[l])
    return x
```

```python
import functools
import math

import numpy as np
import jax
import jax.numpy as jnp
from jax import lax
from jax.experimental import pallas as pl
from jax.experimental.pallas import tpu as pltpu

F32 = jnp.float32
BF = jnp.bfloat16

D_MODEL = 1024
DEPTH = 2
GRID_W = 64
BRANCH_W = 256
N_BRANCH = 5
NA_HEADS = 4
NA_HD = 64
NA_KH = 8
NA_KW = 16
DA_HEADS = 4
DA_QK = 32
DA_V = 64
ALIBI_BASE = 8.0
POOL_WINDOWS = (2, 4, 8, 16)
POOL_GC = 64
SG_CHUNK = 128
SG_GROUPS = 4
SG_GC = 64
MEM_HEADS = 4
MEM_HD = 64
N_SLABS = 15
SLAB_COLS = N_SLABS * BRANCH_W
EPS = 1e-6

(S_AQ, S_AK, S_AV, S_AZ, S_BQ, S_BK, S_BV, S_BZ, S_CX, S_CZ,
 S_DU, S_DV, S_DZ, S_MQ, S_MZ) = range(N_SLABS)

(G_AQ, G_AK, G_BQ, G_BK, G_MQ, G_LNG, G_LNB) = range(7)

VMEM_LIMIT_BYTES = 56 * 1024 * 1024
NEG_BIG = -1e30

ALIBI_SLOPES = tuple(2.0 ** (-ALIBI_BASE * (h + 1) / DA_HEADS) for h in range(DA_HEADS))
DA_LANES = 128


def _bf16_pieces(x, n):
    out = []
    for _ in range(n):
        piece = float(np.asarray(x, np.float32).astype(jnp.bfloat16).astype(np.float32))
        out.append(piece)
        x -= piece
    return tuple(out)


LOG2E = math.log2(math.e)
LOG2E_PIECES = _bf16_pieces(LOG2E, 3)
N_PIECES = len(LOG2E_PIECES)
X_KHI, X_KLO, X_QHI, X_QLO = (DA_V + g * N_PIECES for g in range(4))
X_END = DA_V + 4 * N_PIECES
X_ROWS = 16


def _piece_of(idx):
    out = jnp.zeros(idx.shape, F32)
    for g in range(4):
        for p, piece in enumerate(LOG2E_PIECES):
            out = jnp.where(idx == DA_V + g * N_PIECES + p, piece, out)
    return out


def _cparams(*sem):
    return pltpu.CompilerParams(dimension_semantics=sem, vmem_limit_bytes=VMEM_LIMIT_BYTES)


def _resident(shape, index_map):
    return pl.BlockSpec(shape, index_map, pipeline_mode=pl.Buffered(1))


def _dot(a, b):
    return jnp.dot(a, b, preferred_element_type=F32)


def _dot_nt(a, b):
    return lax.dot_general(a, b, (((1,), (1,)), ((), ())), preferred_element_type=F32)


def _lane_group(shape, log2_group):
    return lax.broadcasted_iota(jnp.int32, shape, len(shape) - 1) >> log2_group


def _lane_mask(shape, lo, hi, dtype):
    lane = lax.broadcasted_iota(jnp.int32, shape, len(shape) - 1)
    return jnp.where((lane >= lo) & (lane < hi), 1.0, 0.0).astype(dtype)


def _group_mean_sq(xf, group):
    n = xf.shape[-1]
    sh = int(math.log2(group))
    r = lax.broadcasted_iota(jnp.int32, (n, n), 0) >> sh
    c = lax.broadcasted_iota(jnp.int32, (n, n), 1) >> sh
    pm = jnp.where(r == c, 1.0 / group, 0.0).astype(BF)
    xx = xf * xf
    hi = xx.astype(BF)
    lo = (xx - hi.astype(F32)).astype(BF)
    return _dot(hi, pm) + _dot(lo, pm)


def _sigmoid(x):
    return 1.0 / (1.0 + jnp.exp(-x))


def _silu(x):
    return x * _sigmoid(x)


def _gelu_tanh(x):
    return 0.5 * x * (1.0 + jnp.tanh(math.sqrt(2.0 / math.pi) * (x + 0.044715 * (x * x * x))))


def _rmsnorm_rows(x, g):
    ms = jnp.mean(x * x, axis=-1, keepdims=True)
    return x * lax.rsqrt(ms + EPS) * g


PROJ_LOOKAHEAD = 2


def _proj_kernel(x_ref, ng_ref, w_ref, gains_ref, o_ref):
    h = _rmsnorm_rows(x_ref[...], ng_ref[...]).astype(BF)

    def gnorm(r, group, row):
        return r * lax.rsqrt(_group_mean_sq(r, group) + EPS) * gains_ref[row:row + 1, :]

    def slab(j):
        return _dot(h, w_ref[:, j * BRANCH_W:(j + 1) * BRANCH_W])

    pending = [slab(j) for j in range(PROJ_LOOKAHEAD)]
    for j in range(N_SLABS):
        if j + PROJ_LOOKAHEAD < N_SLABS:
            pending.append(slab(j + PROJ_LOOKAHEAD))
        r = pending.pop(0)
        if j == S_AQ:
            r = gnorm(r, NA_HD, G_AQ)
        elif j == S_AK:
            r = gnorm(r, NA_HD, G_AK)
        elif j == S_BQ:
            r = gnorm(r, DA_QK, G_BQ)
        elif j == S_BK:
            r = gnorm(r, DA_QK, G_BK)
        elif j == S_MQ:
            r = gnorm(r, MEM_HD, G_MQ)
        elif j in (S_AZ, S_BZ, S_CZ, S_DZ, S_MZ):
            r = _silu(r)
        elif j == S_DU:
            r = _gelu_tanh(r)
        elif j == S_DV:
            v = _gelu_tanh(r)
            mu = jnp.mean(v, axis=-1, keepdims=True)
            vc = v - mu
            var = jnp.mean(vc * vc, axis=-1, keepdims=True)
            r = vc * lax.rsqrt(var + EPS) * gains_ref[G_LNG:G_LNG + 1, :] + gains_ref[G_LNB:G_LNB + 1, :]
        o_ref[:, j * BRANCH_W:(j + 1) * BRANCH_W] = r.astype(BF)


def _proj(x2, norm_g, w_slab, gains, tm=512):
    n = x2.shape[0]
    return pl.pallas_call(
        _proj_kernel,
        grid=(n // tm,),
        in_specs=[
            pl.BlockSpec((tm, D_MODEL), lambda i: (i, 0)),
            _resident((1, D_MODEL), lambda i: (0, 0)),
            _resident((D_MODEL, SLAB_COLS), lambda i: (0, 0)),
            _resident((8, BRANCH_W), lambda i: (0, 0)),
        ],
        out_specs=pl.BlockSpec((tm, SLAB_COLS), lambda i: (i, 0)),
        out_shape=jax.ShapeDtypeStruct((n, SLAB_COLS), BF),
        compiler_params=_cparams("parallel"),
        name="proj",
    )(x2, norm_g, w_slab, gains)


def _memkv_kernel(mem_ref, g_ref, w_ref, kg_ref, k_ref, v_ref):
    h = _rmsnorm_rows(mem_ref[0], g_ref[...]).astype(BF)
    k = _dot(h, w_ref[:, :BRANCH_W])
    v = _dot(h, w_ref[:, BRANCH_W:])
    k = k * lax.rsqrt(_group_mean_sq(k, MEM_HD) + EPS) * kg_ref[...]
    k_ref[0] = k.astype(BF)
    v_ref[0] = v.astype(BF)


def _memkv(mem, m_norm_g, m_wkv, kg):
    b, m, _ = mem.shape
    return pl.pallas_call(
        _memkv_kernel,
        grid=(b,),
        in_specs=[
            pl.BlockSpec((1, m, D_MODEL), lambda i: (i, 0, 0)),
            _resident((1, D_MODEL), lambda i: (0, 0)),
            _resident((D_MODEL, 2 * BRANCH_W), lambda i: (0, 0)),
            _resident((1, BRANCH_W), lambda i: (0, 0)),
        ],
        out_specs=[pl.BlockSpec((1, m, BRANCH_W), lambda i: (i, 0, 0)),
                   pl.BlockSpec((1, m, BRANCH_W), lambda i: (i, 0, 0))],
        out_shape=[jax.ShapeDtypeStruct((b, m, BRANCH_W), BF)] * 2,
        compiler_params=_cparams("parallel"),
        name="memkv",
    )(mem, m_norm_g, m_wkv, kg)


NA_ROWS_PER_STEP = 8
NA_WIN = NA_KH * GRID_W
NA_LOOKAHEAD = 2


def _na_kernel(q_ref, k_ref, v_ref, z_ref, bias_ref, o_ref):
    n_rows = k_ref.shape[1] // GRID_W
    rb = pl.program_id(1)
    head_of_lane = _lane_group((GRID_W, BRANCH_W), 6)
    masks = [_lane_mask((GRID_W, BRANCH_W), h * NA_HD, (h + 1) * NA_HD, BF) for h in range(NA_HEADS)]

    def window(rr):
        r = rb * NA_ROWS_PER_STEP + rr
        rs = jnp.clip(r - NA_KH // 2, 0, n_rows - NA_KH)
        cls = jnp.where(r < NA_KH // 2, r, jnp.where(r > n_rows - NA_KH // 2, r - (n_rows - NA_KH), NA_KH // 2))
        return pl.multiple_of(rs * GRID_W, GRID_W), cls

    def scores(rr):
        w0, cls = window(rr)
        q = q_ref[0, rr * GRID_W:(rr + 1) * GRID_W, :]
        qs = jnp.concatenate([q * masks[h] for h in range(NA_HEADS)], axis=0)
        return _dot_nt(qs, k_ref[0, pl.ds(w0, NA_WIN), :]) + bias_ref[cls]

    pending = [scores(rr) for rr in range(NA_LOOKAHEAD)]
    for rr in range(NA_ROWS_PER_STEP):
        if rr + NA_LOOKAHEAD < NA_ROWS_PER_STEP:
            pending.append(scores(rr + NA_LOOKAHEAD))
        s = pending.pop(0)
        w0, _ = window(rr)
        m = jnp.max(s, axis=-1, keepdims=True)
        p = jnp.exp(s - m)
        l = jnp.sum(p, axis=-1, keepdims=True)
        pv = _dot(p.astype(BF), v_ref[0, pl.ds(w0, NA_WIN), :]) / l
        o = jnp.zeros((GRID_W, BRANCH_W), F32)
        for h in range(NA_HEADS):
            o = jnp.where(head_of_lane == h, pv[h * GRID_W:(h + 1) * GRID_W], o)
        rows = slice(rr * GRID_W, (rr + 1) * GRID_W)
        o_ref[0, rows, :] = (o * z_ref[0, rows, :].astype(F32)).astype(BF)


def _na(p3, bias):
    b, t, _ = p3.shape
    tq = NA_ROWS_PER_STEP * GRID_W
    return pl.pallas_call(
        _na_kernel,
        grid=(b, t // tq),
        in_specs=[
            pl.BlockSpec((1, tq, BRANCH_W), lambda i, j: (i, j, S_AQ)),
            pl.BlockSpec((1, t, BRANCH_W), lambda i, j: (i, 0, S_AK)),
            pl.BlockSpec((1, t, BRANCH_W), lambda i, j: (i, 0, S_AV)),
            pl.BlockSpec((1, tq, BRANCH_W), lambda i, j: (i, j, S_AZ)),
            _resident(bias.shape, lambda i, j: (0, 0, 0)),
        ],
        out_specs=pl.BlockSpec((1, tq, BRANCH_W), lambda i, j: (i, j, 0)),
        out_shape=jax.ShapeDtypeStruct((b, t, BRANCH_W), BF),
        compiler_params=_cparams("parallel", "arbitrary"),
        name="na",
    )(p3, p3, p3, p3, bias)


NA_REL_ROWS = 2 * NA_KH - 1
NA_REL_COLS = 2 * NA_KW - 1
NA_CLASSES = NA_KH


def _na_class_roff(n_rows):
    half = NA_KH // 2
    rows = np.array(list(range(half)) + [half] + [n_rows - half + 1 + i for i in range(half - 1)])
    rs = np.clip(rows - half, 0, n_rows - NA_KH)
    return [int(v) for v in rs - rows + NA_KH - 1]


def _nabias_kernel(roff0, rpb_ref, o_ref, f_scr):
    h = pl.program_id(0)
    shape = (GRID_W, 2 * GRID_W)
    q = lax.broadcasted_iota(jnp.int32, shape, 0)
    lane = lax.broadcasted_iota(jnp.int32, shape, 1)
    k = lane & (GRID_W - 1)
    coff = k - q + (NA_KW - 1)
    cs = jnp.clip(q - NA_KW // 2, 0, GRID_W - NA_KW)
    valid = (k >= cs) & (k < cs + NA_KW)
    first = lane < GRID_W

    def tile(r, carry):
        base = (h * NA_REL_ROWS + r) * NA_REL_COLS
        acc = jnp.zeros(shape, F32)
        for s in range(NA_REL_COLS):
            acc = jnp.where(coff == s, jnp.where(first, rpb_ref[base + s], rpb_ref[base + NA_REL_COLS + s]), acc)
        f_scr[r] = jnp.where(valid, acc, NEG_BIG)
        return carry

    lax.fori_loop(0, NA_REL_ROWS - 1, tile, 0)
    for c in range(NA_CLASSES):
        for p in range(NA_KH // 2):
            o_ref[c, 0, :, p * 2 * GRID_W:(p + 1) * 2 * GRID_W] = f_scr[roff0[c] + 2 * p]


def _na_bias_table(rpb, n_rows):
    return pl.pallas_call(
        functools.partial(_nabias_kernel, _na_class_roff(n_rows)),
        grid=(NA_HEADS,),
        in_specs=[pl.BlockSpec(memory_space=pltpu.SMEM)],
        out_specs=pl.BlockSpec((NA_CLASSES, 1, GRID_W, NA_WIN), lambda h: (0, h, 0, 0)),
        out_shape=jax.ShapeDtypeStruct((NA_CLASSES, NA_HEADS, GRID_W, NA_WIN), F32),
        scratch_shapes=[pltpu.VMEM((NA_REL_ROWS - 1, GRID_W, 2 * GRID_W), F32)],
        compiler_params=_cparams("arbitrary"),
        name="nabias",
    )(rpb.astype(F32).reshape(-1)).reshape(NA_CLASSES, NA_HEADS * GRID_W, NA_WIN)


DA_VT_ROWS = 80
X_SUM = DA_V


def _dprep_kernel(q_ref, k_ref, v_ref, qt_ref, k4_ref, vt_ref):
    tt = q_ref.shape[1]
    t0 = pl.program_id(1) * tt
    q = q_ref[0]
    k = k_ref[0]
    v = v_ref[0]
    src = lax.broadcasted_iota(jnp.int32, (BRANCH_W, DA_LANES), 0)
    dst = lax.broadcasted_iota(jnp.int32, (BRANCH_W, DA_LANES), 1)
    src_t = lax.broadcasted_iota(jnp.int32, (DA_LANES, BRANCH_W), 1)
    dst_t = lax.broadcasted_iota(jnp.int32, (DA_LANES, BRANCH_W), 0)
    klane = lax.broadcasted_iota(jnp.int32, (tt, DA_LANES), 1)
    kpos = t0 + lax.broadcasted_iota(jnp.int32, (tt, DA_LANES), 0)
    vrow = lax.broadcasted_iota(jnp.int32, (DA_VT_ROWS, tt), 0)
    for h in range(DA_HEADS):
        sel = jnp.where((src == h * DA_V + dst) & (dst < DA_V), 1.0, 0.0).astype(BF)
        sel_t = jnp.where((src_t == h * DA_V + dst_t) & (dst_t < DA_V), 1.0, 0.0).astype(BF)
        qt_ref[0, h] = _dot_nt(sel_t, q).astype(BF)
        vt = _dot_nt(sel_t[:DA_VT_ROWS], v)
        vt_ref[0, h] = jnp.where(vrow == X_SUM, 1.0, vt).astype(BF)
        slope = ALIBI_SLOPES[h]
        k4 = _dot(k, sel)
        k4 = jnp.where((klane >= X_KHI) & (klane < X_KLO), (kpos >> 6).astype(F32) * (slope * 64.0), k4)
        k4 = jnp.where((klane >= X_KLO) & (klane < X_QHI), (kpos & 63).astype(F32) * slope, k4)
        k4 = jnp.where((klane >= X_QHI) & (klane < X_END), _piece_of(klane), k4)
        k4_ref[0, h] = k4.astype(BF)


def _dprep(p3, tt=512):
    b, t, _ = p3.shape
    return pl.pallas_call(
        _dprep_kernel,
        grid=(b, t // tt),
        in_specs=[
            pl.BlockSpec((1, tt, BRANCH_W), lambda i, j: (i, j, S_BQ)),
            pl.BlockSpec((1, tt, BRANCH_W), lambda i, j: (i, j, S_BK)),
            pl.BlockSpec((1, tt, BRANCH_W), lambda i, j: (i, j, S_BV)),
        ],
        out_specs=[
            pl.BlockSpec((1, DA_HEADS, DA_LANES, tt), lambda i, j: (i, 0, 0, j)),
            pl.BlockSpec((1, DA_HEADS, tt, DA_LANES), lambda i, j: (i, 0, j, 0)),
            pl.BlockSpec((1, DA_HEADS, DA_VT_ROWS, tt), lambda i, j: (i, 0, 0, j)),
        ],
        out_shape=[
            jax.ShapeDtypeStruct((b, DA_HEADS, DA_LANES, t), BF),
            jax.ShapeDtypeStruct((b, DA_HEADS, t, DA_LANES), BF),
            jax.ShapeDtypeStruct((b, DA_HEADS, DA_VT_ROWS, t), BF),
        ],
        compiler_params=_cparams("parallel", "parallel"),
        name="dprep",
    )(p3, p3, p3)


DA_QCOLS = 256
DA_KROWS = 256
DA_LOOKAHEAD = 6
Q_LEFT, Q_DIAG, Q_RIGHT = range(3)


def _da_kernel(lam_init, qt_ref, k_ref, vt_ref, z_ref, lamv_ref, subg_ref, o_ref,
               qv_scr, dist_scr, spre_scr, m_scr, acc_scr):
    bq = qt_ref.shape[3]
    bk = bq
    nk = k_ref.shape[2] // bk
    qi = pl.program_id(1)
    frow = lax.broadcasted_iota(jnp.int32, (DA_V, bq), 0)
    xrow = DA_V + lax.broadcasted_iota(jnp.int32, (X_ROWS, bq), 0)
    qpos = qi * bq + lax.broadcasted_iota(jnp.int32, (X_ROWS, bq), 1)

    lv = lamv_ref[...]
    lam = (jnp.exp(jnp.sum(lv[0:1] * lv[1:2], axis=-1, keepdims=True))
           - jnp.exp(jnp.sum(lv[2:3] * lv[3:4], axis=-1, keepdims=True)) + lam_init)

    cols = [slice(c * DA_QCOLS, (c + 1) * DA_QCOLS) for c in range(bq // DA_QCOLS)]
    chains = [(h, m, c) for h in range(DA_HEADS) for m in range(2) for c in range(len(cols))]

    def tiles_of(blk):
        return [(blk, kc * DA_KROWS, ch) for kc in range(bk // DA_KROWS) for ch in chains]

    def scores(tile):
        (k0, side, diag), r0, (h, m, c) = tile
        s = _dot(k_ref[0, h, pl.ds(k0 + r0, DA_KROWS), :], qv_scr[h, side, m, :, cols[c]])
        if diag:
            s = s - dist_scr[r0:r0 + DA_KROWS, cols[c]] * (ALIBI_SLOPES[h] * LOG2E)
        return s

    def run_block(blk, nxt):
        tiles = tiles_of(blk)
        pending = []
        for i, ((k0, _, _), r0, (h, m, c)) in enumerate(tiles):
            j = i + DA_LOOKAHEAD
            if j < len(tiles):
                pending.append(scores(tiles[j]))
            elif nxt is not None:
                spre_scr[j - len(tiles)] = scores(tiles_of(nxt)[j - len(tiles)])
            s = spre_scr[i] if i < DA_LOOKAHEAD else pending.pop(0)
            m_old = m_scr[h, m, :, cols[c]]
            m_new = jnp.maximum(m_old, jnp.max(s, axis=0, keepdims=True))
            alpha = jnp.exp2(m_old - m_new)
            p = jnp.exp2(s - m_new).astype(BF)
            pv = _dot(vt_ref[0, h, :, pl.ds(k0 + r0, DA_KROWS)], p)
            acc_scr[h, m, :, cols[c]] = alpha * acc_scr[h, m, :, cols[c]] + pv
            m_scr[h, m, :, cols[c]] = m_new

    rel = lax.broadcasted_iota(jnp.int32, (bk, bq), 0) - lax.broadcasted_iota(jnp.int32, (bk, bq), 1)
    dist_scr[...] = jnp.abs(rel).astype(F32)
    pieces = _piece_of(xrow)
    tail = jnp.zeros((DA_LANES - DA_V - X_ROWS, bq), BF)
    for h in range(DA_HEADS):
        slope = ALIBI_SLOPES[h]
        ih = (qpos >> 6).astype(F32) * (slope * 64.0)
        il = (qpos & 63).astype(F32) * slope
        ex = jnp.where(xrow < X_QHI, pieces,
                       jnp.where(xrow < X_QLO, -ih, jnp.where(xrow < X_END, -il, 0.0))).astype(BF)
        for m in range(2):
            qm = qt_ref[0, h, :DA_V, :] * jnp.where(frow >> 5 == m, 1.0, 0.0).astype(BF)
            for side, extra in ((Q_LEFT, ex), (Q_DIAG, jnp.zeros_like(ex)), (Q_RIGHT, -ex)):
                qv_scr[h, side, m, :DA_V, :] = qm
                qv_scr[h, side, m, DA_V:DA_V + X_ROWS, :] = extra
                qv_scr[h, side, m, DA_V + X_ROWS:, :] = tail
    m_scr[...] = jnp.full(m_scr.shape, NEG_BIG, F32)
    acc_scr[...] = jnp.zeros(acc_scr.shape, F32)

    def off_diagonal(t):
        kb = jnp.where(t >= qi, t + 1, t)
        return pl.multiple_of(kb * bk, bk), jnp.where(t >= qi, Q_RIGHT, Q_LEFT), False

    diagonal = (pl.multiple_of(qi * bk, bk), Q_DIAG, True)

    for i, tile in enumerate(tiles_of(off_diagonal(0))[:DA_LOOKAHEAD]):
        spre_scr[i] = scores(tile)

    def body(t, carry):
        run_block(off_diagonal(t), off_diagonal(t + 1))
        return carry

    lax.fori_loop(0, nk - 2, body, 0)
    run_block(off_diagonal(nk - 2), diagonal)
    run_block(diagonal, None)

    heads = []
    for h in range(DA_HEADS):
        o = [acc_scr[h, m, :DA_V, :] / acc_scr[h, m, X_SUM:X_SUM + 1, :] for m in range(2)]
        heads.append(o[0] - lam * o[1])

    full = jnp.concatenate(heads, axis=0).T
    y = full * lax.rsqrt(_group_mean_sq(full, DA_V) + EPS) * subg_ref[...]
    o_ref[0] = (y * z_ref[0].astype(F32)).astype(BF)


def _da(qt4, k4, vt4, p3, lamv, subg, lam_init, bq=512):
    b, _, t, _ = k4.shape
    return pl.pallas_call(
        functools.partial(_da_kernel, lam_init),
        grid=(b, t // bq),
        in_specs=[
            pl.BlockSpec((1, DA_HEADS, DA_LANES, bq), lambda i, j: (i, 0, 0, j)),
            pl.BlockSpec((1, DA_HEADS, t, DA_LANES), lambda i, j: (i, 0, 0, 0)),
            pl.BlockSpec((1, DA_HEADS, DA_VT_ROWS, t), lambda i, j: (i, 0, 0, 0)),
            pl.BlockSpec((1, bq, BRANCH_W), lambda i, j: (i, j, S_BZ)),
            _resident((4, DA_QK), lambda i, j: (0, 0)),
            _resident((1, BRANCH_W), lambda i, j: (0, 0)),
        ],
        out_specs=pl.BlockSpec((1, bq, BRANCH_W), lambda i, j: (i, j, 0)),
        out_shape=jax.ShapeDtypeStruct((b, t, BRANCH_W), BF),
        scratch_shapes=[
            pltpu.VMEM((DA_HEADS, 3, 2, DA_LANES, bq), BF),
            pltpu.VMEM((bq, bq), F32),
            pltpu.VMEM((DA_LOOKAHEAD, DA_KROWS, DA_QCOLS), F32),
            pltpu.VMEM((DA_HEADS, 2, 1, bq), F32),
            pltpu.VMEM((DA_HEADS, 2, DA_VT_ROWS, bq), F32),
        ],
        compiler_params=_cparams("parallel", "arbitrary"),
        name="da",
    )(qt4, k4, vt4, p3, lamv, subg)


POOL_HALO = 16


def _pool_kernel(x_ref, z_ref, band_ref, cw_ref, cs_ref, o_ref):
    tt = z_ref.shape[1]
    t_total = x_ref.shape[1]
    nt = t_total // tt
    ti = pl.program_id(1)
    t0 = pl.multiple_of(ti * tt, tt)
    xm = x_ref[0, pl.ds(t0, tt), :]
    p0 = pl.multiple_of(jnp.maximum(t0 - POOL_HALO, 0), POOL_HALO)
    n0 = pl.multiple_of(jnp.minimum(t0 + tt, t_total - POOL_HALO), POOL_HALO)
    has_prev = (ti > 0).astype(F32)
    has_next = (ti < nt - 1).astype(F32)
    prev = (x_ref[0, pl.ds(p0, POOL_HALO), :].astype(F32) * has_prev).astype(BF)
    nxt = (x_ref[0, pl.ds(n0, POOL_HALO), :].astype(F32) * has_next).astype(BF)
    xcat = jnp.concatenate([prev, xm, nxt], axis=0)

    group = _lane_group((tt, BRANCH_W), 6)
    wsum = jnp.zeros((tt, BRANCH_W), F32)
    for g in range(len(POOL_WINDOWS)):
        wsum = jnp.where(group == g, _dot(band_ref[g], xcat), wsum)

    tpos = t0 + lax.broadcasted_iota(jnp.int32, (tt, BRANCH_W), 0)
    half = jnp.left_shift(1, group)
    lo = jnp.maximum(tpos - half, 0)
    hi = jnp.minimum(tpos + half - 1, t_total - 1)
    cnt = (hi - lo + 1).astype(F32)
    dlt = wsum / cnt - xm.astype(F32)
    y = _dot(dlt.astype(BF), cw_ref[...]) * cs_ref[...]
    o_ref[0] = (y * z_ref[0].astype(F32)).astype(BF)


def _pool_band(tt):
    band = np.zeros((len(POOL_WINDOWS), tt, tt + 2 * POOL_HALO), np.float32)
    t = np.arange(tt)[:, None]
    u = np.arange(tt + 2 * POOL_HALO)[None, :] - POOL_HALO
    for g, w in enumerate(POOL_WINDOWS):
        band[g] = ((u >= t - w // 2) & (u <= t - w // 2 + w - 1)).astype(np.float32)
    return jnp.asarray(band, BF)


def _pool(p3, cw_bd, c_scale, tt=256):
    b, t, _ = p3.shape
    band = _pool_band(tt)
    return pl.pallas_call(
        _pool_kernel,
        grid=(b, t // tt),
        in_specs=[
            pl.BlockSpec((1, t, BRANCH_W), lambda i, j: (i, 0, S_CX)),
            pl.BlockSpec((1, tt, BRANCH_W), lambda i, j: (i, j, S_CZ)),
            _resident(band.shape, lambda i, j: (0, 0, 0)),
            _resident((BRANCH_W, BRANCH_W), lambda i, j: (0, 0)),
            _resident((1, BRANCH_W), lambda i, j: (0, 0)),
        ],
        out_specs=pl.BlockSpec((1, tt, BRANCH_W), lambda i, j: (i, j, 0)),
        out_shape=jax.ShapeDtypeStruct((b, t, BRANCH_W), BF),
        compiler_params=_cparams("parallel", "arbitrary"),
        name="pool",
    )(p3, p3, band, cw_bd, c_scale)


def _sg_kernel(u_ref, v_ref, z_ref, ws_ref, bias_ref, o_ref):
    tt = u_ref.shape[0]
    group = _lane_group((SG_CHUNK, BRANCH_W), 6)
    for c in range(tt // SG_CHUNK):
        rows = slice(c * SG_CHUNK, (c + 1) * SG_CHUNK)
        vc = v_ref[rows, :]
        mix = jnp.zeros((SG_CHUNK, BRANCH_W), F32)
        for g in range(SG_GROUPS):
            mix = jnp.where(group == g, _dot(ws_ref[g], vc), mix)
        y = u_ref[rows, :].astype(F32) * (mix + bias_ref[...]) * z_ref[rows, :].astype(F32)
        o_ref[rows, :] = y.astype(BF)


def _sg(p2, ws, sg_bias, tt=512):
    n = p2.shape[0]
    return pl.pallas_call(
        _sg_kernel,
        grid=(n // tt,),
        in_specs=[
            pl.BlockSpec((tt, BRANCH_W), lambda i: (i, S_DU)),
            pl.BlockSpec((tt, BRANCH_W), lambda i: (i, S_DV)),
            pl.BlockSpec((tt, BRANCH_W), lambda i: (i, S_DZ)),
            _resident((SG_GROUPS, SG_CHUNK, SG_CHUNK), lambda i: (0, 0, 0)),
            _resident((SG_CHUNK, BRANCH_W), lambda i: (0, 0)),
        ],
        out_specs=pl.BlockSpec((tt, BRANCH_W), lambda i: (i, 0)),
        out_shape=jax.ShapeDtypeStruct((n, BRANCH_W), BF),
        compiler_params=_cparams("parallel"),
        name="sg",
    )(p2, p2, p2, ws, sg_bias)


def _mem_kernel(q_ref, z_ref, k_ref, v_ref, o_ref):
    tt = q_ref.shape[1]
    q = q_ref[0]
    k = k_ref[0]
    v = v_ref[0]
    head_of_lane = _lane_group((tt, BRANCH_W), 6)
    scores = [_dot_nt(q * _lane_mask(q.shape, h * MEM_HD, (h + 1) * MEM_HD, BF), k) for h in range(MEM_HEADS)]
    o = jnp.zeros((tt, BRANCH_W), F32)
    for h in range(MEM_HEADS):
        s = scores[h]
        m = jnp.max(s, axis=-1, keepdims=True)
        p = jnp.exp(s - m)
        l = jnp.sum(p, axis=-1, keepdims=True)
        oh = _dot(p.astype(BF), v) / l
        o = jnp.where(head_of_lane == h, oh, o)
    o_ref[0] = (o * z_ref[0].astype(F32)).astype(BF)


def _mem_attn(p3, mk, mv, tt=512):
    b, t, _ = p3.shape
    m = mk.shape[1]
    return pl.pallas_call(
        _mem_kernel,
        grid=(b, t // tt),
        in_specs=[
            pl.BlockSpec((1, tt, BRANCH_W), lambda i, j: (i, j, S_MQ)),
            pl.BlockSpec((1, tt, BRANCH_W), lambda i, j: (i, j, S_MZ)),
            pl.BlockSpec((1, m, BRANCH_W), lambda i, j: (i, 0, 0)),
            pl.BlockSpec((1, m, BRANCH_W), lambda i, j: (i, 0, 0)),
        ],
        out_specs=pl.BlockSpec((1, tt, BRANCH_W), lambda i, j: (i, j, 0)),
        out_shape=jax.ShapeDtypeStruct((b, t, BRANCH_W), BF),
        compiler_params=_cparams("parallel", "arbitrary"),
        name="mem",
    )(p3, p3, mk, mv)


def _final_kernel(x_ref, ng_ref, ya_ref, yb_ref, yc_ref, yd_ref, ym_ref,
                  wg_ref, bg_ref, wb_ref, wo_ref, o_ref):
    x = x_ref[...]
    h = _rmsnorm_rows(x, ng_ref[...]).astype(BF)
    y_refs = (ya_ref, yb_ref, yc_ref, yd_ref, ym_ref)

    def branch(i):
        return _dot(h, wg_ref[:, i * D_MODEL:(i + 1) * D_MODEL]), _dot(y_refs[i][...], wb_ref[i])

    merged = jnp.zeros(x.shape, F32)
    pending = [branch(0)]
    for i in range(N_BRANCH):
        if i + 1 < N_BRANCH:
            pending.append(branch(i + 1))
        logit, proj = pending.pop(0)
        merged = merged + _sigmoid(logit + bg_ref[i:i + 1, :]) * proj
    o_ref[...] = x + _dot(merged.astype(BF), wo_ref[...])


def _final(x2, norm_g, ys, w_gate, b_gate, w_branch, w_out, tm=512):
    n = x2.shape[0]
    yspec = pl.BlockSpec((tm, BRANCH_W), lambda i: (i, 0))
    return pl.pallas_call(
        _final_kernel,
        grid=(n // tm,),
        in_specs=[
            pl.BlockSpec((tm, D_MODEL), lambda i: (i, 0)),
            _resident((1, D_MODEL), lambda i: (0, 0)),
            yspec, yspec, yspec, yspec, yspec,
            _resident((D_MODEL, N_BRANCH * D_MODEL), lambda i: (0, 0)),
            _resident((8, D_MODEL), lambda i: (0, 0)),
            _resident((N_BRANCH, BRANCH_W, D_MODEL), lambda i: (0, 0, 0)),
            _resident((D_MODEL, D_MODEL), lambda i: (0, 0)),
        ],
        out_specs=pl.BlockSpec((tm, D_MODEL), lambda i: (i, 0)),
        out_shape=jax.ShapeDtypeStruct((n, D_MODEL), F32),
        compiler_params=_cparams("parallel"),
        name="final",
    )(x2, norm_g, *ys, w_gate, b_gate, w_branch, w_out)


def _tile4(g):
    return jnp.tile(g.astype(F32), BRANCH_W // g.shape[0])


def _layer(x, mem, layer_idx, norm_g, w_in, b_gate, a_qn_g, a_kn_g, a_rpb,
           b_qn_g, b_kn_g, b_lam_q1, b_lam_k1, b_lam_q2, b_lam_k2, b_sub_g,
           c_w, c_scale, d_ln_g, d_ln_b, d_ws, d_bs,
           m_norm_g, m_wkv, m_qn_g, m_kn_g, w_branch, w_out):
    b, t, d = x.shape
    n = b * t
    x2 = x.reshape(n, d)
    ng = norm_g.reshape(1, d).astype(F32)

    gains = jnp.zeros((8, BRANCH_W), F32)
    gains = gains.at[G_AQ].set(_tile4(a_qn_g) * NA_HD ** -0.5)
    gains = gains.at[G_AK].set(_tile4(a_kn_g))
    gains = gains.at[G_BQ].set(_tile4(b_qn_g) * (DA_QK ** -0.5 * LOG2E))
    gains = gains.at[G_BK].set(_tile4(b_kn_g))
    gains = gains.at[G_MQ].set(_tile4(m_qn_g) * MEM_HD ** -0.5)
    gains = gains.at[G_LNG].set(d_ln_g.astype(F32))
    gains = gains.at[G_LNB].set(d_ln_b.astype(F32))

    p2 = _proj(x2, ng, w_in[:, :SLAB_COLS].astype(BF), gains)
    p3 = p2.reshape(b, t, SLAB_COLS)

    y_a = _na(p3, _na_bias_table(a_rpb, t // GRID_W))

    lam_init = 0.8 - 0.6 * math.exp(-0.3 * layer_idx)
    qt4, k4, vt4 = _dprep(p3)
    lamv = jnp.stack([b_lam_q1, b_lam_k1, b_lam_q2, b_lam_k2]).astype(F32)
    subg = (_tile4(b_sub_g) * (1.0 - lam_init)).reshape(1, BRANCH_W)
    y_b = _da(qt4, k4, vt4, p3, lamv, subg, lam_init)

    cw_bd = jnp.zeros((BRANCH_W, BRANCH_W), F32)
    for g in range(len(POOL_WINDOWS)):
        cw_bd = cw_bd.at[g * POOL_GC:(g + 1) * POOL_GC, g * POOL_GC:(g + 1) * POOL_GC].set(c_w[g].astype(F32))
    y_c = _pool(p3, cw_bd.astype(BF), c_scale.reshape(1, BRANCH_W).astype(F32))

    sg_bias = jnp.repeat(d_bs.astype(F32).T, SG_GC, axis=1)
    y_d = _sg(p2, d_ws.astype(BF), sg_bias)

    mk, mv = _memkv(mem, m_norm_g.reshape(1, d).astype(F32), m_wkv.astype(BF),
                    _tile4(m_kn_g).reshape(1, BRANCH_W))
    y_m = _mem_attn(p3, mk, mv)

    bg = jnp.zeros((8, d), F32).at[:N_BRANCH].set(b_gate.astype(F32))
    ys = [y.reshape(n, BRANCH_W) for y in (y_a, y_b, y_c, y_d, y_m)]
    out = _final(x2, ng, ys, w_in[:, SLAB_COLS:].astype(BF), bg,
                 w_branch.astype(BF), w_out.astype(BF))
    return out.reshape(b, t, d)


def kernel(x, mem, norm_g, w_in, b_gate, a_qn_g, a_kn_g, a_rpb, b_qn_g, b_kn_g, b_lam_q1, b_lam_k1, b_lam_q2, b_lam_k2, b_sub_g, c_w, c_scale, d_ln_g, d_ln_b, d_ws, d_bs, m_norm_g, m_wkv, m_qn_g, m_kn_g, w_branch, w_out):
    for l in range(DEPTH):
        x = _layer(x, mem, l, norm_g[l], w_in[l], b_gate[l], a_qn_g[l], a_kn_g[l], a_rpb[l],
                   b_qn_g[l], b_kn_g[l], b_lam_q1[l], b_lam_k1[l], b_lam_q2[l], b_lam_k2[l], b_sub_g[l],
                   c_w[l], c_scale[l], d_ln_g[l], d_ln_b[l], d_ws[l], d_bs[l],
                   m_norm_g[l], m_wkv[l], m_qn_g[l], m_kn_g[l], w_branch[l], w_out[l])
    return x
```

```python
import functools
import math

import numpy as np
import jax
import jax.numpy as jnp
from jax import lax
from jax.experimental import pallas as pl
from jax.experimental.pallas import tpu as pltpu

F32 = jnp.float32
BF = jnp.bfloat16

D_MODEL = 1024
DEPTH = 2
GRID_W = 64
BRANCH_W = 256
N_BRANCH = 5
NA_HEADS = 4
NA_HD = 64
NA_KH = 8
NA_KW = 16
DA_HEADS = 4
DA_QK = 32
DA_V = 64
ALIBI_BASE = 8.0
POOL_WINDOWS = (2, 4, 8, 16)
POOL_GC = 64
SG_CHUNK = 128
SG_GROUPS = 4
SG_GC = 64
MEM_HEADS = 4
MEM_HD = 64
N_SLABS = 15
SLAB_COLS = N_SLABS * BRANCH_W
EPS = 1e-6

(S_AQ, S_AK, S_AV, S_AZ, S_BQ, S_BK, S_BV, S_BZ, S_CX, S_CZ,
 S_DU, S_DV, S_DZ, S_MQ, S_MZ) = range(N_SLABS)

(G_AQ, G_AK, G_BQ, G_BK, G_MQ, G_LNG, G_LNB) = range(7)

VMEM_LIMIT_BYTES = 56 * 1024 * 1024
NEG_BIG = -1e30

ALIBI_SLOPES = tuple(2.0 ** (-ALIBI_BASE * (h + 1) / DA_HEADS) for h in range(DA_HEADS))
DA_LANES = 128


def _bf16_pieces(x, n):
    out = []
    for _ in range(n):
        piece = float(np.asarray(x, np.float32).astype(jnp.bfloat16).astype(np.float32))
        out.append(piece)
        x -= piece
    return tuple(out)


LOG2E = math.log2(math.e)
LOG2E_PIECES = _bf16_pieces(LOG2E, 3)
N_PIECES = len(LOG2E_PIECES)
X_KHI, X_KLO, X_QHI, X_QLO = (DA_V + g * N_PIECES for g in range(4))
X_END = DA_V + 4 * N_PIECES
X_ROWS = 16


def _piece_of(idx):
    out = jnp.zeros(idx.shape, F32)
    for g in range(4):
        for p, piece in enumerate(LOG2E_PIECES):
            out = jnp.where(idx == DA_V + g * N_PIECES + p, piece, out)
    return out


def _cparams(*sem):
    return pltpu.CompilerParams(dimension_semantics=sem, vmem_limit_bytes=VMEM_LIMIT_BYTES)


def _resident(shape, index_map):
    return pl.BlockSpec(shape, index_map, pipeline_mode=pl.Buffered(1))


def _layer_resident(shape, layer):
    return pl.BlockSpec((None,) + tuple(shape), lambda *_: (layer,) + (0,) * len(shape),
                        pipeline_mode=pl.Buffered(1))


def _dot(a, b):
    return jnp.dot(a, b, preferred_element_type=F32)


def _dot_nt(a, b):
    return lax.dot_general(a, b, (((1,), (1,)), ((), ())), preferred_element_type=F32)


def _lane_group(shape, log2_group):
    return lax.broadcasted_iota(jnp.int32, shape, len(shape) - 1) >> log2_group


def _lane_mask(shape, lo, hi, dtype):
    lane = lax.broadcasted_iota(jnp.int32, shape, len(shape) - 1)
    return jnp.where((lane >= lo) & (lane < hi), 1.0, 0.0).astype(dtype)


def _group_mean_sq(xf, group):
    n = xf.shape[-1]
    sh = int(math.log2(group))
    r = lax.broadcasted_iota(jnp.int32, (n, n), 0) >> sh
    c = lax.broadcasted_iota(jnp.int32, (n, n), 1) >> sh
    pm = jnp.where(r == c, 1.0 / group, 0.0).astype(BF)
    xx = xf * xf
    hi = xx.astype(BF)
    lo = (xx - hi.astype(F32)).astype(BF)
    return _dot(hi, pm) + _dot(lo, pm)


def _sigmoid(x):
    return 1.0 / (1.0 + jnp.exp(-x))


def _silu(x):
    return x * _sigmoid(x)


def _gelu_tanh(x):
    return 0.5 * x * (1.0 + jnp.tanh(math.sqrt(2.0 / math.pi) * (x + 0.044715 * (x * x * x))))


def _rmsnorm_rows(x, g):
    ms = jnp.mean(x * x, axis=-1, keepdims=True)
    return x * lax.rsqrt(ms + EPS) * g


PROJ_LOOKAHEAD = 2


def _proj_kernel(x_ref, ng_ref, w_ref, gains_ref, o_ref):
    h = _rmsnorm_rows(x_ref[...], ng_ref[...]).astype(BF)

    def gnorm(r, group, row):
        return r * lax.rsqrt(_group_mean_sq(r, group) + EPS) * gains_ref[row:row + 1, :]

    def slab(j):
        return _dot(h, w_ref[:, j * BRANCH_W:(j + 1) * BRANCH_W])

    pending = [slab(j) for j in range(PROJ_LOOKAHEAD)]
    for j in range(N_SLABS):
        if j + PROJ_LOOKAHEAD < N_SLABS:
            pending.append(slab(j + PROJ_LOOKAHEAD))
        r = pending.pop(0)
        if j == S_AQ:
            r = gnorm(r, NA_HD, G_AQ)
        elif j == S_AK:
            r = gnorm(r, NA_HD, G_AK)
        elif j == S_BQ:
            r = gnorm(r, DA_QK, G_BQ)
        elif j == S_BK:
            r = gnorm(r, DA_QK, G_BK)
        elif j == S_MQ:
            r = gnorm(r, MEM_HD, G_MQ)
        elif j in (S_AZ, S_BZ, S_CZ, S_DZ, S_MZ):
            r = _silu(r)
        elif j == S_DU:
            r = _gelu_tanh(r)
        elif j == S_DV:
            v = _gelu_tanh(r)
            mu = jnp.mean(v, axis=-1, keepdims=True)
            vc = v - mu
            var = jnp.mean(vc * vc, axis=-1, keepdims=True)
            r = vc * lax.rsqrt(var + EPS) * gains_ref[G_LNG:G_LNG + 1, :] + gains_ref[G_LNB:G_LNB + 1, :]
        o_ref[:, j * BRANCH_W:(j + 1) * BRANCH_W] = r.astype(BF)


def _proj(x2, norm_g, w_slab, layer, gains, tm=512):
    n = x2.shape[0]
    return pl.pallas_call(
        _proj_kernel,
        grid=(n // tm,),
        in_specs=[
            pl.BlockSpec((tm, D_MODEL), lambda i: (i, 0)),
            _resident((1, D_MODEL), lambda i: (0, 0)),
            _layer_resident((D_MODEL, SLAB_COLS), layer),
            _resident((8, BRANCH_W), lambda i: (0, 0)),
        ],
        out_specs=pl.BlockSpec((tm, SLAB_COLS), lambda i: (i, 0)),
        out_shape=jax.ShapeDtypeStruct((n, SLAB_COLS), BF),
        compiler_params=_cparams("parallel"),
        name="proj",
    )(x2, norm_g, w_slab, gains)


def _memkv_kernel(mem_ref, g_ref, w_ref, kg_ref, k_ref, v_ref):
    h = _rmsnorm_rows(mem_ref[0], g_ref[...]).astype(BF)
    k = _dot(h, w_ref[:, :BRANCH_W])
    v = _dot(h, w_ref[:, BRANCH_W:])
    k = k * lax.rsqrt(_group_mean_sq(k, MEM_HD) + EPS) * kg_ref[...]
    k_ref[0] = k.astype(BF)
    v_ref[0] = v.astype(BF)


def _memkv(mem, m_norm_g, m_wkv, layer, kg):
    b, m, _ = mem.shape
    return pl.pallas_call(
        _memkv_kernel,
        grid=(b,),
        in_specs=[
            pl.BlockSpec((1, m, D_MODEL), lambda i: (i, 0, 0)),
            _resident((1, D_MODEL), lambda i: (0, 0)),
            _layer_resident((D_MODEL, 2 * BRANCH_W), layer),
            _resident((1, BRANCH_W), lambda i: (0, 0)),
        ],
        out_specs=[pl.BlockSpec((1, m, BRANCH_W), lambda i: (i, 0, 0)),
                   pl.BlockSpec((1, m, BRANCH_W), lambda i: (i, 0, 0))],
        out_shape=[jax.ShapeDtypeStruct((b, m, BRANCH_W), BF)] * 2,
        compiler_params=_cparams("parallel"),
        name="memkv",
    )(mem, m_norm_g, m_wkv, kg)


NA_ROWS_PER_STEP = 8
NA_WIN = NA_KH * GRID_W
NA_LOOKAHEAD = 2


def _na_kernel(q_ref, k_ref, v_ref, z_ref, bias_ref, o_ref):
    n_rows = k_ref.shape[1] // GRID_W
    rb = pl.program_id(1)
    head_of_lane = _lane_group((GRID_W, BRANCH_W), 6)
    masks = [_lane_mask((GRID_W, BRANCH_W), h * NA_HD, (h + 1) * NA_HD, BF) for h in range(NA_HEADS)]

    def window(rr):
        r = rb * NA_ROWS_PER_STEP + rr
        rs = jnp.clip(r - NA_KH // 2, 0, n_rows - NA_KH)
        cls = jnp.where(r < NA_KH // 2, r, jnp.where(r > n_rows - NA_KH // 2, r - (n_rows - NA_KH), NA_KH // 2))
        return pl.multiple_of(rs * GRID_W, GRID_W), cls

    def scores(rr):
        w0, cls = window(rr)
        q = q_ref[0, rr * GRID_W:(rr + 1) * GRID_W, :]
        qs = jnp.concatenate([q * masks[h] for h in range(NA_HEADS)], axis=0)
        return _dot_nt(qs, k_ref[0, pl.ds(w0, NA_WIN), :]) + bias_ref[cls]

    pending = [scores(rr) for rr in range(NA_LOOKAHEAD)]
    for rr in range(NA_ROWS_PER_STEP):
        if rr + NA_LOOKAHEAD < NA_ROWS_PER_STEP:
            pending.append(scores(rr + NA_LOOKAHEAD))
        s = pending.pop(0)
        w0, _ = window(rr)
        m = jnp.max(s, axis=-1, keepdims=True)
        p = jnp.exp(s - m)
        l = jnp.sum(p, axis=-1, keepdims=True)
        pv = _dot(p.astype(BF), v_ref[0, pl.ds(w0, NA_WIN), :]) / l
        o = jnp.zeros((GRID_W, BRANCH_W), F32)
        for h in range(NA_HEADS):
            o = jnp.where(head_of_lane == h, pv[h * GRID_W:(h + 1) * GRID_W], o)
        rows = slice(rr * GRID_W, (rr + 1) * GRID_W)
        o_ref[0, rows, :] = (o * z_ref[0, rows, :].astype(F32)).astype(BF)


def _na(p3, bias):
    b, t, _ = p3.shape
    tq = NA_ROWS_PER_STEP * GRID_W
    return pl.pallas_call(
        _na_kernel,
        grid=(b, t // tq),
        in_specs=[
            pl.BlockSpec((1, tq, BRANCH_W), lambda i, j: (i, j, S_AQ)),
            pl.BlockSpec((1, t, BRANCH_W), lambda i, j: (i, 0, S_AK)),
            pl.BlockSpec((1, t, BRANCH_W), lambda i, j: (i, 0, S_AV)),
            pl.BlockSpec((1, tq, BRANCH_W), lambda i, j: (i, j, S_AZ)),
            _resident(bias.shape, lambda i, j: (0, 0, 0)),
        ],
        out_specs=pl.BlockSpec((1, tq, BRANCH_W), lambda i, j: (i, j, 0)),
        out_shape=jax.ShapeDtypeStruct((b, t, BRANCH_W), BF),
        compiler_params=_cparams("parallel", "arbitrary"),
        name="na",
    )(p3, p3, p3, p3, bias)


NA_REL_ROWS = 2 * NA_KH - 1
NA_REL_COLS = 2 * NA_KW - 1
NA_CLASSES = NA_KH


def _na_class_roff(n_rows):
    half = NA_KH // 2
    rows = np.array(list(range(half)) + [half] + [n_rows - half + 1 + i for i in range(half - 1)])
    rs = np.clip(rows - half, 0, n_rows - NA_KH)
    return [int(v) for v in rs - rows + NA_KH - 1]


def _nabias_kernel(roff0, rpb_ref, o_ref, f_scr):
    h = pl.program_id(0)
    shape = (GRID_W, 2 * GRID_W)
    q = lax.broadcasted_iota(jnp.int32, shape, 0)
    lane = lax.broadcasted_iota(jnp.int32, shape, 1)
    k = lane & (GRID_W - 1)
    coff = k - q + (NA_KW - 1)
    cs = jnp.clip(q - NA_KW // 2, 0, GRID_W - NA_KW)
    valid = (k >= cs) & (k < cs + NA_KW)
    first = lane < GRID_W

    def tile(r, carry):
        base = (h * NA_REL_ROWS + r) * NA_REL_COLS
        acc = jnp.zeros(shape, F32)
        for s in range(NA_REL_COLS):
            acc = jnp.where(coff == s, jnp.where(first, rpb_ref[base + s], rpb_ref[base + NA_REL_COLS + s]), acc)
        f_scr[r] = jnp.where(valid, acc, NEG_BIG)
        return carry

    lax.fori_loop(0, NA_REL_ROWS - 1, tile, 0)
    for c in range(NA_CLASSES):
        for p in range(NA_KH // 2):
            o_ref[c, 0, :, p * 2 * GRID_W:(p + 1) * 2 * GRID_W] = f_scr[roff0[c] + 2 * p]


def _na_bias_table(rpb, n_rows):
    return pl.pallas_call(
        functools.partial(_nabias_kernel, _na_class_roff(n_rows)),
        grid=(NA_HEADS,),
        in_specs=[pl.BlockSpec(memory_space=pltpu.SMEM)],
        out_specs=pl.BlockSpec((NA_CLASSES, 1, GRID_W, NA_WIN), lambda h: (0, h, 0, 0)),
        out_shape=jax.ShapeDtypeStruct((NA_CLASSES, NA_HEADS, GRID_W, NA_WIN), F32),
        scratch_shapes=[pltpu.VMEM((NA_REL_ROWS - 1, GRID_W, 2 * GRID_W), F32)],
        compiler_params=_cparams("arbitrary"),
        name="nabias",
    )(rpb.astype(F32).reshape(-1)).reshape(NA_CLASSES, NA_HEADS * GRID_W, NA_WIN)


DA_VT_ROWS = 80
X_SUM = DA_V


def _dprep_kernel(q_ref, k_ref, v_ref, qt_ref, k4_ref, vt_ref):
    tt = q_ref.shape[1]
    t0 = pl.program_id(1) * tt
    q = q_ref[0]
    k = k_ref[0]
    v = v_ref[0]
    src = lax.broadcasted_iota(jnp.int32, (BRANCH_W, DA_LANES), 0)
    dst = lax.broadcasted_iota(jnp.int32, (BRANCH_W, DA_LANES), 1)
    src_t = lax.broadcasted_iota(jnp.int32, (DA_LANES, BRANCH_W), 1)
    dst_t = lax.broadcasted_iota(jnp.int32, (DA_LANES, BRANCH_W), 0)
    klane = lax.broadcasted_iota(jnp.int32, (tt, DA_LANES), 1)
    kpos = t0 + lax.broadcasted_iota(jnp.int32, (tt, DA_LANES), 0)
    vrow = lax.broadcasted_iota(jnp.int32, (DA_VT_ROWS, tt), 0)
    for h in range(DA_HEADS):
        sel = jnp.where((src == h * DA_V + dst) & (dst < DA_V), 1.0, 0.0).astype(BF)
        sel_t = jnp.where((src_t == h * DA_V + dst_t) & (dst_t < DA_V), 1.0, 0.0).astype(BF)
        qt_ref[0, h] = _dot_nt(sel_t[:DA_V], q).astype(BF)
        vt = _dot_nt(sel_t[:DA_VT_ROWS], v)
        vt_ref[0, h] = jnp.where(vrow == X_SUM, 1.0, vt).astype(BF)
        slope = ALIBI_SLOPES[h]
        k4 = _dot(k, sel)
        k4 = jnp.where((klane >= X_KHI) & (klane < X_KLO), (kpos >> 6).astype(F32) * (slope * 64.0), k4)
        k4 = jnp.where((klane >= X_KLO) & (klane < X_QHI), (kpos & 63).astype(F32) * slope, k4)
        k4 = jnp.where((klane >= X_QHI) & (klane < X_END), _piece_of(klane), k4)
        k4_ref[0, h] = k4.astype(BF)


def _dprep(p3, tt=512):
    b, t, _ = p3.shape
    return pl.pallas_call(
        _dprep_kernel,
        grid=(b, t // tt),
        in_specs=[
            pl.BlockSpec((1, tt, BRANCH_W), lambda i, j: (i, j, S_BQ)),
            pl.BlockSpec((1, tt, BRANCH_W), lambda i, j: (i, j, S_BK)),
            pl.BlockSpec((1, tt, BRANCH_W), lambda i, j: (i, j, S_BV)),
        ],
        out_specs=[
            pl.BlockSpec((1, DA_HEADS, DA_V, tt), lambda i, j: (i, 0, 0, j)),
            pl.BlockSpec((1, DA_HEADS, tt, DA_LANES), lambda i, j: (i, 0, j, 0)),
            pl.BlockSpec((1, DA_HEADS, DA_VT_ROWS, tt), lambda i, j: (i, 0, 0, j)),
        ],
        out_shape=[
            jax.ShapeDtypeStruct((b, DA_HEADS, DA_V, t), BF),
            jax.ShapeDtypeStruct((b, DA_HEADS, t, DA_LANES), BF),
            jax.ShapeDtypeStruct((b, DA_HEADS, DA_VT_ROWS, t), BF),
        ],
        compiler_params=_cparams("parallel", "parallel"),
        name="dprep",
    )(p3, p3, p3)


DA_QCOLS = 256
DA_KROWS = 256
DA_LOOKAHEAD = 6
Q_LEFT, Q_DIAG, Q_RIGHT = range(3)


def _da_kernel(lam_init, qt_ref, k_ref, vt_ref, z_ref, lamv_ref, subg_ref, o_ref,
               qv_scr, dist_scr, spre_scr, m_scr, acc_scr):
    bq = qt_ref.shape[3]
    bk = bq
    nk = k_ref.shape[2] // bk
    qi = pl.program_id(1)
    frow = lax.broadcasted_iota(jnp.int32, (DA_V, bq), 0)
    xrow = DA_V + lax.broadcasted_iota(jnp.int32, (X_ROWS, bq), 0)
    qpos = qi * bq + lax.broadcasted_iota(jnp.int32, (X_ROWS, bq), 1)

    lv = lamv_ref[...]
    lam = (jnp.exp(jnp.sum(lv[0:1] * lv[1:2], axis=-1, keepdims=True))
           - jnp.exp(jnp.sum(lv[2:3] * lv[3:4], axis=-1, keepdims=True)) + lam_init)

    cols = [slice(c * DA_QCOLS, (c + 1) * DA_QCOLS) for c in range(bq // DA_QCOLS)]
    chains = [(h, m, c) for h in range(DA_HEADS) for m in range(2) for c in range(len(cols))]

    def tiles_of(blk):
        return [(blk, kc * DA_KROWS, ch) for kc in range(bk // DA_KROWS) for ch in chains]

    def scores(tile):
        (k0, side, diag), r0, (h, m, c) = tile
        if diag:
            c0 = c * DA_QCOLS
            side = Q_DIAG if r0 == c0 else (Q_LEFT if r0 < c0 else Q_RIGHT)
        s = _dot(k_ref[0, h, pl.ds(k0 + r0, DA_KROWS), :], qv_scr[h, side, m, :, cols[c]])
        if diag and r0 == c0:
            s = s - dist_scr[...] * (ALIBI_SLOPES[h] * LOG2E)
        return s

    def run_block(blk, nxt):
        tiles = tiles_of(blk)
        pending = []
        for i, ((k0, _, _), r0, (h, m, c)) in enumerate(tiles):
            j = i + DA_LOOKAHEAD
            if j < len(tiles):
                pending.append(scores(tiles[j]))
            elif nxt is not None:
                spre_scr[j - len(tiles)] = scores(tiles_of(nxt)[j - len(tiles)])
            s = spre_scr[i] if i < DA_LOOKAHEAD else pending.pop(0)
            m_old = m_scr[h, m, :, cols[c]]
            m_new = jnp.maximum(m_old, jnp.max(s, axis=0, keepdims=True))
            alpha = jnp.exp2(m_old - m_new)
            p = jnp.exp2(s - m_new).astype(BF)
            pv = _dot(vt_ref[0, h, :, pl.ds(k0 + r0, DA_KROWS)], p)
            acc_scr[h, m, :, cols[c]] = alpha * acc_scr[h, m, :, cols[c]] + pv
            m_scr[h, m, :, cols[c]] = m_new

    assert DA_KROWS == DA_QCOLS, "diagonal tiles are taken to be square"
    rel = (lax.broadcasted_iota(jnp.int32, (DA_KROWS, DA_QCOLS), 0)
           - lax.broadcasted_iota(jnp.int32, (DA_KROWS, DA_QCOLS), 1))
    dist_scr[...] = jnp.abs(rel).astype(F32)
    pieces = _piece_of(xrow)
    tail = jnp.zeros((DA_LANES - DA_V - X_ROWS, bq), BF)
    for h in range(DA_HEADS):
        slope = ALIBI_SLOPES[h]
        ih = (qpos >> 6).astype(F32) * (slope * 64.0)
        il = (qpos & 63).astype(F32) * slope
        ex = jnp.where(xrow < X_QHI, pieces,
                       jnp.where(xrow < X_QLO, -ih, jnp.where(xrow < X_END, -il, 0.0))).astype(BF)
        for m in range(2):
            qm = qt_ref[0, h] * jnp.where(frow >> 5 == m, 1.0, 0.0).astype(BF)
            for side, extra in ((Q_LEFT, ex), (Q_DIAG, jnp.zeros_like(ex)), (Q_RIGHT, -ex)):
                qv_scr[h, side, m, :DA_V, :] = qm
                qv_scr[h, side, m, DA_V:DA_V + X_ROWS, :] = extra
                qv_scr[h, side, m, DA_V + X_ROWS:, :] = tail
    m_scr[...] = jnp.full(m_scr.shape, NEG_BIG, F32)
    acc_scr[...] = jnp.zeros(acc_scr.shape, F32)

    def off_diagonal(t):
        kb = jnp.where(t >= qi, t + 1, t)
        return pl.multiple_of(kb * bk, bk), jnp.where(t >= qi, Q_RIGHT, Q_LEFT), False

    diagonal = (pl.multiple_of(qi * bk, bk), Q_DIAG, True)

    for i, tile in enumerate(tiles_of(off_diagonal(0))[:DA_LOOKAHEAD]):
        spre_scr[i] = scores(tile)

    def body(t, carry):
        run_block(off_diagonal(t), off_diagonal(t + 1))
        return carry

    lax.fori_loop(0, nk - 2, body, 0)
    run_block(off_diagonal(nk - 2), diagonal)
    run_block(diagonal, None)

    heads = []
    for h in range(DA_HEADS):
        o = [acc_scr[h, m, :DA_V, :] / acc_scr[h, m, X_SUM:X_SUM + 1, :] for m in range(2)]
        heads.append(o[0] - lam * o[1])

    full = jnp.concatenate(heads, axis=0).T
    y = full * lax.rsqrt(_group_mean_sq(full, DA_V) + EPS) * subg_ref[...]
    o_ref[0] = (y * z_ref[0].astype(F32)).astype(BF)


def _da(qt4, k4, vt4, p3, lamv, subg, lam_init, bq=512):
    b, _, t, _ = k4.shape
    return pl.pallas_call(
        functools.partial(_da_kernel, lam_init),
        grid=(b, t // bq),
        in_specs=[
            pl.BlockSpec((1, DA_HEADS, DA_V, bq), lambda i, j: (i, 0, 0, j)),
            pl.BlockSpec((1, DA_HEADS, t, DA_LANES), lambda i, j: (i, 0, 0, 0)),
            pl.BlockSpec((1, DA_HEADS, DA_VT_ROWS, t), lambda i, j: (i, 0, 0, 0)),
            pl.BlockSpec((1, bq, BRANCH_W), lambda i, j: (i, j, S_BZ)),
            _resident((4, DA_QK), lambda i, j: (0, 0)),
            _resident((1, BRANCH_W), lambda i, j: (0, 0)),
        ],
        out_specs=pl.BlockSpec((1, bq, BRANCH_W), lambda i, j: (i, j, 0)),
        out_shape=jax.ShapeDtypeStruct((b, t, BRANCH_W), BF),
        scratch_shapes=[
            pltpu.VMEM((DA_HEADS, 3, 2, DA_LANES, bq), BF),
            pltpu.VMEM((DA_KROWS, DA_QCOLS), F32),
            pltpu.VMEM((DA_LOOKAHEAD, DA_KROWS, DA_QCOLS), F32),
            pltpu.VMEM((DA_HEADS, 2, 1, bq), F32),
            pltpu.VMEM((DA_HEADS, 2, DA_VT_ROWS, bq), F32),
        ],
        compiler_params=_cparams("parallel", "arbitrary"),
        name="da",
    )(qt4, k4, vt4, p3, lamv, subg)


POOL_HALO = 16


def _pool_kernel(x_ref, z_ref, band_ref, cw_ref, cs_ref, o_ref):
    tt = z_ref.shape[1]
    t_total = x_ref.shape[1]
    nt = t_total // tt
    ti = pl.program_id(1)
    t0 = pl.multiple_of(ti * tt, tt)
    xm = x_ref[0, pl.ds(t0, tt), :]
    p0 = pl.multiple_of(jnp.maximum(t0 - POOL_HALO, 0), POOL_HALO)
    n0 = pl.multiple_of(jnp.minimum(t0 + tt, t_total - POOL_HALO), POOL_HALO)
    has_prev = (ti > 0).astype(F32)
    has_next = (ti < nt - 1).astype(F32)
    prev = (x_ref[0, pl.ds(p0, POOL_HALO), :].astype(F32) * has_prev).astype(BF)
    nxt = (x_ref[0, pl.ds(n0, POOL_HALO), :].astype(F32) * has_next).astype(BF)
    xcat = jnp.concatenate([prev, xm, nxt], axis=0)

    sub = band_ref.shape[1]
    group = _lane_group((sub, BRANCH_W), 6)
    half = jnp.left_shift(1, group)
    sums = [[_dot(band_ref[g], xcat[r0:r0 + sub + 2 * POOL_HALO]) for g in range(len(POOL_WINDOWS))]
            for r0 in range(0, tt, sub)]
    for i, r0 in enumerate(range(0, tt, sub)):
        wsum = jnp.zeros((sub, BRANCH_W), F32)
        for g in range(len(POOL_WINDOWS)):
            wsum = jnp.where(group == g, sums[i][g], wsum)
        tpos = t0 + r0 + lax.broadcasted_iota(jnp.int32, (sub, BRANCH_W), 0)
        lo = jnp.maximum(tpos - half, 0)
        hi = jnp.minimum(tpos + half - 1, t_total - 1)
        cnt = (hi - lo + 1).astype(F32)
        dlt = wsum / cnt - xm[r0:r0 + sub].astype(F32)
        y = _dot(dlt.astype(BF), cw_ref[...]) * cs_ref[...]
        o_ref[0, r0:r0 + sub, :] = (y * z_ref[0, r0:r0 + sub, :].astype(F32)).astype(BF)


def _pool_band(tt):
    band = np.zeros((len(POOL_WINDOWS), tt, tt + 2 * POOL_HALO), np.float32)
    t = np.arange(tt)[:, None]
    u = np.arange(tt + 2 * POOL_HALO)[None, :] - POOL_HALO
    for g, w in enumerate(POOL_WINDOWS):
        band[g] = ((u >= t - w // 2) & (u <= t - w // 2 + w - 1)).astype(np.float32)
    return jnp.asarray(band, BF)


def _pool(p3, cw_bd, c_scale, tt=512, sub=256):
    b, t, _ = p3.shape
    band = _pool_band(sub)
    return pl.pallas_call(
        _pool_kernel,
        grid=(b, t // tt),
        in_specs=[
            pl.BlockSpec((1, t, BRANCH_W), lambda i, j: (i, 0, S_CX)),
            pl.BlockSpec((1, tt, BRANCH_W), lambda i, j: (i, j, S_CZ)),
            _resident(band.shape, lambda i, j: (0, 0, 0)),
            _resident((BRANCH_W, BRANCH_W), lambda i, j: (0, 0)),
            _resident((1, BRANCH_W), lambda i, j: (0, 0)),
        ],
        out_specs=pl.BlockSpec((1, tt, BRANCH_W), lambda i, j: (i, j, 0)),
        out_shape=jax.ShapeDtypeStruct((b, t, BRANCH_W), BF),
        compiler_params=_cparams("parallel", "arbitrary"),
        name="pool",
    )(p3, p3, band, cw_bd, c_scale)


def _sg_kernel(u_ref, v_ref, z_ref, ws_ref, bias_ref, o_ref):
    tt = u_ref.shape[0]
    group = _lane_group((SG_CHUNK, BRANCH_W), 6)
    for c in range(tt // SG_CHUNK):
        rows = slice(c * SG_CHUNK, (c + 1) * SG_CHUNK)
        vc = v_ref[rows, :]
        mix = jnp.zeros((SG_CHUNK, BRANCH_W), F32)
        for g in range(SG_GROUPS):
            mix = jnp.where(group == g, _dot(ws_ref[g], vc), mix)
        y = u_ref[rows, :].astype(F32) * (mix + bias_ref[...]) * z_ref[rows, :].astype(F32)
        o_ref[rows, :] = y.astype(BF)


def _sg(p2, ws, layer, sg_bias, tt=512):
    n = p2.shape[0]
    return pl.pallas_call(
        _sg_kernel,
        grid=(n // tt,),
        in_specs=[
            pl.BlockSpec((tt, BRANCH_W), lambda i: (i, S_DU)),
            pl.BlockSpec((tt, BRANCH_W), lambda i: (i, S_DV)),
            pl.BlockSpec((tt, BRANCH_W), lambda i: (i, S_DZ)),
            _layer_resident((SG_GROUPS, SG_CHUNK, SG_CHUNK), layer),
            _resident((SG_CHUNK, BRANCH_W), lambda i: (0, 0)),
        ],
        out_specs=pl.BlockSpec((tt, BRANCH_W), lambda i: (i, 0)),
        out_shape=jax.ShapeDtypeStruct((n, BRANCH_W), BF),
        compiler_params=_cparams("parallel"),
        name="sg",
    )(p2, p2, p2, ws, sg_bias)


def _mem_kernel(q_ref, z_ref, k_ref, v_ref, o_ref):
    tt = q_ref.shape[1]
    q = q_ref[0]
    k = k_ref[0]
    v = v_ref[0]
    head_of_lane = _lane_group((tt, BRANCH_W), 6)
    scores = [_dot_nt(q * _lane_mask(q.shape, h * MEM_HD, (h + 1) * MEM_HD, BF), k) for h in range(MEM_HEADS)]
    o = jnp.zeros((tt, BRANCH_W), F32)
    for h in range(MEM_HEADS):
        s = scores[h]
        m = jnp.max(s, axis=-1, keepdims=True)
        p = jnp.exp(s - m)
        l = jnp.sum(p, axis=-1, keepdims=True)
        oh = _dot(p.astype(BF), v) / l
        o = jnp.where(head_of_lane == h, oh, o)
    o_ref[0] = (o * z_ref[0].astype(F32)).astype(BF)


def _mem_attn(p3, mk, mv, tt=512):
    b, t, _ = p3.shape
    m = mk.shape[1]
    return pl.pallas_call(
        _mem_kernel,
        grid=(b, t // tt),
        in_specs=[
            pl.BlockSpec((1, tt, BRANCH_W), lambda i, j: (i, j, S_MQ)),
            pl.BlockSpec((1, tt, BRANCH_W), lambda i, j: (i, j, S_MZ)),
            pl.BlockSpec((1, m, BRANCH_W), lambda i, j: (i, 0, 0)),
            pl.BlockSpec((1, m, BRANCH_W), lambda i, j: (i, 0, 0)),
        ],
        out_specs=pl.BlockSpec((1, tt, BRANCH_W), lambda i, j: (i, j, 0)),
        out_shape=jax.ShapeDtypeStruct((b, t, BRANCH_W), BF),
        compiler_params=_cparams("parallel", "arbitrary"),
        name="mem",
    )(p3, p3, mk, mv)


def _final_kernel(x_ref, ng_ref, ya_ref, yb_ref, yc_ref, yd_ref, ym_ref,
                  wg_ref, bg_ref, wb_ref, wo_ref, o_ref):
    x = x_ref[...]
    h = _rmsnorm_rows(x, ng_ref[...]).astype(BF)
    merged = jnp.zeros(x.shape, F32)
    for i, y_ref in enumerate((ya_ref, yb_ref, yc_ref, yd_ref, ym_ref)):
        logit = _dot(h, wg_ref[:, i * D_MODEL:(i + 1) * D_MODEL]) + bg_ref[i:i + 1, :]
        merged = merged + _sigmoid(logit) * _dot(y_ref[...], wb_ref[i])
    o_ref[...] = x + _dot(merged.astype(BF), wo_ref[...])


def _final(x2, norm_g, ys, w_gate, b_gate, w_branch, w_out, layer, tm=512):
    n = x2.shape[0]
    yspec = pl.BlockSpec((tm, BRANCH_W), lambda i: (i, 0))
    return pl.pallas_call(
        _final_kernel,
        grid=(n // tm,),
        in_specs=[
            pl.BlockSpec((tm, D_MODEL), lambda i: (i, 0)),
            _resident((1, D_MODEL), lambda i: (0, 0)),
            yspec, yspec, yspec, yspec, yspec,
            _layer_resident((D_MODEL, N_BRANCH * D_MODEL), layer),
            _resident((8, D_MODEL), lambda i: (0, 0)),
            _layer_resident((N_BRANCH, BRANCH_W, D_MODEL), layer),
            _layer_resident((D_MODEL, D_MODEL), layer),
        ],
        out_specs=pl.BlockSpec((tm, D_MODEL), lambda i: (i, 0)),
        out_shape=jax.ShapeDtypeStruct((n, D_MODEL), F32),
        compiler_params=_cparams("parallel"),
        name="final",
    )(x2, norm_g, *ys, w_gate, b_gate, w_branch, w_out)


def _tile4(g):
    return jnp.tile(g.astype(F32), BRANCH_W // g.shape[0])


def _layer(x, mem, layer_idx, stacked, norm_g, b_gate, a_qn_g, a_kn_g, a_rpb,
           b_qn_g, b_kn_g, b_lam_q1, b_lam_k1, b_lam_q2, b_lam_k2, b_sub_g,
           c_w, c_scale, d_ln_g, d_ln_b, d_bs, m_norm_g, m_qn_g, m_kn_g):
    b, t, d = x.shape
    n = b * t
    x2 = x.reshape(n, d)
    ng = norm_g.reshape(1, d).astype(F32)

    gains = jnp.zeros((8, BRANCH_W), F32)
    gains = gains.at[G_AQ].set(_tile4(a_qn_g) * NA_HD ** -0.5)
    gains = gains.at[G_AK].set(_tile4(a_kn_g))
    gains = gains.at[G_BQ].set(_tile4(b_qn_g) * (DA_QK ** -0.5 * LOG2E))
    gains = gains.at[G_BK].set(_tile4(b_kn_g))
    gains = gains.at[G_MQ].set(_tile4(m_qn_g) * MEM_HD ** -0.5)
    gains = gains.at[G_LNG].set(d_ln_g.astype(F32))
    gains = gains.at[G_LNB].set(d_ln_b.astype(F32))

    p2 = _proj(x2, ng, stacked["w_slab"], layer_idx, gains)
    p3 = p2.reshape(b, t, SLAB_COLS)

    y_a = _na(p3, _na_bias_table(a_rpb, t // GRID_W))

    lam_init = 0.8 - 0.6 * math.exp(-0.3 * layer_idx)
    qt4, k4, vt4 = _dprep(p3)
    lamv = jnp.stack([b_lam_q1, b_lam_k1, b_lam_q2, b_lam_k2]).astype(F32)
    subg = (_tile4(b_sub_g) * (1.0 - lam_init)).reshape(1, BRANCH_W)
    y_b = _da(qt4, k4, vt4, p3, lamv, subg, lam_init)

    cw_bd = jnp.zeros((BRANCH_W, BRANCH_W), F32)
    for g in range(len(POOL_WINDOWS)):
        cw_bd = cw_bd.at[g * POOL_GC:(g + 1) * POOL_GC, g * POOL_GC:(g + 1) * POOL_GC].set(c_w[g].astype(F32))
    y_c = _pool(p3, cw_bd.astype(BF), c_scale.reshape(1, BRANCH_W).astype(F32))

    sg_bias = jnp.repeat(d_bs.astype(F32).T, SG_GC, axis=1)
    y_d = _sg(p2, stacked["d_ws"], layer_idx, sg_bias)

    mk, mv = _memkv(mem, m_norm_g.reshape(1, d).astype(F32), stacked["m_wkv"], layer_idx,
                    _tile4(m_kn_g).reshape(1, BRANCH_W))
    y_m = _mem_attn(p3, mk, mv)

    bg = jnp.zeros((8, d), F32).at[:N_BRANCH].set(b_gate.astype(F32))
    ys = [y.reshape(n, BRANCH_W) for y in (y_a, y_b, y_c, y_d, y_m)]
    out = _final(x2, ng, ys, stacked["w_gate"], bg, stacked["w_branch"], stacked["w_out"], layer_idx)
    return out.reshape(b, t, d)


def kernel(x, mem, norm_g, w_in, b_gate, a_qn_g, a_kn_g, a_rpb, b_qn_g, b_kn_g, b_lam_q1, b_lam_k1, b_lam_q2, b_lam_k2, b_sub_g, c_w, c_scale, d_ln_g, d_ln_b, d_ws, d_bs, m_norm_g, m_wkv, m_qn_g, m_kn_g, w_branch, w_out):
    stacked = {
        "w_slab": w_in[:, :, :SLAB_COLS].astype(BF),
        "w_gate": w_in[:, :, SLAB_COLS:].astype(BF),
        "w_branch": w_branch.astype(BF),
        "w_out": w_out.astype(BF),
        "m_wkv": m_wkv.astype(BF),
        "d_ws": d_ws.astype(BF),
    }
    for l in range(DEPTH):
        x = _layer(x, mem, l, stacked, norm_g[l], b_gate[l], a_qn_g[l], a_kn_g[l], a_rpb[l],
                   b_qn_g[l], b_kn_g[l], b_lam_q1[l], b_lam_k1[l], b_lam_q2[l], b_lam_k2[l], b_sub_g[l],
                   c_w[l], c_scale[l], d_ln_g[l], d_ln_b[l], d_bs[l], m_norm_g[l], m_qn_g[l], m_kn_g[l])
    return x
```

```python
import functools
import math

import numpy as np
import jax
import jax.numpy as jnp
from jax import lax
from jax.experimental import pallas as pl
from jax.experimental.pallas import tpu as pltpu

F32 = jnp.float32
BF = jnp.bfloat16

D_MODEL = 1024
DEPTH = 2
GRID_W = 64
BRANCH_W = 256
N_BRANCH = 5
NA_HEADS = 4
NA_HD = 64
NA_KH = 8
NA_KW = 16
DA_HEADS = 4
DA_QK = 32
DA_V = 64
ALIBI_BASE = 8.0
POOL_WINDOWS = (2, 4, 8, 16)
POOL_GC = 64
SG_CHUNK = 128
SG_GROUPS = 4
SG_GC = 64
MEM_HEADS = 4
MEM_HD = 64
N_SLABS = 15
SLAB_COLS = N_SLABS * BRANCH_W
EPS = 1e-6

(S_AQ, S_AK, S_AV, S_AZ, S_BQ, S_BK, S_BV, S_BZ, S_CX, S_CZ,
 S_DU, S_DV, S_DZ, S_MQ, S_MZ) = range(N_SLABS)

(G_AQ, G_AK, G_BQ, G_BK, G_MQ, G_LNG, G_LNB) = range(7)

VMEM_LIMIT_BYTES = 56 * 1024 * 1024
NEG_BIG = -1e30

ALIBI_SLOPES = tuple(2.0 ** (-ALIBI_BASE * (h + 1) / DA_HEADS) for h in range(DA_HEADS))
DA_LANES = 128


def _bf16_pieces(x, n):
    out = []
    for _ in range(n):
        piece = float(np.asarray(x, np.float32).astype(jnp.bfloat16).astype(np.float32))
        out.append(piece)
        x -= piece
    return tuple(out)


LOG2E = math.log2(math.e)
LOG2E_PIECES = _bf16_pieces(LOG2E, 3)
N_PIECES = len(LOG2E_PIECES)
X_KHI, X_KLO, X_QHI, X_QLO = (DA_V + g * N_PIECES for g in range(4))
X_END = DA_V + 4 * N_PIECES
X_ROWS = 16


def _piece_of(idx):
    out = jnp.zeros(idx.shape, F32)
    for g in range(4):
        for p, piece in enumerate(LOG2E_PIECES):
            out = jnp.where(idx == DA_V + g * N_PIECES + p, piece, out)
    return out


def _cparams(*sem):
    return pltpu.CompilerParams(dimension_semantics=sem, vmem_limit_bytes=VMEM_LIMIT_BYTES)


def _resident(shape, index_map):
    return pl.BlockSpec(shape, index_map, pipeline_mode=pl.Buffered(1))


def _layer_resident(shape, layer):
    return pl.BlockSpec((None,) + tuple(shape), lambda *_: (layer,) + (0,) * len(shape),
                        pipeline_mode=pl.Buffered(1))


def _dot(a, b):
    return jnp.dot(a, b, preferred_element_type=F32)


def _dot_nt(a, b):
    return lax.dot_general(a, b, (((1,), (1,)), ((), ())), preferred_element_type=F32)


def _lane_group(shape, log2_group):
    return lax.broadcasted_iota(jnp.int32, shape, len(shape) - 1) >> log2_group


def _lane_mask(shape, lo, hi, dtype):
    lane = lax.broadcasted_iota(jnp.int32, shape, len(shape) - 1)
    return jnp.where((lane >= lo) & (lane < hi), 1.0, 0.0).astype(dtype)


def _group_mean_sq(xf, group):
    n = xf.shape[-1]
    sh = int(math.log2(group))
    r = lax.broadcasted_iota(jnp.int32, (n, n), 0) >> sh
    c = lax.broadcasted_iota(jnp.int32, (n, n), 1) >> sh
    pm = jnp.where(r == c, 1.0 / group, 0.0).astype(BF)
    xx = xf * xf
    hi = xx.astype(BF)
    lo = (xx - hi.astype(F32)).astype(BF)
    return _dot(hi, pm) + _dot(lo, pm)


def _sigmoid(x):
    return 1.0 / (1.0 + jnp.exp(-x))


def _silu(x):
    return x * _sigmoid(x)


def _gelu_tanh(x):
    return 0.5 * x * (1.0 + jnp.tanh(math.sqrt(2.0 / math.pi) * (x + 0.044715 * (x * x * x))))


def _rmsnorm_rows(x, g):
    ms = jnp.mean(x * x, axis=-1, keepdims=True)
    return x * lax.rsqrt(ms + EPS) * g


PROJ_LOOKAHEAD = 3
P_SLABS = (S_AQ, S_AK, S_AV, S_AZ, S_BZ, S_CX, S_CZ)
P_POS = {s: i for i, s in enumerate(P_SLABS)}
P_COLS = len(P_SLABS) * BRANCH_W
PROJ_ORDER = (S_MQ, S_MZ, S_DU, S_DV, S_DZ, S_BQ, S_BK, S_BV) + P_SLABS


def _proj_kernel(tiles_per_seq, x_ref, ng_ref, w_ref, gains_ref, ws_ref, sgb_ref, mk_ref, mv_ref,
                 p_ref, yd_ref, ym_ref, qt_ref, k4_ref, vt_ref):
    tm = x_ref.shape[0]
    t0 = (pl.program_id(0) % tiles_per_seq) * tm
    h = _rmsnorm_rows(x_ref[...], ng_ref[...]).astype(BF)

    def gnorm(r, group, row):
        return r * lax.rsqrt(_group_mean_sq(r, group) + EPS) * gains_ref[row:row + 1, :]

    def slab(j):
        return _dot(h, w_ref[:, j * BRANCH_W:(j + 1) * BRANCH_W])

    kept = {}
    pending = [slab(j) for j in PROJ_ORDER[:PROJ_LOOKAHEAD]]
    for idx, j in enumerate(PROJ_ORDER):
        if idx + PROJ_LOOKAHEAD < N_SLABS:
            pending.append(slab(PROJ_ORDER[idx + PROJ_LOOKAHEAD]))
        r = pending.pop(0)
        if j == S_AQ:
            r = gnorm(r, NA_HD, G_AQ)
        elif j == S_AK:
            r = gnorm(r, NA_HD, G_AK)
        elif j == S_BQ:
            r = gnorm(r, DA_QK, G_BQ)
        elif j == S_BK:
            r = gnorm(r, DA_QK, G_BK)
        elif j == S_MQ:
            r = gnorm(r, MEM_HD, G_MQ)
        elif j in (S_AZ, S_BZ, S_CZ, S_DZ, S_MZ):
            r = _silu(r)
        elif j == S_DU:
            r = _gelu_tanh(r)
        elif j == S_DV:
            v = _gelu_tanh(r)
            mu = jnp.mean(v, axis=-1, keepdims=True)
            vc = v - mu
            var = jnp.mean(vc * vc, axis=-1, keepdims=True)
            r = vc * lax.rsqrt(var + EPS) * gains_ref[G_LNG:G_LNG + 1, :] + gains_ref[G_LNB:G_LNB + 1, :]
        if j in P_POS:
            p_ref[:, P_POS[j] * BRANCH_W:(P_POS[j] + 1) * BRANCH_W] = r.astype(BF)
        elif j == S_MZ:
            ym_ref[...] = _mem_attend(kept.pop(S_MQ), r, mk_ref[0], mv_ref[0]).astype(BF)
        elif j == S_DZ:
            _spatial_gate(kept.pop(S_DU), kept.pop(S_DV), r, ws_ref, sgb_ref, yd_ref)
        elif j == S_BV:
            _da_layouts(kept.pop(S_BQ), kept.pop(S_BK), r.astype(BF), t0, qt_ref, k4_ref, vt_ref)
        elif j in (S_MQ, S_DV, S_BQ, S_BK):
            kept[j] = r.astype(BF)
        else:
            kept[j] = r


def _proj(x2, norm_g, w_slab, layer, gains, ws, sg_bias, mk, mv, seq_len, tm=512):
    n = x2.shape[0]
    b = n // seq_len
    tps = seq_len // tm
    m = mk.shape[1]
    return pl.pallas_call(
        functools.partial(_proj_kernel, tps),
        grid=(n // tm,),
        in_specs=[
            pl.BlockSpec((tm, D_MODEL), lambda i: (i, 0)),
            _resident((1, D_MODEL), lambda i: (0, 0)),
            _layer_resident((D_MODEL, SLAB_COLS), layer),
            _resident((8, BRANCH_W), lambda i: (0, 0)),
            _layer_resident((SG_GROUPS, SG_CHUNK, SG_CHUNK), layer),
            _resident((SG_CHUNK, BRANCH_W), lambda i: (0, 0)),
            pl.BlockSpec((1, m, BRANCH_W), lambda i: (i // tps, 0, 0)),
            pl.BlockSpec((1, m, BRANCH_W), lambda i: (i // tps, 0, 0)),
        ],
        out_specs=[
            pl.BlockSpec((tm, P_COLS), lambda i: (i, 0)),
            pl.BlockSpec((tm, BRANCH_W), lambda i: (i, 0)),
            pl.BlockSpec((tm, BRANCH_W), lambda i: (i, 0)),
            pl.BlockSpec((1, DA_HEADS, DA_V, tm), lambda i: (i // tps, 0, 0, i % tps)),
            pl.BlockSpec((1, DA_HEADS, tm, DA_LANES), lambda i: (i // tps, 0, i % tps, 0)),
            pl.BlockSpec((1, DA_HEADS, DA_VT_ROWS, tm), lambda i: (i // tps, 0, 0, i % tps)),
        ],
        out_shape=[
            jax.ShapeDtypeStruct((n, P_COLS), BF),
            jax.ShapeDtypeStruct((n, BRANCH_W), BF),
            jax.ShapeDtypeStruct((n, BRANCH_W), BF),
            jax.ShapeDtypeStruct((b, DA_HEADS, DA_V, seq_len), BF),
            jax.ShapeDtypeStruct((b, DA_HEADS, seq_len, DA_LANES), BF),
            jax.ShapeDtypeStruct((b, DA_HEADS, DA_VT_ROWS, seq_len), BF),
        ],
        compiler_params=_cparams("parallel"),
        name="proj",
    )(x2, norm_g, w_slab, gains, ws, sg_bias, mk, mv)


def _memkv_kernel(mem_ref, g_ref, w_ref, kg_ref, k_ref, v_ref):
    h = _rmsnorm_rows(mem_ref[0], g_ref[...]).astype(BF)
    k = _dot(h, w_ref[:, :BRANCH_W])
    v = _dot(h, w_ref[:, BRANCH_W:])
    k = k * lax.rsqrt(_group_mean_sq(k, MEM_HD) + EPS) * kg_ref[...]
    k_ref[0] = k.astype(BF)
    v_ref[0] = v.astype(BF)


def _memkv(mem, m_norm_g, m_wkv, layer, kg):
    b, m, _ = mem.shape
    return pl.pallas_call(
        _memkv_kernel,
        grid=(b,),
        in_specs=[
            pl.BlockSpec((1, m, D_MODEL), lambda i: (i, 0, 0)),
            _resident((1, D_MODEL), lambda i: (0, 0)),
            _layer_resident((D_MODEL, 2 * BRANCH_W), layer),
            _resident((1, BRANCH_W), lambda i: (0, 0)),
        ],
        out_specs=[pl.BlockSpec((1, m, BRANCH_W), lambda i: (i, 0, 0)),
                   pl.BlockSpec((1, m, BRANCH_W), lambda i: (i, 0, 0))],
        out_shape=[jax.ShapeDtypeStruct((b, m, BRANCH_W), BF)] * 2,
        compiler_params=_cparams("parallel"),
        name="memkv",
    )(mem, m_norm_g, m_wkv, kg)


NA_ROWS_PER_STEP = 8
NA_WIN = NA_KH * GRID_W
NA_LOOKAHEAD = 2


def _na_kernel(q_ref, k_ref, v_ref, z_ref, bias_ref, o_ref):
    n_rows = k_ref.shape[1] // GRID_W
    rb = pl.program_id(1)
    head_of_lane = _lane_group((GRID_W, BRANCH_W), 6)
    masks = [_lane_mask((GRID_W, BRANCH_W), h * NA_HD, (h + 1) * NA_HD, BF) for h in range(NA_HEADS)]

    def window(rr):
        r = rb * NA_ROWS_PER_STEP + rr
        rs = jnp.clip(r - NA_KH // 2, 0, n_rows - NA_KH)
        cls = jnp.where(r < NA_KH // 2, r, jnp.where(r > n_rows - NA_KH // 2, r - (n_rows - NA_KH), NA_KH // 2))
        return pl.multiple_of(rs * GRID_W, GRID_W), cls

    def scores(rr):
        w0, cls = window(rr)
        q = q_ref[0, rr * GRID_W:(rr + 1) * GRID_W, :]
        qs = jnp.concatenate([q * masks[h] for h in range(NA_HEADS)], axis=0)
        return _dot_nt(qs, k_ref[0, pl.ds(w0, NA_WIN), :]) + bias_ref[cls]

    pending = [scores(rr) for rr in range(NA_LOOKAHEAD)]
    for rr in range(NA_ROWS_PER_STEP):
        if rr + NA_LOOKAHEAD < NA_ROWS_PER_STEP:
            pending.append(scores(rr + NA_LOOKAHEAD))
        s = pending.pop(0)
        w0, _ = window(rr)
        m = jnp.max(s, axis=-1, keepdims=True)
        p = jnp.exp(s - m)
        l = jnp.sum(p, axis=-1, keepdims=True)
        pv = _dot(p.astype(BF), v_ref[0, pl.ds(w0, NA_WIN), :]) / l
        o = jnp.zeros((GRID_W, BRANCH_W), F32)
        for h in range(NA_HEADS):
            o = jnp.where(head_of_lane == h, pv[h * GRID_W:(h + 1) * GRID_W], o)
        rows = slice(rr * GRID_W, (rr + 1) * GRID_W)
        o_ref[0, rows, :] = (o * z_ref[0, rows, :].astype(F32)).astype(BF)


def _na(p3, bias):
    b, t, _ = p3.shape
    tq = NA_ROWS_PER_STEP * GRID_W
    return pl.pallas_call(
        _na_kernel,
        grid=(b, t // tq),
        in_specs=[
            pl.BlockSpec((1, tq, BRANCH_W), lambda i, j: (i, j, P_POS[S_AQ])),
            pl.BlockSpec((1, t, BRANCH_W), lambda i, j: (i, 0, P_POS[S_AK])),
            pl.BlockSpec((1, t, BRANCH_W), lambda i, j: (i, 0, P_POS[S_AV])),
            pl.BlockSpec((1, tq, BRANCH_W), lambda i, j: (i, j, P_POS[S_AZ])),
            _resident(bias.shape, lambda i, j: (0, 0, 0)),
        ],
        out_specs=pl.BlockSpec((1, tq, BRANCH_W), lambda i, j: (i, j, 0)),
        out_shape=jax.ShapeDtypeStruct((b, t, BRANCH_W), BF),
        compiler_params=_cparams("parallel", "arbitrary"),
        name="na",
    )(p3, p3, p3, p3, bias)


NA_REL_ROWS = 2 * NA_KH - 1
NA_REL_COLS = 2 * NA_KW - 1
NA_CLASSES = NA_KH


def _na_class_roff(n_rows):
    half = NA_KH // 2
    rows = np.array(list(range(half)) + [half] + [n_rows - half + 1 + i for i in range(half - 1)])
    rs = np.clip(rows - half, 0, n_rows - NA_KH)
    return [int(v) for v in rs - rows + NA_KH - 1]


def _nabias_kernel(roff0, rpb_ref, o_ref, f_scr):
    h = pl.program_id(0)
    shape = (GRID_W, 2 * GRID_W)
    q = lax.broadcasted_iota(jnp.int32, shape, 0)
    lane = lax.broadcasted_iota(jnp.int32, shape, 1)
    k = lane & (GRID_W - 1)
    coff = k - q + (NA_KW - 1)
    cs = jnp.clip(q - NA_KW // 2, 0, GRID_W - NA_KW)
    valid = (k >= cs) & (k < cs + NA_KW)
    first = lane < GRID_W

    def tile(r, carry):
        base = (h * NA_REL_ROWS + r) * NA_REL_COLS
        acc = jnp.zeros(shape, F32)
        for s in range(NA_REL_COLS):
            acc = jnp.where(coff == s, jnp.where(first, rpb_ref[base + s], rpb_ref[base + NA_REL_COLS + s]), acc)
        f_scr[r] = jnp.where(valid, acc, NEG_BIG)
        return carry

    lax.fori_loop(0, NA_REL_ROWS - 1, tile, 0)
    for c in range(NA_CLASSES):
        for p in range(NA_KH // 2):
            o_ref[c, 0, :, p * 2 * GRID_W:(p + 1) * 2 * GRID_W] = f_scr[roff0[c] + 2 * p]


def _na_bias_table(rpb, n_rows):
    return pl.pallas_call(
        functools.partial(_nabias_kernel, _na_class_roff(n_rows)),
        grid=(NA_HEADS,),
        in_specs=[pl.BlockSpec(memory_space=pltpu.SMEM)],
        out_specs=pl.BlockSpec((NA_CLASSES, 1, GRID_W, NA_WIN), lambda h: (0, h, 0, 0)),
        out_shape=jax.ShapeDtypeStruct((NA_CLASSES, NA_HEADS, GRID_W, NA_WIN), F32),
        scratch_shapes=[pltpu.VMEM((NA_REL_ROWS - 1, GRID_W, 2 * GRID_W), F32)],
        compiler_params=_cparams("arbitrary"),
        name="nabias",
    )(rpb.astype(F32).reshape(-1)).reshape(NA_CLASSES, NA_HEADS * GRID_W, NA_WIN)


DA_VT_ROWS = 80
X_SUM = DA_V


def _da_layouts(q, k, v, t0, qt_ref, k4_ref, vt_ref):
    tt = q.shape[0]
    src = lax.broadcasted_iota(jnp.int32, (BRANCH_W, DA_LANES), 0)
    dst = lax.broadcasted_iota(jnp.int32, (BRANCH_W, DA_LANES), 1)
    src_t = lax.broadcasted_iota(jnp.int32, (DA_LANES, BRANCH_W), 1)
    dst_t = lax.broadcasted_iota(jnp.int32, (DA_LANES, BRANCH_W), 0)
    klane = lax.broadcasted_iota(jnp.int32, (tt, DA_LANES), 1)
    kpos = t0 + lax.broadcasted_iota(jnp.int32, (tt, DA_LANES), 0)
    vrow = lax.broadcasted_iota(jnp.int32, (DA_VT_ROWS, tt), 0)
    for h in range(DA_HEADS):
        sel = jnp.where((src == h * DA_V + dst) & (dst < DA_V), 1.0, 0.0).astype(BF)
        sel_t = jnp.where((src_t == h * DA_V + dst_t) & (dst_t < DA_V), 1.0, 0.0).astype(BF)
        qt_ref[0, h] = _dot_nt(sel_t[:DA_V], q).astype(BF)
        vt = _dot_nt(sel_t[:DA_VT_ROWS], v)
        vt_ref[0, h] = jnp.where(vrow == X_SUM, 1.0, vt).astype(BF)
        slope = ALIBI_SLOPES[h]
        k4 = _dot(k, sel)
        k4 = jnp.where((klane >= X_KHI) & (klane < X_KLO), (kpos >> 6).astype(F32) * (slope * 64.0), k4)
        k4 = jnp.where((klane >= X_KLO) & (klane < X_QHI), (kpos & 63).astype(F32) * slope, k4)
        k4 = jnp.where((klane >= X_QHI) & (klane < X_END), _piece_of(klane), k4)
        k4_ref[0, h] = k4.astype(BF)


DA_QCOLS = 256
DA_KROWS = 256
DA_LOOKAHEAD = 6
Q_LEFT, Q_DIAG, Q_RIGHT = range(3)


def _da_kernel(lam_init, qt_ref, k_ref, vt_ref, z_ref, lamv_ref, subg_ref, o_ref,
               qv_scr, dist_scr, spre_scr, m_scr, acc_scr):
    bq = qt_ref.shape[3]
    bk = bq
    nk = k_ref.shape[2] // bk
    qi = pl.program_id(1)
    frow = lax.broadcasted_iota(jnp.int32, (DA_V, bq), 0)
    xrow = DA_V + lax.broadcasted_iota(jnp.int32, (X_ROWS, bq), 0)
    qpos = qi * bq + lax.broadcasted_iota(jnp.int32, (X_ROWS, bq), 1)

    lv = lamv_ref[...]
    lam = (jnp.exp(jnp.sum(lv[0:1] * lv[1:2], axis=-1, keepdims=True))
           - jnp.exp(jnp.sum(lv[2:3] * lv[3:4], axis=-1, keepdims=True)) + lam_init)

    cols = [slice(c * DA_QCOLS, (c + 1) * DA_QCOLS) for c in range(bq // DA_QCOLS)]
    chains = [(h, m, c) for h in range(DA_HEADS) for m in range(2) for c in range(len(cols))]

    def tiles_of(blk):
        return [(blk, kc * DA_KROWS, ch) for kc in range(bk // DA_KROWS) for ch in chains]

    def scores(tile):
        (k0, side, diag), r0, (h, m, c) = tile
        if diag:
            c0 = c * DA_QCOLS
            side = Q_DIAG if r0 == c0 else (Q_LEFT if r0 < c0 else Q_RIGHT)
        s = _dot(k_ref[0, h, pl.ds(k0 + r0, DA_KROWS), :], qv_scr[h, side, m, :, cols[c]])
        if diag and r0 == c0:
            s = s - dist_scr[...] * (ALIBI_SLOPES[h] * LOG2E)
        return s

    def run_block(blk, nxt):
        tiles = tiles_of(blk)
        pending = []
        for i, ((k0, _, _), r0, (h, m, c)) in enumerate(tiles):
            j = i + DA_LOOKAHEAD
            if j < len(tiles):
                pending.append(scores(tiles[j]))
            elif nxt is not None:
                spre_scr[j - len(tiles)] = scores(tiles_of(nxt)[j - len(tiles)])
            s = spre_scr[i] if i < DA_LOOKAHEAD else pending.pop(0)
            m_old = m_scr[h, m, :, cols[c]]
            m_new = jnp.maximum(m_old, jnp.max(s, axis=0, keepdims=True))
            alpha = jnp.exp2(m_old - m_new)
            p = jnp.exp2(s - m_new).astype(BF)
            pv = _dot(vt_ref[0, h, :, pl.ds(k0 + r0, DA_KROWS)], p)
            acc_scr[h, m, :, cols[c]] = alpha * acc_scr[h, m, :, cols[c]] + pv
            m_scr[h, m, :, cols[c]] = m_new

    assert DA_KROWS == DA_QCOLS, "diagonal tiles are taken to be square"
    rel = (lax.broadcasted_iota(jnp.int32, (DA_KROWS, DA_QCOLS), 0)
           - lax.broadcasted_iota(jnp.int32, (DA_KROWS, DA_QCOLS), 1))
    dist_scr[...] = jnp.abs(rel).astype(F32)
    pieces = _piece_of(xrow)
    tail = jnp.zeros((DA_LANES - DA_V - X_ROWS, bq), BF)
    for h in range(DA_HEADS):
        slope = ALIBI_SLOPES[h]
        ih = (qpos >> 6).astype(F32) * (slope * 64.0)
        il = (qpos & 63).astype(F32) * slope
        ex = jnp.where(xrow < X_QHI, pieces,
                       jnp.where(xrow < X_QLO, -ih, jnp.where(xrow < X_END, -il, 0.0))).astype(BF)
        for m in range(2):
            qm = qt_ref[0, h] * jnp.where(frow >> 5 == m, 1.0, 0.0).astype(BF)
            for side, extra in ((Q_LEFT, ex), (Q_DIAG, jnp.zeros_like(ex)), (Q_RIGHT, -ex)):
                qv_scr[h, side, m, :DA_V, :] = qm
                qv_scr[h, side, m, DA_V:DA_V + X_ROWS, :] = extra
                qv_scr[h, side, m, DA_V + X_ROWS:, :] = tail
    m_scr[...] = jnp.full(m_scr.shape, NEG_BIG, F32)
    acc_scr[...] = jnp.zeros(acc_scr.shape, F32)

    def off_diagonal(t):
        kb = jnp.where(t >= qi, t + 1, t)
        return pl.multiple_of(kb * bk, bk), jnp.where(t >= qi, Q_RIGHT, Q_LEFT), False

    diagonal = (pl.multiple_of(qi * bk, bk), Q_DIAG, True)

    for i, tile in enumerate(tiles_of(off_diagonal(0))[:DA_LOOKAHEAD]):
        spre_scr[i] = scores(tile)

    def body(t, carry):
        run_block(off_diagonal(t), off_diagonal(t + 1))
        return carry

    lax.fori_loop(0, nk - 2, body, 0)
    run_block(off_diagonal(nk - 2), diagonal)
    run_block(diagonal, None)

    heads = []
    for h in range(DA_HEADS):
        o = [acc_scr[h, m, :DA_V, :] / acc_scr[h, m, X_SUM:X_SUM + 1, :] for m in range(2)]
        heads.append(o[0] - lam * o[1])

    full = jnp.concatenate(heads, axis=0).T
    y = full * lax.rsqrt(_group_mean_sq(full, DA_V) + EPS) * subg_ref[...]
    o_ref[0] = (y * z_ref[0].astype(F32)).astype(BF)


def _da(qt4, k4, vt4, p3, lamv, subg, lam_init, bq=512):
    b, _, t, _ = k4.shape
    return pl.pallas_call(
        functools.partial(_da_kernel, lam_init),
        grid=(b, t // bq),
        in_specs=[
            pl.BlockSpec((1, DA_HEADS, DA_V, bq), lambda i, j: (i, 0, 0, j)),
            pl.BlockSpec((1, DA_HEADS, t, DA_LANES), lambda i, j: (i, 0, 0, 0)),
            pl.BlockSpec((1, DA_HEADS, DA_VT_ROWS, t), lambda i, j: (i, 0, 0, 0)),
            pl.BlockSpec((1, bq, BRANCH_W), lambda i, j: (i, j, P_POS[S_BZ])),
            _resident((4, DA_QK), lambda i, j: (0, 0)),
            _resident((1, BRANCH_W), lambda i, j: (0, 0)),
        ],
        out_specs=pl.BlockSpec((1, bq, BRANCH_W), lambda i, j: (i, j, 0)),
        out_shape=jax.ShapeDtypeStruct((b, t, BRANCH_W), BF),
        scratch_shapes=[
            pltpu.VMEM((DA_HEADS, 3, 2, DA_LANES, bq), BF),
            pltpu.VMEM((DA_KROWS, DA_QCOLS), F32),
            pltpu.VMEM((DA_LOOKAHEAD, DA_KROWS, DA_QCOLS), F32),
            pltpu.VMEM((DA_HEADS, 2, 1, bq), F32),
            pltpu.VMEM((DA_HEADS, 2, DA_VT_ROWS, bq), F32),
        ],
        compiler_params=_cparams("parallel", "arbitrary"),
        name="da",
    )(qt4, k4, vt4, p3, lamv, subg)


POOL_HALO = 16


def _pool_kernel(x_ref, z_ref, band_ref, cw_ref, cs_ref, o_ref):
    tt = z_ref.shape[1]
    t_total = x_ref.shape[1]
    nt = t_total // tt
    ti = pl.program_id(1)
    t0 = pl.multiple_of(ti * tt, tt)
    xm = x_ref[0, pl.ds(t0, tt), :]
    p0 = pl.multiple_of(jnp.maximum(t0 - POOL_HALO, 0), POOL_HALO)
    n0 = pl.multiple_of(jnp.minimum(t0 + tt, t_total - POOL_HALO), POOL_HALO)
    has_prev = (ti > 0).astype(F32)
    has_next = (ti < nt - 1).astype(F32)
    prev = (x_ref[0, pl.ds(p0, POOL_HALO), :].astype(F32) * has_prev).astype(BF)
    nxt = (x_ref[0, pl.ds(n0, POOL_HALO), :].astype(F32) * has_next).astype(BF)
    xcat = jnp.concatenate([prev, xm, nxt], axis=0)

    sub = band_ref.shape[1]
    group = _lane_group((sub, BRANCH_W), 6)
    half = jnp.left_shift(1, group)
    sums = [[_dot(band_ref[g], xcat[r0:r0 + sub + 2 * POOL_HALO]) for g in range(len(POOL_WINDOWS))]
            for r0 in range(0, tt, sub)]
    for i, r0 in enumerate(range(0, tt, sub)):
        wsum = jnp.zeros((sub, BRANCH_W), F32)
        for g in range(len(POOL_WINDOWS)):
            wsum = jnp.where(group == g, sums[i][g], wsum)
        tpos = t0 + r0 + lax.broadcasted_iota(jnp.int32, (sub, BRANCH_W), 0)
        lo = jnp.maximum(tpos - half, 0)
        hi = jnp.minimum(tpos + half - 1, t_total - 1)
        cnt = (hi - lo + 1).astype(F32)
        dlt = wsum / cnt - xm[r0:r0 + sub].astype(F32)
        y = _dot(dlt.astype(BF), cw_ref[...]) * cs_ref[...]
        o_ref[0, r0:r0 + sub, :] = (y * z_ref[0, r0:r0 + sub, :].astype(F32)).astype(BF)


def _pool_band(tt):
    band = np.zeros((len(POOL_WINDOWS), tt, tt + 2 * POOL_HALO), np.float32)
    t = np.arange(tt)[:, None]
    u = np.arange(tt + 2 * POOL_HALO)[None, :] - POOL_HALO
    for g, w in enumerate(POOL_WINDOWS):
        band[g] = ((u >= t - w // 2) & (u <= t - w // 2 + w - 1)).astype(np.float32)
    return jnp.asarray(band, BF)


def _pool(p3, cw_bd, c_scale, tt=512, sub=256):
    b, t, _ = p3.shape
    band = _pool_band(sub)
    return pl.pallas_call(
        _pool_kernel,
        grid=(b, t // tt),
        in_specs=[
            pl.BlockSpec((1, t, BRANCH_W), lambda i, j: (i, 0, P_POS[S_CX])),
            pl.BlockSpec((1, tt, BRANCH_W), lambda i, j: (i, j, P_POS[S_CZ])),
            _resident(band.shape, lambda i, j: (0, 0, 0)),
            _resident((BRANCH_W, BRANCH_W), lambda i, j: (0, 0)),
            _resident((1, BRANCH_W), lambda i, j: (0, 0)),
        ],
        out_specs=pl.BlockSpec((1, tt, BRANCH_W), lambda i, j: (i, j, 0)),
        out_shape=jax.ShapeDtypeStruct((b, t, BRANCH_W), BF),
        compiler_params=_cparams("parallel", "arbitrary"),
        name="pool",
    )(p3, p3, band, cw_bd, c_scale)


def _spatial_gate(u, vn, z, ws_ref, bias_ref, o_ref):
    group = _lane_group((SG_CHUNK, BRANCH_W), 6)
    for c in range(u.shape[0] // SG_CHUNK):
        rows = slice(c * SG_CHUNK, (c + 1) * SG_CHUNK)
        mix = jnp.zeros((SG_CHUNK, BRANCH_W), F32)
        for g in range(SG_GROUPS):
            mix = jnp.where(group == g, _dot(ws_ref[g], vn[rows]), mix)
        o_ref[rows, :] = (u[rows] * (mix + bias_ref[...]) * z[rows]).astype(BF)


def _mem_attend(q, z, k, v):
    head_of_lane = _lane_group(q.shape, 6)
    scores = [_dot_nt(q * _lane_mask(q.shape, h * MEM_HD, (h + 1) * MEM_HD, BF), k) for h in range(MEM_HEADS)]
    o = jnp.zeros(q.shape, F32)
    for h in range(MEM_HEADS):
        s = scores[h]
        m = jnp.max(s, axis=-1, keepdims=True)
        p = jnp.exp(s - m)
        l = jnp.sum(p, axis=-1, keepdims=True)
        oh = _dot(p.astype(BF), v) / l
        o = jnp.where(head_of_lane == h, oh, o)
    return o * z


def _final_kernel(x_ref, ng_ref, ya_ref, yb_ref, yc_ref, yd_ref, ym_ref,
                  wg_ref, bg_ref, wb_ref, wo_ref, o_ref):
    x = x_ref[...]
    h = _rmsnorm_rows(x, ng_ref[...]).astype(BF)
    merged = jnp.zeros(x.shape, F32)
    for i, y_ref in enumerate((ya_ref, yb_ref, yc_ref, yd_ref, ym_ref)):
        logit = _dot(h, wg_ref[:, i * D_MODEL:(i + 1) * D_MODEL]) + bg_ref[i:i + 1, :]
        merged = merged + _sigmoid(logit) * _dot(y_ref[...], wb_ref[i])
    o_ref[...] = x + _dot(merged.astype(BF), wo_ref[...])


def _final(x2, norm_g, ys, w_gate, b_gate, w_branch, w_out, layer, tm=512):
    n = x2.shape[0]
    yspec = pl.BlockSpec((tm, BRANCH_W), lambda i: (i, 0))
    return pl.pallas_call(
        _final_kernel,
        grid=(n // tm,),
        in_specs=[
            pl.BlockSpec((tm, D_MODEL), lambda i: (i, 0)),
            _resident((1, D_MODEL), lambda i: (0, 0)),
            yspec, yspec, yspec, yspec, yspec,
            _layer_resident((D_MODEL, N_BRANCH * D_MODEL), layer),
            _resident((8, D_MODEL), lambda i: (0, 0)),
            _layer_resident((N_BRANCH, BRANCH_W, D_MODEL), layer),
            _layer_resident((D_MODEL, D_MODEL), layer),
        ],
        out_specs=pl.BlockSpec((tm, D_MODEL), lambda i: (i, 0)),
        out_shape=jax.ShapeDtypeStruct((n, D_MODEL), F32),
        compiler_params=_cparams("parallel"),
        name="final",
    )(x2, norm_g, *ys, w_gate, b_gate, w_branch, w_out)


W_IN_CAST_ROWS = 128


def _cast_w_in_kernel(w_ref, slab_ref, gate_ref):
    w = w_ref[0]
    slab_ref[0] = w[:, :SLAB_COLS].astype(BF)
    gate_ref[0] = w[:, SLAB_COLS:].astype(BF)


def _cast_w_in(w_in):
    depth, d, cols = w_in.shape
    gate_cols = cols - SLAB_COLS
    return pl.pallas_call(
        _cast_w_in_kernel,
        grid=(depth, d // W_IN_CAST_ROWS),
        in_specs=[pl.BlockSpec((1, W_IN_CAST_ROWS, cols), lambda l, r: (l, r, 0))],
        out_specs=[pl.BlockSpec((1, W_IN_CAST_ROWS, SLAB_COLS), lambda l, r: (l, r, 0)),
                   pl.BlockSpec((1, W_IN_CAST_ROWS, gate_cols), lambda l, r: (l, r, 0))],
        out_shape=[jax.ShapeDtypeStruct((depth, d, SLAB_COLS), BF),
                   jax.ShapeDtypeStruct((depth, d, gate_cols), BF)],
        compiler_params=_cparams("parallel", "parallel"),
        name="cast_w_in",
    )(w_in)


def _tile4(g):
    return jnp.tile(g.astype(F32), BRANCH_W // g.shape[0])


def _layer(x, mem, layer_idx, stacked, norm_g, b_gate, a_qn_g, a_kn_g, a_rpb,
           b_qn_g, b_kn_g, b_lam_q1, b_lam_k1, b_lam_q2, b_lam_k2, b_sub_g,
           c_w, c_scale, d_ln_g, d_ln_b, d_bs, m_norm_g, m_qn_g, m_kn_g):
    b, t, d = x.shape
    n = b * t
    x2 = x.reshape(n, d)
    ng = norm_g.reshape(1, d).astype(F32)

    gains = jnp.zeros((8, BRANCH_W), F32)
    gains = gains.at[G_AQ].set(_tile4(a_qn_g) * NA_HD ** -0.5)
    gains = gains.at[G_AK].set(_tile4(a_kn_g))
    gains = gains.at[G_BQ].set(_tile4(b_qn_g) * (DA_QK ** -0.5 * LOG2E))
    gains = gains.at[G_BK].set(_tile4(b_kn_g))
    gains = gains.at[G_MQ].set(_tile4(m_qn_g) * MEM_HD ** -0.5)
    gains = gains.at[G_LNG].set(d_ln_g.astype(F32))
    gains = gains.at[G_LNB].set(d_ln_b.astype(F32))

    mk, mv = _memkv(mem, m_norm_g.reshape(1, d).astype(F32), stacked["m_wkv"], layer_idx,
                    _tile4(m_kn_g).reshape(1, BRANCH_W))
    sg_bias = jnp.repeat(d_bs.astype(F32).T, SG_GC, axis=1)
    p2, y_d, y_m, qt4, k4, vt4 = _proj(x2, ng, stacked["w_slab"], layer_idx, gains,
                                       stacked["d_ws"], sg_bias, mk, mv, t)
    p3 = p2.reshape(b, t, P_COLS)

    y_a = _na(p3, _na_bias_table(a_rpb, t // GRID_W))

    lam_init = 0.8 - 0.6 * math.exp(-0.3 * layer_idx)
    lamv = jnp.stack([b_lam_q1, b_lam_k1, b_lam_q2, b_lam_k2]).astype(F32)
    subg = (_tile4(b_sub_g) * (1.0 - lam_init)).reshape(1, BRANCH_W)
    y_b = _da(qt4, k4, vt4, p3, lamv, subg, lam_init)

    cw_bd = jnp.zeros((BRANCH_W, BRANCH_W), F32)
    for g in range(len(POOL_WINDOWS)):
        cw_bd = cw_bd.at[g * POOL_GC:(g + 1) * POOL_GC, g * POOL_GC:(g + 1) * POOL_GC].set(c_w[g].astype(F32))
    y_c = _pool(p3, cw_bd.astype(BF), c_scale.reshape(1, BRANCH_W).astype(F32))

    bg =jnp.zeros((8, d), F32).at[:N_BRANCH].set(b_gate.astype(F32))
    ys = [y.reshape(n, BRANCH_W) for y in (y_a, y_b, y_c, y_d, y_m)]
    out = _final(x2, ng, ys, stacked["w_gate"], bg, stacked["w_branch"], stacked["w_out"], layer_idx)
    return out.reshape(b, t, d)


def kernel(x, mem, norm_g, w_in, b_gate, a_qn_g, a_kn_g, a_rpb, b_qn_g, b_kn_g, b_lam_q1, b_lam_k1, b_lam_q2, b_lam_k2, b_sub_g, c_w, c_scale, d_ln_g, d_ln_b, d_ws, d_bs, m_norm_g, m_wkv, m_qn_g, m_kn_g, w_branch, w_out):
    w_slab, w_gate = _cast_w_in(w_in)
    stacked = {
        "w_slab": w_slab,
        "w_gate": w_gate,
        "w_branch": w_branch.astype(BF),
        "w_out": w_out.astype(BF),
        "m_wkv": m_wkv.astype(BF),
        "d_ws": d_ws.astype(BF),
    }
    for l in range(DEPTH):
        x = _layer(x, mem, l, stacked, norm_g[l], b_gate[l], a_qn_g[l], a_kn_g[l], a_rpb[l],
                   b_qn_g[l], b_kn_g[l], b_lam_q1[l], b_lam_k1[l], b_lam_q2[l], b_lam_k2[l], b_sub_g[l],
                   c_w[l], c_scale[l], d_ln_g[l], d_ln_b[l], d_bs[l], m_norm_g[l], m_qn_g[l], m_kn_g[l])
    return x
```

```python
import functools
import math

import numpy as np
import jax
import jax.numpy as jnp
from jax import lax
from jax.experimental import pallas as pl
from jax.experimental.pallas import tpu as pltpu

F32 = jnp.float32
BF = jnp.bfloat16

D_MODEL = 1024
DEPTH = 2
GRID_W = 64
BRANCH_W = 256
N_BRANCH = 5
NA_HEADS = 4
NA_HD = 64
NA_KH = 8
NA_KW = 16
DA_HEADS = 4
DA_QK = 32
DA_V = 64
ALIBI_BASE = 8.0
POOL_WINDOWS = (2, 4, 8, 16)
POOL_GC = 64
SG_CHUNK = 128
SG_GROUPS = 4
SG_GC = 64
MEM_HEADS = 4
MEM_HD = 64
N_SLABS = 15
SLAB_COLS = N_SLABS * BRANCH_W
EPS = 1e-6

(S_AQ, S_AK, S_AV, S_AZ, S_BQ, S_BK, S_BV, S_BZ, S_CX, S_CZ,
 S_DU, S_DV, S_DZ, S_MQ, S_MZ) = range(N_SLABS)

(G_AQ, G_AK, G_BQ, G_BK, G_MQ, G_LNG, G_LNB) = range(7)

VMEM_LIMIT_BYTES = 56 * 1024 * 1024
NEG_BIG = -1e30

ALIBI_SLOPES = tuple(2.0 ** (-ALIBI_BASE * (h + 1) / DA_HEADS) for h in range(DA_HEADS))
DA_LANES = 128


def _bf16_pieces(x, n):
    out = []
    for _ in range(n):
        piece = float(np.asarray(x, np.float32).astype(jnp.bfloat16).astype(np.float32))
        out.append(piece)
        x -= piece
    return tuple(out)


LOG2E = math.log2(math.e)
LOG2E_PIECES = _bf16_pieces(LOG2E, 3)
N_PIECES = len(LOG2E_PIECES)
X_KHI, X_KLO, X_QHI, X_QLO = (DA_V + g * N_PIECES for g in range(4))
X_END = DA_V + 4 * N_PIECES
X_ROWS = 16
POS_SPLIT = 64


def _alibi_parts(pos, slope):
    hi = (pos >> _log2(POS_SPLIT)).astype(F32) * (slope * POS_SPLIT)
    lo = (pos & (POS_SPLIT - 1)).astype(F32) * slope
    return hi, lo


def _piece_of(idx):
    out = jnp.zeros(idx.shape, F32)
    for g in range(4):
        for p, piece in enumerate(LOG2E_PIECES):
            out = jnp.where(idx == DA_V + g * N_PIECES + p, piece, out)
    return out


def _cparams(*sem):
    return pltpu.CompilerParams(dimension_semantics=sem, vmem_limit_bytes=VMEM_LIMIT_BYTES)


def _resident(shape, index_map):
    return pl.BlockSpec(shape, index_map, pipeline_mode=pl.Buffered(1))


def _layer_resident(shape, layer):
    return pl.BlockSpec((None,) + tuple(shape), lambda *_: (layer,) + (0,) * len(shape),
                        pipeline_mode=pl.Buffered(1))


def _dot(a, b):
    return jnp.dot(a, b, preferred_element_type=F32)


def _dot_nt(a, b):
    return lax.dot_general(a, b, (((1,), (1,)), ((), ())), preferred_element_type=F32)


def _log2(n):
    assert n & (n - 1) == 0, "power of two expected"
    return n.bit_length() - 1


def _lane_group(shape, width):
    return lax.broadcasted_iota(jnp.int32, shape, len(shape) - 1) >> _log2(width)


def _lane_mask(shape, lo, hi, dtype):
    lane = lax.broadcasted_iota(jnp.int32, shape, len(shape) - 1)
    return jnp.where((lane >= lo) & (lane < hi), 1.0, 0.0).astype(dtype)


def _group_mean_sq(xf, group):
    n = xf.shape[-1]
    sh = int(math.log2(group))
    r = lax.broadcasted_iota(jnp.int32, (n, n), 0) >> sh
    c = lax.broadcasted_iota(jnp.int32, (n, n), 1) >> sh
    pm = jnp.where(r == c, 1.0 / group, 0.0).astype(BF)
    xx = xf * xf
    hi = xx.astype(BF)
    lo = (xx - hi.astype(F32)).astype(BF)
    return _dot(hi, pm) + _dot(lo, pm)


def _sigmoid(x):
    return 1.0 / (1.0 + jnp.exp(-x))


def _silu(x):
    return x * _sigmoid(x)


def _gelu_tanh(x):
    return 0.5 * x * (1.0 + jnp.tanh(math.sqrt(2.0 / math.pi) * (x + 0.044715 * (x * x * x))))


def _rmsnorm_rows(x, g):
    ms = jnp.mean(x * x, axis=-1, keepdims=True)
    return x * lax.rsqrt(ms + EPS) * g


PROJ_LOOKAHEAD = 3
P_SLABS = (S_AQ, S_AK, S_AV, S_AZ, S_BZ, S_CX, S_CZ)
P_POS = {s: i for i, s in enumerate(P_SLABS)}
P_COLS = len(P_SLABS) * BRANCH_W
PROJ_ORDER = (S_MQ, S_MZ, S_DU, S_DV, S_DZ, S_BQ, S_BK, S_BV) + P_SLABS


def _proj_kernel(tiles_per_seq, x_ref, ng_ref, w_ref, gains_ref, ws_ref, sgb_ref, mk_ref, mv_ref,
                 p_ref, yd_ref, ym_ref, qt_ref, k4_ref, vt_ref):
    tm = x_ref.shape[0]
    t0 = (pl.program_id(0) % tiles_per_seq) * tm
    h = _rmsnorm_rows(x_ref[...], ng_ref[...]).astype(BF)

    def gnorm(r, group, row):
        return r * lax.rsqrt(_group_mean_sq(r, group) + EPS) * gains_ref[row:row + 1, :]

    def slab(j):
        return _dot(h, w_ref[:, j * BRANCH_W:(j + 1) * BRANCH_W])

    kept = {}
    pending = [slab(j) for j in PROJ_ORDER[:PROJ_LOOKAHEAD]]
    for idx, j in enumerate(PROJ_ORDER):
        if idx + PROJ_LOOKAHEAD < N_SLABS:
            pending.append(slab(PROJ_ORDER[idx + PROJ_LOOKAHEAD]))
        r = pending.pop(0)
        if j == S_AQ:
            r = gnorm(r, NA_HD, G_AQ)
        elif j == S_AK:
            r = gnorm(r, NA_HD, G_AK)
        elif j == S_BQ:
            r = gnorm(r, DA_QK, G_BQ)
        elif j == S_BK:
            r = gnorm(r, DA_QK, G_BK)
        elif j == S_MQ:
            r = gnorm(r, MEM_HD, G_MQ)
        elif j in (S_AZ, S_BZ, S_CZ, S_DZ, S_MZ):
            r = _silu(r)
        elif j == S_DU:
            r = _gelu_tanh(r)
        elif j == S_DV:
            v = _gelu_tanh(r)
            mu = jnp.mean(v, axis=-1, keepdims=True)
            vc = v - mu
            var = jnp.mean(vc * vc, axis=-1, keepdims=True)
            r = vc * lax.rsqrt(var + EPS) * gains_ref[G_LNG:G_LNG + 1, :] + gains_ref[G_LNB:G_LNB + 1, :]
        if j in P_POS:
            p_ref[:, P_POS[j] * BRANCH_W:(P_POS[j] + 1) * BRANCH_W] = r.astype(BF)
        elif j == S_MZ:
            ym_ref[...] = _mem_attend(kept.pop(S_MQ), r, mk_ref[0], mv_ref[0]).astype(BF)
        elif j == S_DZ:
            _spatial_gate(kept.pop(S_DU), kept.pop(S_DV), r, ws_ref, sgb_ref, yd_ref)
        elif j == S_BV:
            _da_layouts(kept.pop(S_BQ), kept.pop(S_BK), r.astype(BF), t0, qt_ref, k4_ref, vt_ref)
        elif j in (S_MQ, S_DV, S_BQ, S_BK):
            kept[j] = r.astype(BF)
        else:
            kept[j] = r


def _proj(x2, norm_g, w_slab, layer, gains, ws, sg_bias, mk, mv, seq_len, tm=512):
    n = x2.shape[0]
    b = n // seq_len
    tps = seq_len // tm
    m = mk.shape[1]
    return pl.pallas_call(
        functools.partial(_proj_kernel, tps),
        grid=(n // tm,),
        in_specs=[
            pl.BlockSpec((tm, D_MODEL), lambda i: (i, 0)),
            _resident((1, D_MODEL), lambda i: (0, 0)),
            _layer_resident((D_MODEL, SLAB_COLS), layer),
            _resident((8, BRANCH_W), lambda i: (0, 0)),
            _layer_resident((SG_GROUPS, SG_CHUNK, SG_CHUNK), layer),
            _resident((SG_CHUNK, BRANCH_W), lambda i: (0, 0)),
            pl.BlockSpec((1, m, BRANCH_W), lambda i: (i // tps, 0, 0)),
            pl.BlockSpec((1, m, BRANCH_W), lambda i: (i // tps, 0, 0)),
        ],
        out_specs=[
            pl.BlockSpec((tm, P_COLS), lambda i: (i, 0)),
            pl.BlockSpec((tm, BRANCH_W), lambda i: (i, 0)),
            pl.BlockSpec((tm, BRANCH_W), lambda i: (i, 0)),
            pl.BlockSpec((1, DA_HEADS, DA_V, tm), lambda i: (i // tps, 0, 0, i % tps)),
            pl.BlockSpec((1, DA_HEADS, tm, DA_LANES), lambda i: (i // tps, 0, i % tps, 0)),
            pl.BlockSpec((1, DA_HEADS, DA_VT_ROWS, tm), lambda i: (i // tps, 0, 0, i % tps)),
        ],
        out_shape=[
            jax.ShapeDtypeStruct((n, P_COLS), BF),
            jax.ShapeDtypeStruct((n, BRANCH_W), BF),
            jax.ShapeDtypeStruct((n, BRANCH_W), BF),
            jax.ShapeDtypeStruct((b, DA_HEADS, DA_V, seq_len), BF),
            jax.ShapeDtypeStruct((b, DA_HEADS, seq_len, DA_LANES), BF),
            jax.ShapeDtypeStruct((b, DA_HEADS, DA_VT_ROWS, seq_len), BF),
        ],
        compiler_params=_cparams("parallel"),
        name="proj",
    )(x2, norm_g, w_slab, gains, ws, sg_bias, mk, mv)


def _memkv_kernel(mem_ref, g_ref, w_ref, kg_ref, k_ref, v_ref):
    h = _rmsnorm_rows(mem_ref[0], g_ref[...]).astype(BF)
    k = _dot(h, w_ref[:, :BRANCH_W])
    v = _dot(h, w_ref[:, BRANCH_W:])
    k = k * lax.rsqrt(_group_mean_sq(k, MEM_HD) + EPS) * kg_ref[...]
    k_ref[0] = k.astype(BF)
    v_ref[0] = v.astype(BF)


def _memkv(mem, m_norm_g, m_wkv, layer, kg):
    b, m, _ = mem.shape
    return pl.pallas_call(
        _memkv_kernel,
        grid=(b,),
        in_specs=[
            pl.BlockSpec((1, m, D_MODEL), lambda i: (i, 0, 0)),
            _resident((1, D_MODEL), lambda i: (0, 0)),
            _layer_resident((D_MODEL, 2 * BRANCH_W), layer),
            _resident((1, BRANCH_W), lambda i: (0, 0)),
        ],
        out_specs=[pl.BlockSpec((1, m, BRANCH_W), lambda i: (i, 0, 0)),
                   pl.BlockSpec((1, m, BRANCH_W), lambda i: (i, 0, 0))],
        out_shape=[jax.ShapeDtypeStruct((b, m, BRANCH_W), BF)] * 2,
        compiler_params=_cparams("parallel"),
        name="memkv",
    )(mem, m_norm_g, m_wkv, kg)


NA_ROWS_PER_STEP = 8
NA_WIN = NA_KH * GRID_W
NA_LOOKAHEAD = 3


def _na_kernel(q_ref, k_ref, v_ref, z_ref, bias_ref, o_ref):
    n_rows = k_ref.shape[1] // GRID_W
    rb = pl.program_id(1)
    head_of_lane = _lane_group((GRID_W, BRANCH_W), NA_HD)
    masks = [_lane_mask((GRID_W, BRANCH_W), h * NA_HD, (h + 1) * NA_HD, BF) for h in range(NA_HEADS)]

    def window(rr):
        r = rb * NA_ROWS_PER_STEP + rr
        rs = jnp.clip(r - NA_KH // 2, 0, n_rows - NA_KH)
        cls = jnp.where(r < NA_KH // 2, r, jnp.where(r > n_rows - NA_KH // 2, r - (n_rows - NA_KH), NA_KH // 2))
        return pl.multiple_of(rs * GRID_W, GRID_W), cls

    def scores(rr):
        w0, cls = window(rr)
        q = q_ref[0, rr * GRID_W:(rr + 1) * GRID_W, :]
        qs = jnp.concatenate([q * masks[h] for h in range(NA_HEADS)], axis=0)
        return _dot_nt(qs, k_ref[0, pl.ds(w0, NA_WIN), :]) + bias_ref[cls]

    pending = [scores(rr) for rr in range(NA_LOOKAHEAD)]
    for rr in range(NA_ROWS_PER_STEP):
        if rr + NA_LOOKAHEAD < NA_ROWS_PER_STEP:
            pending.append(scores(rr + NA_LOOKAHEAD))
        s = pending.pop(0)
        w0, _ = window(rr)
        m = jnp.max(s, axis=-1, keepdims=True)
        p = jnp.exp2(s - m)
        l = jnp.sum(p, axis=-1, keepdims=True)
        pv = _dot(p.astype(BF), v_ref[0, pl.ds(w0, NA_WIN), :]) / l
        o = jnp.zeros((GRID_W, BRANCH_W), F32)
        for h in range(NA_HEADS):
            o = jnp.where(head_of_lane == h, pv[h * GRID_W:(h + 1) * GRID_W], o)
        rows = slice(rr * GRID_W, (rr + 1) * GRID_W)
        o_ref[0, rows, :] = (o * z_ref[0, rows, :].astype(F32)).astype(BF)


def _na(p3, bias):
    b, t, _ = p3.shape
    tq = NA_ROWS_PER_STEP * GRID_W
    return pl.pallas_call(
        _na_kernel,
        grid=(b, t // tq),
        in_specs=[
            pl.BlockSpec((1, tq, BRANCH_W), lambda i, j: (i, j, P_POS[S_AQ])),
            pl.BlockSpec((1, t, BRANCH_W), lambda i, j: (i, 0, P_POS[S_AK])),
            pl.BlockSpec((1, t, BRANCH_W), lambda i, j: (i, 0, P_POS[S_AV])),
            pl.BlockSpec((1, tq, BRANCH_W), lambda i, j: (i, j, P_POS[S_AZ])),
            _resident(bias.shape, lambda i, j: (0, 0, 0)),
        ],
        out_specs=pl.BlockSpec((1, tq, BRANCH_W), lambda i, j: (i, j, 0)),
        out_shape=jax.ShapeDtypeStruct((b, t, BRANCH_W), BF),
        compiler_params=_cparams("parallel", "arbitrary"),
        name="na",
    )(p3, p3, p3, p3, bias)


NA_REL_ROWS = 2 * NA_KH - 1
NA_REL_COLS = 2 * NA_KW - 1
NA_CLASSES = NA_KH


def _na_class_roff(n_rows):
    half = NA_KH // 2
    rows = np.array(list(range(half)) + [half] + [n_rows - half + 1 + i for i in range(half - 1)])
    rs = np.clip(rows - half, 0, n_rows - NA_KH)
    return [int(v) for v in rs - rows + NA_KH - 1]


def _nabias_kernel(roff0, rpb_ref, o_ref, f_scr):
    h = pl.program_id(0)
    shape = (GRID_W, 2 * GRID_W)
    q = lax.broadcasted_iota(jnp.int32, shape, 0)
    lane = lax.broadcasted_iota(jnp.int32, shape, 1)
    k = lane & (GRID_W - 1)
    coff = k - q + (NA_KW - 1)
    cs = jnp.clip(q - NA_KW // 2, 0, GRID_W - NA_KW)
    valid = (k >= cs) & (k < cs + NA_KW)
    first = lane < GRID_W

    def tile(r, carry):
        base = (h * NA_REL_ROWS + r) * NA_REL_COLS
        acc = jnp.zeros(shape, F32)
        for s in range(NA_REL_COLS):
            acc = jnp.where(coff == s, jnp.where(first, rpb_ref[base + s], rpb_ref[base + NA_REL_COLS + s]), acc)
        f_scr[r] = jnp.where(valid, acc * LOG2E, NEG_BIG)
        return carry

    lax.fori_loop(0, NA_REL_ROWS - 1, tile, 0)
    for c in range(NA_CLASSES):
        for p in range(NA_KH // 2):
            o_ref[c, 0, :, p * 2 * GRID_W:(p + 1) * 2 * GRID_W] = f_scr[roff0[c] + 2 * p]


def _na_bias_table(rpb, n_rows):
    return pl.pallas_call(
        functools.partial(_nabias_kernel, _na_class_roff(n_rows)),
        grid=(NA_HEADS,),
        in_specs=[pl.BlockSpec(memory_space=pltpu.SMEM)],
        out_specs=pl.BlockSpec((NA_CLASSES, 1, GRID_W, NA_WIN), lambda h: (0, h, 0, 0)),
        out_shape=jax.ShapeDtypeStruct((NA_CLASSES, NA_HEADS, GRID_W, NA_WIN), F32),
        scratch_shapes=[pltpu.VMEM((NA_REL_ROWS - 1, GRID_W, 2 * GRID_W), F32)],
        compiler_params=_cparams("arbitrary"),
        name="nabias",
    )(rpb.astype(F32).reshape(-1)).reshape(NA_CLASSES, NA_HEADS * GRID_W, NA_WIN)


DA_VT_ROWS = 80
X_SUM = DA_V


def _da_layouts(q, k, v, t0, qt_ref, k4_ref, vt_ref):
    tt = q.shape[0]
    src = lax.broadcasted_iota(jnp.int32, (BRANCH_W, DA_LANES), 0)
    dst = lax.broadcasted_iota(jnp.int32, (BRANCH_W, DA_LANES), 1)
    src_t = lax.broadcasted_iota(jnp.int32, (DA_LANES, BRANCH_W), 1)
    dst_t = lax.broadcasted_iota(jnp.int32, (DA_LANES, BRANCH_W), 0)
    klane = lax.broadcasted_iota(jnp.int32, (tt, DA_LANES), 1)
    kpos = t0 + lax.broadcasted_iota(jnp.int32, (tt, DA_LANES), 0)
    vrow = lax.broadcasted_iota(jnp.int32, (DA_VT_ROWS, tt), 0)
    for h in range(DA_HEADS):
        sel = jnp.where((src == h * DA_V + dst) & (dst < DA_V), 1.0, 0.0).astype(BF)
        sel_t = jnp.where((src_t == h * DA_V + dst_t) & (dst_t < DA_V), 1.0, 0.0).astype(BF)
        qt_ref[0, h] = _dot_nt(sel_t[:DA_V], q).astype(BF)
        vt = _dot_nt(sel_t[:DA_VT_ROWS], v)
        vt_ref[0, h] = jnp.where(vrow == X_SUM, 1.0, vt).astype(BF)
        slope = ALIBI_SLOPES[h]
        k4 = _dot(k, sel)
        jh, jl = _alibi_parts(kpos, slope)
        k4 = jnp.where((klane >= X_KHI) & (klane < X_KLO), jh, k4)
        k4 = jnp.where((klane >= X_KLO) & (klane < X_QHI), jl, k4)
        k4 = jnp.where((klane >= X_QHI) & (klane < X_END), _piece_of(klane), k4)
        k4_ref[0, h] = k4.astype(BF)


DA_QCOLS = 256
DA_KROWS = 256
DA_LOOKAHEAD = 6
Q_LEFT, Q_DIAG, Q_RIGHT = range(3)


def _da_kernel(lam_init, qt_ref, k_ref, vt_ref, z_ref, lamv_ref, subg_ref, o_ref,
               qv_scr, dist_scr, spre_scr, m_scr, acc_scr):
    bq = qt_ref.shape[3]
    bk = bq
    nk = k_ref.shape[2] // bk
    qi = pl.program_id(1)
    frow = lax.broadcasted_iota(jnp.int32, (DA_V, bq), 0)
    xrow = DA_V + lax.broadcasted_iota(jnp.int32, (X_ROWS, bq), 0)
    qpos = qi * bq + lax.broadcasted_iota(jnp.int32, (X_ROWS, bq), 1)

    lv = lamv_ref[...]
    lam = (jnp.exp(jnp.sum(lv[0:1] * lv[1:2], axis=-1, keepdims=True))
           - jnp.exp(jnp.sum(lv[2:3] * lv[3:4], axis=-1, keepdims=True)) + lam_init)

    cols = [slice(c * DA_QCOLS, (c + 1) * DA_QCOLS) for c in range(bq // DA_QCOLS)]
    chains = [(h, m, c) for h in range(DA_HEADS) for m in range(2) for c in range(len(cols))]

    def tiles_of(blk):
        return [(blk, kc * DA_KROWS, ch) for kc in range(bk // DA_KROWS) for ch in chains]

    def scores(tile):
        (k0, side, diag), r0, (h, m, c) = tile
        if diag:
            c0 = c * DA_QCOLS
            side = Q_DIAG if r0 == c0 else (Q_LEFT if r0 < c0 else Q_RIGHT)
        s = _dot(k_ref[0, h, pl.ds(k0 + r0, DA_KROWS), :], qv_scr[h, side, m, :, cols[c]])
        if diag and r0 == c0:
            s = s - dist_scr[...] * (ALIBI_SLOPES[h] * LOG2E)
        return s

    def run_block(blk, nxt):
        tiles = tiles_of(blk)
        pending = []
        for i, ((k0, _, _), r0, (h, m, c)) in enumerate(tiles):
            j = i + DA_LOOKAHEAD
            if j < len(tiles):
                pending.append(scores(tiles[j]))
            elif nxt is not None:
                spre_scr[j - len(tiles)] = scores(tiles_of(nxt)[j - len(tiles)])
            s = spre_scr[i] if i < DA_LOOKAHEAD else pending.pop(0)
            m_old = m_scr[h, m, :, cols[c]]
            m_new = jnp.maximum(m_old, jnp.max(s, axis=0, keepdims=True))
            alpha = jnp.exp2(m_old - m_new)
            p = jnp.exp2(s - m_new).astype(BF)
            pv = _dot(vt_ref[0, h, :, pl.ds(k0 + r0, DA_KROWS)], p)
            acc_scr[h, m, :, cols[c]] = alpha * acc_scr[h, m, :, cols[c]] + pv
            m_scr[h, m, :, cols[c]] = m_new

    assert DA_KROWS == DA_QCOLS, "diagonal tiles are taken to be square"
    rel = (lax.broadcasted_iota(jnp.int32, (DA_KROWS, DA_QCOLS), 0)
           - lax.broadcasted_iota(jnp.int32, (DA_KROWS, DA_QCOLS), 1))
    dist_scr[...] = jnp.abs(rel).astype(F32)
    pieces = _piece_of(xrow)
    tail = jnp.zeros((DA_LANES - DA_V - X_ROWS, bq), BF)
    for h in range(DA_HEADS):
        slope = ALIBI_SLOPES[h]
        ih, il = _alibi_parts(qpos, slope)
        ex =jnp.where(xrow < X_QHI, pieces,
                       jnp.where(xrow < X_QLO, -ih, jnp.where(xrow < X_END, -il, 0.0))).astype(BF)
        for m in range(2):
            qm = qt_ref[0, h] * jnp.where(frow >> _log2(DA_QK) == m, 1.0, 0.0).astype(BF)
            for side, extra in ((Q_LEFT, ex), (Q_DIAG, jnp.zeros_like(ex)), (Q_RIGHT, -ex)):
                qv_scr[h, side, m, :DA_V, :] = qm
                qv_scr[h, side, m, DA_V:DA_V + X_ROWS, :] = extra
                qv_scr[h, side, m, DA_V + X_ROWS:, :] = tail
    m_scr[...] = jnp.full(m_scr.shape, NEG_BIG, F32)
    acc_scr[...] = jnp.zeros(acc_scr.shape, F32)

    def off_diagonal(t):
        kb = jnp.where(t >= qi, t + 1, t)
        return pl.multiple_of(kb * bk, bk), jnp.where(t >= qi, Q_RIGHT, Q_LEFT), False

    diagonal = (pl.multiple_of(qi * bk, bk), Q_DIAG, True)

    for i, tile in enumerate(tiles_of(off_diagonal(0))[:DA_LOOKAHEAD]):
        spre_scr[i] = scores(tile)

    def body(t, carry):
        run_block(off_diagonal(t), off_diagonal(t + 1))
        return carry

    lax.fori_loop(0, nk - 2, body, 0)
    run_block(off_diagonal(nk - 2), diagonal)
    run_block(diagonal, None)

    heads = []
    for h in range(DA_HEADS):
        o = [acc_scr[h, m, :DA_V, :] / acc_scr[h, m, X_SUM:X_SUM + 1, :] for m in range(2)]
        heads.append(o[0] - lam * o[1])

    full = jnp.concatenate(heads, axis=0).T
    y = full * lax.rsqrt(_group_mean_sq(full, DA_V) + EPS) * subg_ref[...]
    o_ref[0] = (y * z_ref[0].astype(F32)).astype(BF)


def _da(qt4, k4, vt4, p3, lamv, subg, lam_init, bq=512):
    b, _, t, _ = k4.shape
    assert all(math.frexp(s)[0] == 0.5 for s in ALIBI_SLOPES), "ALiBi slopes must be powers of two"
    assert t <= POS_SPLIT * 256 and POS_SPLIT <= 256
    return pl.pallas_call(
        functools.partial(_da_kernel, lam_init),
        grid=(b, t // bq),
        in_specs=[
            pl.BlockSpec((1, DA_HEADS, DA_V, bq), lambda i, j: (i, 0, 0, j)),
            pl.BlockSpec((1, DA_HEADS, t, DA_LANES), lambda i, j: (i, 0, 0, 0)),
            pl.BlockSpec((1, DA_HEADS, DA_VT_ROWS, t), lambda i, j: (i, 0, 0, 0)),
            pl.BlockSpec((1, bq, BRANCH_W), lambda i, j: (i, j, P_POS[S_BZ])),
            _resident((4, DA_QK), lambda i, j: (0, 0)),
            _resident((1, BRANCH_W), lambda i, j: (0, 0)),
        ],
        out_specs=pl.BlockSpec((1, bq, BRANCH_W), lambda i, j: (i, j, 0)),
        out_shape=jax.ShapeDtypeStruct((b, t, BRANCH_W), BF),
        scratch_shapes=[
            pltpu.VMEM((DA_HEADS, 3, 2, DA_LANES, bq), BF),
            pltpu.VMEM((DA_KROWS, DA_QCOLS), F32),
            pltpu.VMEM((DA_LOOKAHEAD, DA_KROWS, DA_QCOLS), F32),
            pltpu.VMEM((DA_HEADS, 2, 1, bq), F32),
            pltpu.VMEM((DA_HEADS, 2, DA_VT_ROWS, bq), F32),
        ],
        compiler_params=_cparams("parallel", "arbitrary"),
        name="da",
    )(qt4, k4, vt4, p3, lamv, subg)


POOL_HALO = 16


def _pool_kernel(x_ref, z_ref, band_ref, cw_ref, cs_ref, o_ref):
    tt = z_ref.shape[1]
    t_total = x_ref.shape[1]
    nt = t_total // tt
    ti = pl.program_id(1)
    t0 = pl.multiple_of(ti * tt, tt)
    xm = x_ref[0, pl.ds(t0, tt), :]
    p0 = pl.multiple_of(jnp.maximum(t0 - POOL_HALO, 0), POOL_HALO)
    n0 = pl.multiple_of(jnp.minimum(t0 + tt, t_total - POOL_HALO), POOL_HALO)
    has_prev = (ti > 0).astype(F32)
    has_next = (ti < nt - 1).astype(F32)
    prev = (x_ref[0, pl.ds(p0, POOL_HALO), :].astype(F32) * has_prev).astype(BF)
    nxt = (x_ref[0, pl.ds(n0, POOL_HALO), :].astype(F32) * has_next).astype(BF)
    xcat = jnp.concatenate([prev, xm, nxt], axis=0)

    sub = band_ref.shape[1]
    group = _lane_group((sub, BRANCH_W), POOL_GC)
    half = jnp.left_shift(1, group)
    sums = [[_dot(band_ref[g], xcat[r0:r0 + sub + 2 * POOL_HALO]) for g in range(len(POOL_WINDOWS))]
            for r0 in range(0, tt, sub)]
    for i, r0 in enumerate(range(0, tt, sub)):
        wsum = jnp.zeros((sub, BRANCH_W), F32)
        for g in range(len(POOL_WINDOWS)):
            wsum = jnp.where(group == g, sums[i][g], wsum)
        tpos = t0 + r0 + lax.broadcasted_iota(jnp.int32, (sub, BRANCH_W), 0)
        lo = jnp.maximum(tpos - half, 0)
        hi = jnp.minimum(tpos + half - 1, t_total - 1)
        cnt = (hi - lo + 1).astype(F32)
        dlt = wsum / cnt - xm[r0:r0 + sub].astype(F32)
        y = _dot(dlt.astype(BF), cw_ref[...]) * cs_ref[...]
        o_ref[0, r0:r0 + sub, :] = (y * z_ref[0, r0:r0 + sub, :].astype(F32)).astype(BF)


def _pool_band(tt):
    band = np.zeros((len(POOL_WINDOWS), tt, tt + 2 * POOL_HALO), np.float32)
    t = np.arange(tt)[:, None]
    u = np.arange(tt + 2 * POOL_HALO)[None, :] - POOL_HALO
    for g, w in enumerate(POOL_WINDOWS):
        band[g] = ((u >= t - w // 2) & (u <= t - w // 2 + w - 1)).astype(np.float32)
    return jnp.asarray(band, BF)


def _pool(p3, cw_bd, c_scale, tt=512, sub=256):
    b, t, _ = p3.shape
    band = _pool_band(sub)
    return pl.pallas_call(
        _pool_kernel,
        grid=(b, t // tt),
        in_specs=[
            pl.BlockSpec((1, t, BRANCH_W), lambda i, j: (i, 0, P_POS[S_CX])),
            pl.BlockSpec((1, tt, BRANCH_W), lambda i, j: (i, j, P_POS[S_CZ])),
            _resident(band.shape, lambda i, j: (0, 0, 0)),
            _resident((BRANCH_W, BRANCH_W), lambda i, j: (0, 0)),
            _resident((1, BRANCH_W), lambda i, j: (0, 0)),
        ],
        out_specs=pl.BlockSpec((1, tt, BRANCH_W), lambda i, j: (i, j, 0)),
        out_shape=jax.ShapeDtypeStruct((b, t, BRANCH_W), BF),
        compiler_params=_cparams("parallel", "arbitrary"),
        name="pool",
    )(p3, p3, band, cw_bd, c_scale)


def _spatial_gate(u, vn, z, ws_ref, bias_ref, o_ref):
    group = _lane_group((SG_CHUNK, BRANCH_W), SG_GC)
    for c in range(u.shape[0] // SG_CHUNK):
        rows = slice(c * SG_CHUNK, (c + 1) * SG_CHUNK)
        mix = jnp.zeros((SG_CHUNK, BRANCH_W), F32)
        for g in range(SG_GROUPS):
            mix = jnp.where(group == g, _dot(ws_ref[g], vn[rows]), mix)
        o_ref[rows, :] = (u[rows] * (mix + bias_ref[...]) * z[rows]).astype(BF)


def _mem_attend(q, z, k, v):
    head_of_lane = _lane_group(q.shape, MEM_HD)
    scores = [_dot_nt(q * _lane_mask(q.shape, h * MEM_HD, (h + 1) * MEM_HD, BF), k) for h in range(MEM_HEADS)]
    o = jnp.zeros(q.shape, F32)
    for h in range(MEM_HEADS):
        s = scores[h]
        m = jnp.max(s, axis=-1, keepdims=True)
        p = jnp.exp2(s - m)
        l = jnp.sum(p, axis=-1, keepdims=True)
        oh = _dot(p.astype(BF), v) / l
        o = jnp.where(head_of_lane == h, oh, o)
    return o * z


def _final_kernel(x_ref, ng_ref, ya_ref, yb_ref, yc_ref, yd_ref, ym_ref,
                  wg_ref, bg_ref, wb_ref, wo_ref, o_ref):
    x = x_ref[...]
    h = _rmsnorm_rows(x, ng_ref[...]).astype(BF)
    merged = jnp.zeros(x.shape, F32)
    for i, y_ref in enumerate((ya_ref, yb_ref, yc_ref, yd_ref, ym_ref)):
        logit = _dot(h, wg_ref[:, i * D_MODEL:(i + 1) * D_MODEL]) + bg_ref[i:i + 1, :]
        merged = merged + _sigmoid(logit) * _dot(y_ref[...], wb_ref[i])
    o_ref[...] = x + _dot(merged.astype(BF), wo_ref[...])


def _final(x2, norm_g, ys, w_gate, b_gate, w_branch, w_out, layer, tm=512):
    n = x2.shape[0]
    yspec = pl.BlockSpec((tm, BRANCH_W), lambda i: (i, 0))
    return pl.pallas_call(
        _final_kernel,
        grid=(n // tm,),
        in_specs=[
            pl.BlockSpec((tm, D_MODEL), lambda i: (i, 0)),
            _resident((1, D_MODEL), lambda i: (0, 0)),
            yspec, yspec, yspec, yspec, yspec,
            _layer_resident((D_MODEL, N_BRANCH * D_MODEL), layer),
            _resident((8, D_MODEL), lambda i: (0, 0)),
            _layer_resident((N_BRANCH, BRANCH_W, D_MODEL), layer),
            _layer_resident((D_MODEL, D_MODEL), layer),
        ],
        out_specs=pl.BlockSpec((tm, D_MODEL), lambda i: (i, 0)),
        out_shape=jax.ShapeDtypeStruct((n, D_MODEL), F32),
        compiler_params=_cparams("parallel"),
        name="final",
    )(x2, norm_g, *ys, w_gate, b_gate, w_branch, w_out)


W_IN_CAST_ROWS = 128


def _cast_w_in_kernel(w_ref, slab_ref, gate_ref):
    w = w_ref[0]
    slab_ref[0] = w[:, :SLAB_COLS].astype(BF)
    gate_ref[0] = w[:, SLAB_COLS:].astype(BF)


def _cast_w_in(w_in):
    depth, d, cols = w_in.shape
    gate_cols = cols - SLAB_COLS
    return pl.pallas_call(
        _cast_w_in_kernel,
        grid=(depth, d // W_IN_CAST_ROWS),
        in_specs=[pl.BlockSpec((1, W_IN_CAST_ROWS, cols), lambda l, r: (l, r, 0))],
        out_specs=[pl.BlockSpec((1, W_IN_CAST_ROWS, SLAB_COLS), lambda l, r: (l, r, 0)),
                   pl.BlockSpec((1, W_IN_CAST_ROWS, gate_cols), lambda l, r: (l, r, 0))],
        out_shape=[jax.ShapeDtypeStruct((depth, d, SLAB_COLS), BF),
                   jax.ShapeDtypeStruct((depth, d, gate_cols), BF)],
        compiler_params=_cparams("parallel", "parallel"),
        name="cast_w_in",
    )(w_in)


def _tile4(g):
    return jnp.tile(g.astype(F32), BRANCH_W // g.shape[0])


def _layer(x, mem, layer_idx, stacked, norm_g, b_gate, a_qn_g, a_kn_g, a_rpb,
           b_qn_g, b_kn_g, b_lam_q1, b_lam_k1, b_lam_q2, b_lam_k2, b_sub_g,
           c_w, c_scale, d_ln_g, d_ln_b, d_bs, m_norm_g, m_qn_g, m_kn_g):
    b, t, d = x.shape
    n = b * t
    x2 = x.reshape(n, d)
    ng = norm_g.reshape(1, d).astype(F32)

    gains = jnp.zeros((8, BRANCH_W), F32)
    gains = gains.at[G_AQ].set(_tile4(a_qn_g) * (NA_HD ** -0.5 * LOG2E))
    gains = gains.at[G_AK].set(_tile4(a_kn_g))
    gains = gains.at[G_BQ].set(_tile4(b_qn_g) * (DA_QK ** -0.5 * LOG2E))
    gains = gains.at[G_BK].set(_tile4(b_kn_g))
    gains = gains.at[G_MQ].set(_tile4(m_qn_g) * (MEM_HD ** -0.5 * LOG2E))
    gains = gains.at[G_LNG].set(d_ln_g.astype(F32))
    gains = gains.at[G_LNB].set(d_ln_b.astype(F32))

    mk, mv = _memkv(mem, m_norm_g.reshape(1, d).astype(F32), stacked["m_wkv"], layer_idx,
                    _tile4(m_kn_g).reshape(1, BRANCH_W))
    sg_bias = jnp.repeat(d_bs.astype(F32).T, SG_GC, axis=1)
    p2, y_d, y_m, qt4, k4, vt4 = _proj(x2, ng, stacked["w_slab"], layer_idx, gains,
                                       stacked["d_ws"], sg_bias, mk, mv, t)
    p3 = p2.reshape(b, t, P_COLS)

    y_a = _na(p3, _na_bias_table(a_rpb, t // GRID_W))

    lam_init = 0.8 - 0.6 * math.exp(-0.3 * layer_idx)
    lamv = jnp.stack([b_lam_q1, b_lam_k1, b_lam_q2, b_lam_k2]).astype(F32)
    subg = (_tile4(b_sub_g) * (1.0 - lam_init)).reshape(1, BRANCH_W)
    y_b = _da(qt4, k4, vt4, p3, lamv, subg, lam_init)

    cw_bd = jnp.zeros((BRANCH_W, BRANCH_W), F32)
    for g in range(len(POOL_WINDOWS)):
        cw_bd = cw_bd.at[g * POOL_GC:(g + 1) * POOL_GC, g * POOL_GC:(g + 1) * POOL_GC].set(c_w[g].astype(F32))
    y_c = _pool(p3, cw_bd.astype(BF), c_scale.reshape(1, BRANCH_W).astype(F32))

    bg =jnp.zeros((8, d), F32).at[:N_BRANCH].set(b_gate.astype(F32))
    ys = [y.reshape(n, BRANCH_W) for y in (y_a, y_b, y_c, y_d, y_m)]
    out = _final(x2, ng, ys, stacked["w_gate"], bg, stacked["w_branch"], stacked["w_out"], layer_idx)
    return out.reshape(b, t, d)


def kernel(x, mem, norm_g, w_in, b_gate, a_qn_g, a_kn_g, a_rpb, b_qn_g, b_kn_g, b_lam_q1, b_lam_k1, b_lam_q2, b_lam_k2, b_sub_g, c_w, c_scale, d_ln_g, d_ln_b, d_ws, d_bs, m_norm_g, m_wkv, m_qn_g, m_kn_g, w_branch, w_out):
    w_slab, w_gate = _cast_w_in(w_in)
    stacked = {
        "w_slab": w_slab,
        "w_gate": w_gate,
        "w_branch": w_branch.astype(BF),
        "w_out": w_out.astype(BF),
        "m_wkv": m_wkv.astype(BF),
        "d_ws": d_ws.astype(BF),
    }
    for l in range(DEPTH):
        x = _layer(x, mem, l, stacked, norm_g[l], b_gate[l], a_qn_g[l], a_kn_g[l], a_rpb[l],
                   b_qn_g[l], b_kn_g[l], b_lam_q1[l], b_lam_k1[l], b_lam_q2[l], b_lam_k2[l], b_sub_g[l],
                   c_w[l], c_scale[l], d_ln_g[l], d_ln_b[l], d_bs[l], m_norm_g[l], m_qn_g[l], m_kn_g[l])
    return x
```

```python
import functools
import math

import numpy as np
import jax
import jax.numpy as jnp
from jax import lax
from jax.experimental import pallas as pl
from jax.experimental.pallas import tpu as pltpu

F32 = jnp.float32
BF = jnp.bfloat16

D_MODEL = 1024
DEPTH = 2
GRID_W = 64
BRANCH_W = 256
N_BRANCH = 5
NA_HEADS = 4
NA_HD = 64
NA_KH = 8
NA_KW = 16
DA_HEADS = 4
DA_QK = 32
DA_V = 64
ALIBI_BASE = 8.0
POOL_WINDOWS = (2, 4, 8, 16)
POOL_GC = 64
SG_CHUNK = 128
SG_GROUPS = 4
SG_GC = 64
MEM_HEADS = 4
MEM_HD = 64
N_SLABS = 15
SLAB_COLS = N_SLABS * BRANCH_W
EPS = 1e-6

(S_AQ, S_AK, S_AV, S_AZ, S_BQ, S_BK, S_BV, S_BZ, S_CX, S_CZ,
 S_DU, S_DV, S_DZ, S_MQ, S_MZ) = range(N_SLABS)

(G_AQ, G_AK, G_BQ, G_BK, G_MQ, G_LNG, G_LNB) = range(7)

VMEM_LIMIT_BYTES = 56 * 1024 * 1024
NEG_BIG = -1e30

ALIBI_SLOPES = tuple(2.0 ** (-ALIBI_BASE * (h + 1) / DA_HEADS) for h in range(DA_HEADS))
DA_LANES = 128


def _bf16_pieces(x, n):
    out = []
    for _ in range(n):
        piece = float(np.asarray(x, np.float32).astype(jnp.bfloat16).astype(np.float32))
        out.append(piece)
        x -= piece
    return tuple(out)


LOG2E = math.log2(math.e)
LOG2E_PIECES = _bf16_pieces(LOG2E, 3)
N_PIECES = len(LOG2E_PIECES)
X_KHI, X_KLO, X_QHI, X_QLO = (DA_V + g * N_PIECES for g in range(4))
X_END = DA_V + 4 * N_PIECES
X_ROWS = 16
POS_SPLIT = 64


def _alibi_parts(pos, slope):
    hi = (pos >> _log2(POS_SPLIT)).astype(F32) * (slope * POS_SPLIT)
    lo = (pos & (POS_SPLIT - 1)).astype(F32) * slope
    return hi, lo


def _piece_of(idx):
    out = jnp.zeros(idx.shape, F32)
    for g in range(4):
        for p, piece in enumerate(LOG2E_PIECES):
            out = jnp.where(idx == DA_V + g * N_PIECES + p, piece, out)
    return out


def _cparams(*sem):
    return pltpu.CompilerParams(dimension_semantics=sem, vmem_limit_bytes=VMEM_LIMIT_BYTES)


def _resident(shape, index_map):
    return pl.BlockSpec(shape, index_map, pipeline_mode=pl.Buffered(1))


def _layer_resident(shape, layer):
    return pl.BlockSpec((None,) + tuple(shape), lambda *_: (layer,) + (0,) * len(shape),
                        pipeline_mode=pl.Buffered(1))


def _dot(a, b):
    return jnp.dot(a, b, preferred_element_type=F32)


def _dot_nt(a, b):
    return lax.dot_general(a, b, (((1,), (1,)), ((), ())), preferred_element_type=F32)


def _log2(n):
    assert n & (n - 1) == 0, "power of two expected"
    return n.bit_length() - 1


def _lane_group(shape, width):
    return lax.broadcasted_iota(jnp.int32, shape, len(shape) - 1) >> _log2(width)


def _lane_mask(shape, lo, hi, dtype):
    lane = lax.broadcasted_iota(jnp.int32, shape, len(shape) - 1)
    return jnp.where((lane >= lo) & (lane < hi), 1.0, 0.0).astype(dtype)


def _group_mean_sq(xf, group):
    n = xf.shape[-1]
    r = _lane_group((n, n), group)
    c = lax.broadcasted_iota(jnp.int32, (n, n), 0) >> _log2(group)
    pm = jnp.where(r == c, 1.0 / group, 0.0).astype(BF)
    return _dot((xf * xf).astype(BF), pm)


def _sigmoid(x):
    return 1.0 / (1.0 + jnp.exp(-x))


def _silu(x):
    return x * _sigmoid(x)


def _gelu_tanh(x):
    return 0.5 * x * (1.0 + jnp.tanh(math.sqrt(2.0 / math.pi) * (x + 0.044715 * (x * x * x))))


def _rmsnorm_rows(x, g):
    ms = jnp.mean(x * x, axis=-1, keepdims=True)
    return x * lax.rsqrt(ms + EPS) * g


PROJ_LOOKAHEAD = 3
P_SLABS = (S_AQ, S_AK, S_AV, S_AZ, S_BZ, S_CX, S_CZ)
P_POS = {s: i for i, s in enumerate(P_SLABS)}
P_COLS = len(P_SLABS) * BRANCH_W
PROJ_ORDER = (S_MQ, S_MZ, S_DU, S_DV, S_DZ, S_BQ, S_BK, S_BV) + P_SLABS


def _proj_kernel(tiles_per_seq, x_ref, ng_ref, w_ref, gains_ref, ws_ref, sgb_ref, mk_ref, mv_ref,
                 p_ref, yd_ref, ym_ref, qt_ref, k4_ref, vt_ref):
    tm = x_ref.shape[0]
    t0 = (pl.program_id(0) % tiles_per_seq) * tm
    h = _rmsnorm_rows(x_ref[...], ng_ref[...]).astype(BF)

    def gnorm(r, group, row):
        return r * lax.rsqrt(_group_mean_sq(r, group) + EPS) * gains_ref[row:row + 1, :]

    def slab(j):
        return _dot(h, w_ref[:, j * BRANCH_W:(j + 1) * BRANCH_W])

    kept = {}
    pending = [slab(j) for j in PROJ_ORDER[:PROJ_LOOKAHEAD]]
    for idx, j in enumerate(PROJ_ORDER):
        if idx + PROJ_LOOKAHEAD < N_SLABS:
            pending.append(slab(PROJ_ORDER[idx + PROJ_LOOKAHEAD]))
        r = pending.pop(0)
        if j == S_AQ:
            r = gnorm(r, NA_HD, G_AQ)
        elif j == S_AK:
            r = gnorm(r, NA_HD, G_AK)
        elif j == S_BQ:
            r = gnorm(r, DA_QK, G_BQ)
        elif j == S_BK:
            r = gnorm(r, DA_QK, G_BK)
        elif j == S_MQ:
            r = gnorm(r, MEM_HD, G_MQ)
        elif j in (S_AZ, S_BZ, S_CZ, S_DZ, S_MZ):
            r = _silu(r)
        elif j == S_DU:
            r = _gelu_tanh(r)
        elif j == S_DV:
            v = _gelu_tanh(r)
            mu = jnp.mean(v, axis=-1, keepdims=True)
            vc = v - mu
            var = jnp.mean(vc * vc, axis=-1, keepdims=True)
            r = vc * lax.rsqrt(var + EPS) * gains_ref[G_LNG:G_LNG + 1, :] + gains_ref[G_LNB:G_LNB + 1, :]
        if j in P_POS:
            p_ref[:, P_POS[j] * BRANCH_W:(P_POS[j] + 1) * BRANCH_W] = r.astype(BF)
        elif j == S_MZ:
            ym_ref[...] = _mem_attend(kept.pop(S_MQ), r, mk_ref[0], mv_ref[0]).astype(BF)
        elif j == S_DZ:
            _spatial_gate(kept.pop(S_DU), kept.pop(S_DV), r, ws_ref, sgb_ref, yd_ref)
        elif j == S_BV:
            _da_layouts(kept.pop(S_BQ), kept.pop(S_BK), r.astype(BF), t0, qt_ref, k4_ref, vt_ref)
        elif j in (S_MQ, S_DV, S_BQ, S_BK):
            kept[j] = r.astype(BF)
        else:
            kept[j] = r


def _proj(x2, norm_g, w_slab, layer, gains, ws, sg_bias, mk, mv, seq_len, tm=512):
    n = x2.shape[0]
    b = n // seq_len
    tps = seq_len // tm
    m = mk.shape[1]
    return pl.pallas_call(
        functools.partial(_proj_kernel, tps),
        grid=(n // tm,),
        in_specs=[
            pl.BlockSpec((tm, D_MODEL), lambda i: (i, 0)),
            _resident((1, D_MODEL), lambda i: (0, 0)),
            _layer_resident((D_MODEL, SLAB_COLS), layer),
            _resident((8, BRANCH_W), lambda i: (0, 0)),
            _layer_resident((SG_GROUPS, SG_CHUNK, SG_CHUNK), layer),
            _resident((SG_CHUNK, BRANCH_W), lambda i: (0, 0)),
            pl.BlockSpec((1, m, BRANCH_W), lambda i: (i // tps, 0, 0)),
            pl.BlockSpec((1, m, BRANCH_W), lambda i: (i // tps, 0, 0)),
        ],
        out_specs=[
            pl.BlockSpec((tm, P_COLS), lambda i: (i, 0)),
            pl.BlockSpec((tm, BRANCH_W), lambda i: (i, 0)),
            pl.BlockSpec((tm, BRANCH_W), lambda i: (i, 0)),
            pl.BlockSpec((1, DA_HEADS, DA_V, tm), lambda i: (i // tps, 0, 0, i % tps)),
            pl.BlockSpec((1, DA_HEADS, tm, DA_LANES), lambda i: (i // tps, 0, i % tps, 0)),
            pl.BlockSpec((1, DA_HEADS, DA_VT_ROWS, tm), lambda i: (i // tps, 0, 0, i % tps)),
        ],
        out_shape=[
            jax.ShapeDtypeStruct((n, P_COLS), BF),
            jax.ShapeDtypeStruct((n, BRANCH_W), BF),
            jax.ShapeDtypeStruct((n, BRANCH_W), BF),
            jax.ShapeDtypeStruct((b, DA_HEADS, DA_V, seq_len), BF),
            jax.ShapeDtypeStruct((b, DA_HEADS, seq_len, DA_LANES), BF),
            jax.ShapeDtypeStruct((b, DA_HEADS, DA_VT_ROWS, seq_len), BF),
        ],
        compiler_params=_cparams("parallel"),
        name="proj",
    )(x2, norm_g, w_slab, gains, ws, sg_bias, mk, mv)


def _memkv_kernel(mem_ref, g_ref, w_ref, kg_ref, k_ref, v_ref):
    h = _rmsnorm_rows(mem_ref[0], g_ref[...]).astype(BF)
    k = _dot(h, w_ref[:, :BRANCH_W])
    v = _dot(h, w_ref[:, BRANCH_W:])
    k = k * lax.rsqrt(_group_mean_sq(k, MEM_HD) + EPS) * kg_ref[...]
    k_ref[0] = k.astype(BF)
    v_ref[0] = v.astype(BF)


def _memkv(mem, m_norm_g, m_wkv, layer, kg):
    b, m, _ = mem.shape
    return pl.pallas_call(
        _memkv_kernel,
        grid=(b,),
        in_specs=[
            pl.BlockSpec((1, m, D_MODEL), lambda i: (i, 0, 0)),
            _resident((1, D_MODEL), lambda i: (0, 0)),
            _layer_resident((D_MODEL, 2 * BRANCH_W), layer),
            _resident((1, BRANCH_W), lambda i: (0, 0)),
        ],
        out_specs=[pl.BlockSpec((1, m, BRANCH_W), lambda i: (i, 0, 0)),
                   pl.BlockSpec((1, m, BRANCH_W), lambda i: (i, 0, 0))],
        out_shape=[jax.ShapeDtypeStruct((b, m, BRANCH_W), BF)] * 2,
        compiler_params=_cparams("parallel"),
        name="memkv",
    )(mem, m_norm_g, m_wkv, kg)


NA_ROWS_PER_STEP = 8
NA_WIN = NA_KH * GRID_W
NA_LOOKAHEAD = 3


def _na_kernel(q_ref, k_ref, v_ref, z_ref, bias_ref, o_ref):
    n_rows = k_ref.shape[1] // GRID_W
    rb = pl.program_id(1)
    head_of_lane = _lane_group((GRID_W, BRANCH_W), NA_HD)
    masks = [_lane_mask((GRID_W, BRANCH_W), h * NA_HD, (h + 1) * NA_HD, BF) for h in range(NA_HEADS)]

    def window(rr):
        r = rb * NA_ROWS_PER_STEP + rr
        rs = jnp.clip(r - NA_KH // 2, 0, n_rows - NA_KH)
        cls = jnp.where(r < NA_KH // 2, r, jnp.where(r > n_rows - NA_KH // 2, r - (n_rows - NA_KH), NA_KH // 2))
        return pl.multiple_of(rs * GRID_W, GRID_W), cls

    def scores(rr):
        w0, cls = window(rr)
        q = q_ref[0, rr * GRID_W:(rr + 1) * GRID_W, :]
        qs = jnp.concatenate([q * masks[h] for h in range(NA_HEADS)], axis=0)
        return _dot_nt(qs, k_ref[0, pl.ds(w0, NA_WIN), :]) + bias_ref[cls]

    pending = [scores(rr) for rr in range(NA_LOOKAHEAD)]
    for rr in range(NA_ROWS_PER_STEP):
        if rr + NA_LOOKAHEAD < NA_ROWS_PER_STEP:
            pending.append(scores(rr + NA_LOOKAHEAD))
        s = pending.pop(0)
        w0, _ = window(rr)
        m = jnp.max(s, axis=-1, keepdims=True)
        p = jnp.exp2(s - m)
        l = jnp.sum(p, axis=-1, keepdims=True)
        pv = _dot(p.astype(BF), v_ref[0, pl.ds(w0, NA_WIN), :]) / l
        o = jnp.zeros((GRID_W, BRANCH_W), F32)
        for h in range(NA_HEADS):
            o = jnp.where(head_of_lane == h, pv[h * GRID_W:(h + 1) * GRID_W], o)
        rows = slice(rr * GRID_W, (rr + 1) * GRID_W)
        o_ref[0, rows, :] = (o * z_ref[0, rows, :].astype(F32)).astype(BF)


def _na(p3, bias):
    b, t, _ = p3.shape
    tq = NA_ROWS_PER_STEP * GRID_W
    return pl.pallas_call(
        _na_kernel,
        grid=(b, t // tq),
        in_specs=[
            pl.BlockSpec((1, tq, BRANCH_W), lambda i, j: (i, j, P_POS[S_AQ])),
            pl.BlockSpec((1, t, BRANCH_W), lambda i, j: (i, 0, P_POS[S_AK])),
            pl.BlockSpec((1, t, BRANCH_W), lambda i, j: (i, 0, P_POS[S_AV])),
            pl.BlockSpec((1, tq, BRANCH_W), lambda i, j: (i, j, P_POS[S_AZ])),
            _resident(bias.shape, lambda i, j: (0, 0, 0)),
        ],
        out_specs=pl.BlockSpec((1, tq, BRANCH_W), lambda i, j: (i, j, 0)),
        out_shape=jax.ShapeDtypeStruct((b, t, BRANCH_W), BF),
        compiler_params=_cparams("parallel", "arbitrary"),
        name="na",
    )(p3, p3, p3, p3, bias)


NA_REL_ROWS = 2 * NA_KH - 1
NA_REL_COLS = 2 * NA_KW - 1
NA_CLASSES = NA_KH


def _na_class_roff(n_rows):
    half = NA_KH // 2
    rows = np.array(list(range(half)) + [half] + [n_rows - half + 1 + i for i in range(half - 1)])
    rs = np.clip(rows - half, 0, n_rows - NA_KH)
    return [int(v) for v in rs - rows + NA_KH - 1]


def _nabias_kernel(roff0, rpb_ref, o_ref, f_scr):
    h = pl.program_id(0)
    shape = (GRID_W, 2 * GRID_W)
    q = lax.broadcasted_iota(jnp.int32, shape, 0)
    lane = lax.broadcasted_iota(jnp.int32, shape, 1)
    k = lane & (GRID_W - 1)
    coff = k - q + (NA_KW - 1)
    cs = jnp.clip(q - NA_KW // 2, 0, GRID_W - NA_KW)
    valid = (k >= cs) & (k < cs + NA_KW)
    first = lane < GRID_W

    def tile(r, carry):
        base = (h * NA_REL_ROWS + r) * NA_REL_COLS
        acc = jnp.zeros(shape, F32)
        for s in range(NA_REL_COLS):
            acc = jnp.where(coff == s, jnp.where(first, rpb_ref[base + s], rpb_ref[base + NA_REL_COLS + s]), acc)
        f_scr[r] = jnp.where(valid, acc * LOG2E, NEG_BIG)
        return carry

    lax.fori_loop(0, NA_REL_ROWS - 1, tile, 0)
    for c in range(NA_CLASSES):
        for p in range(NA_KH // 2):
            o_ref[c, 0, :, p * 2 * GRID_W:(p + 1) * 2 * GRID_W] = f_scr[roff0[c] + 2 * p]


def _na_bias_table(rpb, n_rows):
    return pl.pallas_call(
        functools.partial(_nabias_kernel, _na_class_roff(n_rows)),
        grid=(NA_HEADS,),
        in_specs=[pl.BlockSpec(memory_space=pltpu.SMEM)],
        out_specs=pl.BlockSpec((NA_CLASSES, 1, GRID_W, NA_WIN), lambda h: (0, h, 0, 0)),
        out_shape=jax.ShapeDtypeStruct((NA_CLASSES, NA_HEADS, GRID_W, NA_WIN), F32),
        scratch_shapes=[pltpu.VMEM((NA_REL_ROWS - 1, GRID_W, 2 * GRID_W), F32)],
        compiler_params=_cparams("arbitrary"),
        name="nabias",
    )(rpb.astype(F32).reshape(-1)).reshape(NA_CLASSES, NA_HEADS * GRID_W, NA_WIN)


DA_VT_ROWS = 80
X_SUM = DA_V


def _da_layouts(q, k, v, t0, qt_ref, k4_ref, vt_ref):
    tt = q.shape[0]
    src = lax.broadcasted_iota(jnp.int32, (BRANCH_W, DA_HEADS * DA_LANES), 0)
    dst = lax.broadcasted_iota(jnp.int32, (BRANCH_W, DA_HEADS * DA_LANES), 1)
    dst_head, dst_lane = dst >> _log2(DA_LANES), dst & (DA_LANES - 1)
    spread = jnp.where((src == dst_head * DA_V + dst_lane) & (dst_lane < DA_V), 1.0, 0.0).astype(BF)
    k_all = _dot(k, spread)
    src_t = lax.broadcasted_iota(jnp.int32, (DA_LANES, BRANCH_W), 1)
    dst_t = lax.broadcasted_iota(jnp.int32, (DA_LANES, BRANCH_W), 0)
    klane = lax.broadcasted_iota(jnp.int32, (tt, DA_LANES), 1)
    kpos = t0 + lax.broadcasted_iota(jnp.int32, (tt, DA_LANES), 0)
    vrow = lax.broadcasted_iota(jnp.int32, (DA_VT_ROWS, tt), 0)
    for h in range(DA_HEADS):
        sel_t = jnp.where((src_t == h * DA_V + dst_t) & (dst_t < DA_V), 1.0, 0.0).astype(BF)
        qt_ref[0, h] = _dot_nt(sel_t[:DA_V], q).astype(BF)
        vt = _dot_nt(sel_t[:DA_VT_ROWS], v)
        vt_ref[0, h] = jnp.where(vrow == X_SUM, 1.0, vt).astype(BF)
        slope = ALIBI_SLOPES[h]
        k4 = k_all[:, h * DA_LANES:(h + 1) * DA_LANES]
        jh, jl = _alibi_parts(kpos, slope)
        k4 = jnp.where((klane >= X_KHI) & (klane < X_KLO), jh, k4)
        k4 = jnp.where((klane >= X_KLO) & (klane < X_QHI), jl, k4)
        k4 = jnp.where((klane >= X_QHI) & (klane < X_END), _piece_of(klane), k4)
        k4_ref[0, h] = k4.astype(BF)


DA_QCOLS = 256
DA_KROWS = 256
DA_LOOKAHEAD = 6
Q_LEFT, Q_DIAG, Q_RIGHT = range(3)


def _da_kernel(lam_init, qt_ref, k_ref, vt_ref, z_ref, lamv_ref, subg_ref, o_ref,
               qv_scr, dist_scr, spre_scr, m_scr, acc_scr):
    bq = qt_ref.shape[3]
    bk = bq
    nk = k_ref.shape[2] // bk
    qi = pl.program_id(1)
    frow = lax.broadcasted_iota(jnp.int32, (DA_V, bq), 0)
    xrow = DA_V + lax.broadcasted_iota(jnp.int32, (X_ROWS, bq), 0)
    qpos = qi * bq + lax.broadcasted_iota(jnp.int32, (X_ROWS, bq), 1)

    lv = lamv_ref[...]
    lam = (jnp.exp(jnp.sum(lv[0:1] * lv[1:2], axis=-1, keepdims=True))
           - jnp.exp(jnp.sum(lv[2:3] * lv[3:4], axis=-1, keepdims=True)) + lam_init)

    cols = [slice(c * DA_QCOLS, (c + 1) * DA_QCOLS) for c in range(bq // DA_QCOLS)]
    chains = [(h, m, c) for h in range(DA_HEADS) for m in range(2) for c in range(len(cols))]

    def tiles_of(blk):
        return [(blk, kc * DA_KROWS, ch) for kc in range(bk // DA_KROWS) for ch in chains]

    def scores(tile):
        (k0, side, diag), r0, (h, m, c) = tile
        if diag:
            c0 = c * DA_QCOLS
            side = Q_DIAG if r0 == c0 else (Q_LEFT if r0 < c0 else Q_RIGHT)
        s = _dot(k_ref[0, h, pl.ds(k0 + r0, DA_KROWS), :], qv_scr[h, side, m, :, cols[c]])
        if diag and r0 == c0:
            s = s - dist_scr[...] * (ALIBI_SLOPES[h] * LOG2E)
        return s

    def run_block(blk, nxt):
        tiles = tiles_of(blk)
        pending = []
        for i, ((k0, _, _), r0, (h, m, c)) in enumerate(tiles):
            j = i + DA_LOOKAHEAD
            if j < len(tiles):
                pending.append(scores(tiles[j]))
            elif nxt is not None:
                spre_scr[j - len(tiles)] = scores(tiles_of(nxt)[j - len(tiles)])
            s = spre_scr[i] if i < DA_LOOKAHEAD else pending.pop(0)
            m_old = m_scr[h, m, :, cols[c]]
            m_new = jnp.maximum(m_old, jnp.max(s, axis=0, keepdims=True))
            alpha = jnp.exp2(m_old - m_new)
            p = jnp.exp2(s - m_new).astype(BF)
            pv = _dot(vt_ref[0, h, :, pl.ds(k0 + r0, DA_KROWS)], p)
            acc_scr[h, m, :, cols[c]] = alpha * acc_scr[h, m, :, cols[c]] + pv
            m_scr[h, m, :, cols[c]] = m_new

    assert DA_KROWS == DA_QCOLS, "diagonal tiles are taken to be square"
    rel = (lax.broadcasted_iota(jnp.int32, (DA_KROWS, DA_QCOLS), 0)
           - lax.broadcasted_iota(jnp.int32, (DA_KROWS, DA_QCOLS), 1))
    dist_scr[...] = jnp.abs(rel).astype(F32)
    pieces = _piece_of(xrow)
    tail = jnp.zeros((DA_LANES - DA_V - X_ROWS, bq), BF)
    for h in range(DA_HEADS):
        slope = ALIBI_SLOPES[h]
        ih, il = _alibi_parts(qpos, slope)
        ex =jnp.where(xrow < X_QHI, pieces,
                       jnp.where(xrow < X_QLO, -ih, jnp.where(xrow < X_END, -il, 0.0))).astype(BF)
        for m in range(2):
            qm = qt_ref[0, h] * jnp.where(frow >> _log2(DA_QK) == m, 1.0, 0.0).astype(BF)
            for side, extra in ((Q_LEFT, ex), (Q_DIAG, jnp.zeros_like(ex)), (Q_RIGHT, -ex)):
                qv_scr[h, side, m, :DA_V, :] = qm
                qv_scr[h, side, m, DA_V:DA_V + X_ROWS, :] = extra
                qv_scr[h, side, m, DA_V + X_ROWS:, :] = tail
    m_scr[...] = jnp.full(m_scr.shape, NEG_BIG, F32)
    acc_scr[...] = jnp.zeros(acc_scr.shape, F32)

    def off_diagonal(t):
        kb = jnp.where(t >= qi, t + 1, t)
        return pl.multiple_of(kb * bk, bk), jnp.where(t >= qi, Q_RIGHT, Q_LEFT), False

    diagonal = (pl.multiple_of(qi * bk, bk), Q_DIAG, True)

    for i, tile in enumerate(tiles_of(off_diagonal(0))[:DA_LOOKAHEAD]):
        spre_scr[i] = scores(tile)

    def body(t, carry):
        run_block(off_diagonal(t), off_diagonal(t + 1))
        return carry

    lax.fori_loop(0, nk - 2, body, 0)
    run_block(off_diagonal(nk - 2), diagonal)
    run_block(diagonal, None)

    heads = []
    for h in range(DA_HEADS):
        o = [acc_scr[h, m, :DA_V, :] / acc_scr[h, m, X_SUM:X_SUM + 1, :] for m in range(2)]
        heads.append(o[0] - lam * o[1])

    full = jnp.concatenate(heads, axis=0).T
    y = full * lax.rsqrt(_group_mean_sq(full, DA_V) + EPS) * subg_ref[...]
    o_ref[0] = (y * z_ref[0].astype(F32)).astype(BF)


def _da(qt4, k4, vt4, p3, lamv, subg, lam_init, bq=512):
    b, _, t, _ = k4.shape
    assert all(math.frexp(s)[0] == 0.5 for s in ALIBI_SLOPES), "ALiBi slopes must be powers of two"
    assert t <= POS_SPLIT * 256 and POS_SPLIT <= 256
    return pl.pallas_call(
        functools.partial(_da_kernel, lam_init),
        grid=(b, t // bq),
        in_specs=[
            pl.BlockSpec((1, DA_HEADS, DA_V, bq), lambda i, j: (i, 0, 0, j)),
            pl.BlockSpec((1, DA_HEADS, t, DA_LANES), lambda i, j: (i, 0, 0, 0)),
            pl.BlockSpec((1, DA_HEADS, DA_VT_ROWS, t), lambda i, j: (i, 0, 0, 0)),
            pl.BlockSpec((1, bq, BRANCH_W), lambda i, j: (i, j, P_POS[S_BZ])),
            _resident((4, DA_QK), lambda i, j: (0, 0)),
            _resident((1, BRANCH_W), lambda i, j: (0, 0)),
        ],
        out_specs=pl.BlockSpec((1, bq, BRANCH_W), lambda i, j: (i, j, 0)),
        out_shape=jax.ShapeDtypeStruct((b, t, BRANCH_W), BF),
        scratch_shapes=[
            pltpu.VMEM((DA_HEADS, 3, 2, DA_LANES, bq), BF),
            pltpu.VMEM((DA_KROWS, DA_QCOLS), F32),
            pltpu.VMEM((DA_LOOKAHEAD, DA_KROWS, DA_QCOLS), F32),
            pltpu.VMEM((DA_HEADS, 2, 1, bq), F32),
            pltpu.VMEM((DA_HEADS, 2, DA_VT_ROWS, bq), F32),
        ],
        compiler_params=_cparams("parallel", "arbitrary"),
        name="da",
    )(qt4, k4, vt4, p3, lamv, subg)


POOL_HALO = 16


def _pool_kernel(x_ref, z_ref, band_ref, cw_ref, cs_ref, o_ref):
    tt = z_ref.shape[1]
    t_total = x_ref.shape[1]
    nt = t_total // tt
    ti = pl.program_id(1)
    t0 = pl.multiple_of(ti * tt, tt)
    xm = x_ref[0, pl.ds(t0, tt), :]
    p0 = pl.multiple_of(jnp.maximum(t0 - POOL_HALO, 0), POOL_HALO)
    n0 = pl.multiple_of(jnp.minimum(t0 + tt, t_total - POOL_HALO), POOL_HALO)
    has_prev = (ti > 0).astype(F32)
    has_next = (ti < nt - 1).astype(F32)
    prev = (x_ref[0, pl.ds(p0, POOL_HALO), :].astype(F32) * has_prev).astype(BF)
    nxt = (x_ref[0, pl.ds(n0, POOL_HALO), :].astype(F32) * has_next).astype(BF)
    xcat = jnp.concatenate([prev, xm, nxt], axis=0)

    sub = band_ref.shape[1]
    group = _lane_group((sub, BRANCH_W), POOL_GC)
    half = jnp.left_shift(1, group)
    sums = [[_dot(band_ref[g], xcat[r0:r0 + sub + 2 * POOL_HALO]) for g in range(len(POOL_WINDOWS))]
            for r0 in range(0, tt, sub)]
    for i, r0 in enumerate(range(0, tt, sub)):
        wsum = jnp.zeros((sub, BRANCH_W), F32)
        for g in range(len(POOL_WINDOWS)):
            wsum = jnp.where(group == g, sums[i][g], wsum)
        tpos = t0 + r0 + lax.broadcasted_iota(jnp.int32, (sub, BRANCH_W), 0)
        lo = jnp.maximum(tpos - half, 0)
        hi = jnp.minimum(tpos + half - 1, t_total - 1)
        cnt = (hi - lo + 1).astype(F32)
        dlt = wsum / cnt - xm[r0:r0 + sub].astype(F32)
        y = _dot(dlt.astype(BF), cw_ref[...]) * cs_ref[...]
        o_ref[0, r0:r0 + sub, :] = (y * z_ref[0, r0:r0 + sub, :].astype(F32)).astype(BF)


def _pool_band(tt):
    band = np.zeros((len(POOL_WINDOWS), tt, tt + 2 * POOL_HALO), np.float32)
    t = np.arange(tt)[:, None]
    u = np.arange(tt + 2 * POOL_HALO)[None, :] - POOL_HALO
    for g, w in enumerate(POOL_WINDOWS):
        band[g] = ((u >= t - w // 2) & (u <= t - w // 2 + w - 1)).astype(np.float32)
    return jnp.asarray(band, BF)


def _pool(p3, cw_bd, c_scale, tt=512, sub=256):
    b, t, _ = p3.shape
    band = _pool_band(sub)
    return pl.pallas_call(
        _pool_kernel,
        grid=(b, t // tt),
        in_specs=[
            pl.BlockSpec((1, t, BRANCH_W), lambda i, j: (i, 0, P_POS[S_CX])),
            pl.BlockSpec((1, tt, BRANCH_W), lambda i, j: (i, j, P_POS[S_CZ])),
            _resident(band.shape, lambda i, j: (0, 0, 0)),
            _resident((BRANCH_W, BRANCH_W), lambda i, j: (0, 0)),
            _resident((1, BRANCH_W), lambda i, j: (0, 0)),
        ],
        out_specs=pl.BlockSpec((1, tt, BRANCH_W), lambda i, j: (i, j, 0)),
        out_shape=jax.ShapeDtypeStruct((b, t, BRANCH_W), BF),
        compiler_params=_cparams("parallel", "arbitrary"),
        name="pool",
    )(p3, p3, band, cw_bd, c_scale)


def _spatial_gate(u, vn, z, ws_ref, bias_ref, o_ref):
    group = _lane_group((SG_CHUNK, BRANCH_W), SG_GC)
    for c in range(u.shape[0] // SG_CHUNK):
        rows = slice(c * SG_CHUNK, (c + 1) * SG_CHUNK)
        mix = jnp.zeros((SG_CHUNK, BRANCH_W), F32)
        for g in range(SG_GROUPS):
            mix = jnp.where(group == g, _dot(ws_ref[g], vn[rows]), mix)
        o_ref[rows, :] = (u[rows] * (mix + bias_ref[...]) * z[rows]).astype(BF)


def _mem_attend(q, z, k, v):
    head_of_lane = _lane_group(q.shape, MEM_HD)
    scores = [_dot_nt(q * _lane_mask(q.shape, h * MEM_HD, (h + 1) * MEM_HD, BF), k) for h in range(MEM_HEADS)]
    o = jnp.zeros(q.shape, F32)
    for h in range(MEM_HEADS):
        s = scores[h]
        m = jnp.max(s, axis=-1, keepdims=True)
        p = jnp.exp2(s - m)
        l = jnp.sum(p, axis=-1, keepdims=True)
        oh = _dot(p.astype(BF), v) / l
        o = jnp.where(head_of_lane == h, oh, o)
    return o * z


def _final_kernel(x_ref, ng_ref, ya_ref, yb_ref, yc_ref, yd_ref, ym_ref,
                  wg_ref, bg_ref, wb_ref, wo_ref, o_ref):
    x = x_ref[...]
    h = _rmsnorm_rows(x, ng_ref[...]).astype(BF)
    merged = jnp.zeros(x.shape, F32)
    for i, y_ref in enumerate((ya_ref, yb_ref, yc_ref, yd_ref, ym_ref)):
        logit = _dot(h, wg_ref[:, i * D_MODEL:(i + 1) * D_MODEL]) + bg_ref[i:i + 1, :]
        merged = merged + _sigmoid(logit) * _dot(y_ref[...], wb_ref[i])
    o_ref[...] = x + _dot(merged.astype(BF), wo_ref[...])


def _final(x2, norm_g, ys, w_gate, b_gate, w_branch, w_out, layer, tm=512):
    n = x2.shape[0]
    yspec = pl.BlockSpec((tm, BRANCH_W), lambda i: (i, 0))
    return pl.pallas_call(
        _final_kernel,
        grid=(n // tm,),
        in_specs=[
            pl.BlockSpec((tm, D_MODEL), lambda i: (i, 0)),
            _resident((1, D_MODEL), lambda i: (0, 0)),
            yspec, yspec, yspec, yspec, yspec,
            _layer_resident((D_MODEL, N_BRANCH * D_MODEL), layer),
            _resident((8, D_MODEL), lambda i: (0, 0)),
            _layer_resident((N_BRANCH, BRANCH_W, D_MODEL), layer),
            _layer_resident((D_MODEL, D_MODEL), layer),
        ],
        out_specs=pl.BlockSpec((tm, D_MODEL), lambda i: (i, 0)),
        out_shape=jax.ShapeDtypeStruct((n, D_MODEL), F32),
        compiler_params=_cparams("parallel"),
        name="final",
    )(x2, norm_g, *ys, w_gate, b_gate, w_branch, w_out)


W_IN_CAST_ROWS = 128


def _cast_w_in_kernel(w_ref, slab_ref, gate_ref):
    w = w_ref[0]
    slab_ref[0] = w[:, :SLAB_COLS].astype(BF)
    gate_ref[0] = w[:, SLAB_COLS:].astype(BF)


def _cast_w_in(w_in):
    depth, d, cols = w_in.shape
    gate_cols = cols - SLAB_COLS
    return pl.pallas_call(
        _cast_w_in_kernel,
        grid=(depth, d // W_IN_CAST_ROWS),
        in_specs=[pl.BlockSpec((1, W_IN_CAST_ROWS, cols), lambda l, r: (l, r, 0))],
        out_specs=[pl.BlockSpec((1, W_IN_CAST_ROWS, SLAB_COLS), lambda l, r: (l, r, 0)),
                   pl.BlockSpec((1, W_IN_CAST_ROWS, gate_cols), lambda l, r: (l, r, 0))],
        out_shape=[jax.ShapeDtypeStruct((depth, d, SLAB_COLS), BF),
                   jax.ShapeDtypeStruct((depth, d, gate_cols), BF)],
        compiler_params=_cparams("parallel", "parallel"),
        name="cast_w_in",
    )(w_in)


def _tile4(g):
    return jnp.tile(g.astype(F32), BRANCH_W // g.shape[0])


def _layer(x, mem, layer_idx, stacked, norm_g, b_gate, a_qn_g, a_kn_g, a_rpb,
           b_qn_g, b_kn_g, b_lam_q1, b_lam_k1, b_lam_q2, b_lam_k2, b_sub_g,
           c_w, c_scale, d_ln_g, d_ln_b, d_bs, m_norm_g, m_qn_g, m_kn_g):
    b, t, d = x.shape
    n = b * t
    x2 = x.reshape(n, d)
    ng = norm_g.reshape(1, d).astype(F32)

    gains = jnp.zeros((8, BRANCH_W), F32)
    gains = gains.at[G_AQ].set(_tile4(a_qn_g) * (NA_HD ** -0.5 * LOG2E))
    gains = gains.at[G_AK].set(_tile4(a_kn_g))
    gains = gains.at[G_BQ].set(_tile4(b_qn_g) * (DA_QK ** -0.5 * LOG2E))
    gains = gains.at[G_BK].set(_tile4(b_kn_g))
    gains = gains.at[G_MQ].set(_tile4(m_qn_g) * (MEM_HD ** -0.5 * LOG2E))
    gains = gains.at[G_LNG].set(d_ln_g.astype(F32))
    gains = gains.at[G_LNB].set(d_ln_b.astype(F32))

    mk, mv = _memkv(mem, m_norm_g.reshape(1, d).astype(F32), stacked["m_wkv"], layer_idx,
                    _tile4(m_kn_g).reshape(1, BRANCH_W))
    sg_bias = jnp.repeat(d_bs.astype(F32).T, SG_GC, axis=1)
    p2, y_d, y_m, qt4, k4, vt4 = _proj(x2, ng, stacked["w_slab"], layer_idx, gains,
                                       stacked["d_ws"], sg_bias, mk, mv, t)
    p3 = p2.reshape(b, t, P_COLS)

    y_a = _na(p3, _na_bias_table(a_rpb, t // GRID_W))

    lam_init = 0.8 - 0.6 * math.exp(-0.3 * layer_idx)
    lamv = jnp.stack([b_lam_q1, b_lam_k1, b_lam_q2, b_lam_k2]).astype(F32)
    subg = (_tile4(b_sub_g) * (1.0 - lam_init)).reshape(1, BRANCH_W)
    y_b = _da(qt4, k4, vt4, p3, lamv, subg, lam_init)

    cw_bd = jnp.zeros((BRANCH_W, BRANCH_W), F32)
    for g in range(len(POOL_WINDOWS)):
        cw_bd = cw_bd.at[g * POOL_GC:(g + 1) * POOL_GC, g * POOL_GC:(g + 1) * POOL_GC].set(c_w[g].astype(F32))
    y_c = _pool(p3, cw_bd.astype(BF), c_scale.reshape(1, BRANCH_W).astype(F32))

    bg =jnp.zeros((8, d), F32).at[:N_BRANCH].set(b_gate.astype(F32))
    ys = [y.reshape(n, BRANCH_W) for y in (y_a, y_b, y_c, y_d, y_m)]
    out = _final(x2, ng, ys, stacked["w_gate"], bg, stacked["w_branch"], stacked["w_out"], layer_idx)
    return out.reshape(b, t, d)


def kernel(x, mem, norm_g, w_in, b_gate, a_qn_g, a_kn_g, a_rpb, b_qn_g, b_kn_g, b_lam_q1, b_lam_k1, b_lam_q2, b_lam_k2, b_sub_g, c_w, c_scale, d_ln_g, d_ln_b, d_ws, d_bs, m_norm_g, m_wkv, m_qn_g, m_kn_g, w_branch, w_out):
    w_slab, w_gate = _cast_w_in(w_in)
    stacked = {
        "w_slab": w_slab,
        "w_gate": w_gate,
        "w_branch": w_branch.astype(BF),
        "w_out": w_out.astype(BF),
        "m_wkv": m_wkv.astype(BF),
        "d_ws": d_ws.astype(BF),
    }
    for l in range(DEPTH):
        x = _layer(x, mem, l, stacked, norm_g[l], b_gate[l], a_qn_g[l], a_kn_g[l], a_rpb[l],
                   b_qn_g[l], b_kn_g[l], b_lam_q1[l], b_lam_k1[l], b_lam_q2[l], b_lam_k2[l], b_sub_g[l],
                   c_w[l], c_scale[l], d_ln_g[l], d_ln_b[l], d_bs[l], m_norm_g[l], m_qn_g[l], m_kn_g[l])
    return x
```

```python
import functools
import math

import numpy as np
import jax
import jax.numpy as jnp
from jax import lax
from jax.experimental import pallas as pl
from jax.experimental.pallas import tpu as pltpu

F32 = jnp.float32
BF = jnp.bfloat16

D_MODEL = 1024
DEPTH = 2
GRID_W = 64
BRANCH_W = 256
N_BRANCH = 5
NA_HEADS = 4
NA_HD = 64
NA_KH = 8
NA_KW = 16
DA_HEADS = 4
DA_QK = 32
DA_V = 64
ALIBI_BASE = 8.0
POOL_WINDOWS = (2, 4, 8, 16)
POOL_GC = 64
SG_CHUNK = 128
SG_GROUPS = 4
SG_GC = 64
MEM_HEADS = 4
MEM_HD = 64
N_SLABS = 15
SLAB_COLS = N_SLABS * BRANCH_W
EPS = 1e-6

(S_AQ, S_AK, S_AV, S_AZ, S_BQ, S_BK, S_BV, S_BZ, S_CX, S_CZ,
 S_DU, S_DV, S_DZ, S_MQ, S_MZ) = range(N_SLABS)

(G_AQ, G_AK, G_BQ, G_BK, G_MQ, G_LNG, G_LNB) = range(7)

VMEM_LIMIT_BYTES = 56 * 1024 * 1024
NEG_BIG = -1e30

ALIBI_SLOPES = tuple(2.0 ** (-ALIBI_BASE * (h + 1) / DA_HEADS) for h in range(DA_HEADS))
DA_LANES = 128


def _bf16_pieces(x, n):
    out = []
    for _ in range(n):
        piece = float(np.asarray(x, np.float32).astype(jnp.bfloat16).astype(np.float32))
        out.append(piece)
        x -= piece
    return tuple(out)


LOG2E = math.log2(math.e)
LOG2E_PIECES = _bf16_pieces(LOG2E, 3)
N_PIECES = len(LOG2E_PIECES)
X_KHI, X_KLO, X_QHI, X_QLO = (DA_V + g * N_PIECES for g in range(4))
X_END = DA_V + 4 * N_PIECES
X_ROWS = 16
POS_SPLIT = 64


def _alibi_parts(pos, slope):
    hi = (pos >> _log2(POS_SPLIT)).astype(F32) * (slope * POS_SPLIT)
    lo = (pos & (POS_SPLIT - 1)).astype(F32) * slope
    return hi, lo


def _piece_of(idx):
    out = jnp.zeros(idx.shape, F32)
    for g in range(4):
        for p, piece in enumerate(LOG2E_PIECES):
            out = jnp.where(idx == DA_V + g * N_PIECES + p, piece, out)
    return out


def _cparams(*sem):
    return pltpu.CompilerParams(dimension_semantics=sem, vmem_limit_bytes=VMEM_LIMIT_BYTES)


def _resident(shape, index_map):
    return pl.BlockSpec(shape, index_map, pipeline_mode=pl.Buffered(1))


def _layer_resident(shape, layer):
    return pl.BlockSpec((None,) + tuple(shape), lambda *_: (layer,) + (0,) * len(shape),
                        pipeline_mode=pl.Buffered(1))


def _dot(a, b):
    return jnp.dot(a, b, preferred_element_type=F32)


def _dot_nt(a, b):
    return lax.dot_general(a, b, (((1,), (1,)), ((), ())), preferred_element_type=F32)


def _log2(n):
    assert n & (n - 1) == 0, "power of two expected"
    return n.bit_length() - 1


def _lane_group(shape, width):
    return lax.broadcasted_iota(jnp.int32, shape, len(shape) - 1) >> _log2(width)


def _lane_mask(shape, lo, hi, dtype):
    lane = lax.broadcasted_iota(jnp.int32, shape, len(shape) - 1)
    return jnp.where((lane >= lo) & (lane < hi), 1.0, 0.0).astype(dtype)


def _group_mean_sq(xf, group):
    n = xf.shape[-1]
    r = _lane_group((n, n), group)
    c = lax.broadcasted_iota(jnp.int32, (n, n), 0) >> _log2(group)
    pm = jnp.where(r == c, 1.0 / group, 0.0).astype(BF)
    return _dot((xf * xf).astype(BF), pm)


def _sigmoid(x):
    return 1.0 / (1.0 + jnp.exp(-x))


def _silu(x):
    return x * _sigmoid(x)


def _gelu_tanh(x):
    return 0.5 * x * (1.0 + jnp.tanh(math.sqrt(2.0 / math.pi) * (x + 0.044715 * (x * x * x))))


def _rmsnorm_rows(x, g):
    ms = jnp.mean(x * x, axis=-1, keepdims=True)
    return x * lax.rsqrt(ms + EPS) * g


PROJ_LOOKAHEAD = 3
P_SLABS = (S_AQ, S_AK, S_AV, S_AZ, S_BZ, S_CX, S_CZ)
P_POS = {s: i for i, s in enumerate(P_SLABS)}
P_COLS = len(P_SLABS) * BRANCH_W
PROJ_ORDER = (S_MQ, S_MZ, S_DU, S_DV, S_DZ, S_BQ, S_BK, S_BV) + P_SLABS


def _proj_kernel(tiles_per_seq, x_ref, ng_ref, w_ref, gains_ref, ws_ref, sgb_ref, mk_ref, mv_ref,
                 p_ref, yd_ref, ym_ref, qt_ref, k4_ref, vt_ref):
    tm = x_ref.shape[0]
    t0 = (pl.program_id(0) % tiles_per_seq) * tm
    h = _rmsnorm_rows(x_ref[...], ng_ref[...]).astype(BF)

    def gnorm(r, group, row):
        return r * lax.rsqrt(_group_mean_sq(r, group) + EPS) * gains_ref[row:row + 1, :]

    def slab(j):
        return _dot(h, w_ref[:, j * BRANCH_W:(j + 1) * BRANCH_W])

    kept = {}
    pending = [slab(j) for j in PROJ_ORDER[:PROJ_LOOKAHEAD]]
    for idx, j in enumerate(PROJ_ORDER):
        if idx + PROJ_LOOKAHEAD < N_SLABS:
            pending.append(slab(PROJ_ORDER[idx + PROJ_LOOKAHEAD]))
        r = pending.pop(0)
        if j == S_AQ:
            r = gnorm(r, NA_HD, G_AQ)
        elif j == S_AK:
            r = gnorm(r, NA_HD, G_AK)
        elif j == S_BQ:
            r = gnorm(r, DA_QK, G_BQ)
        elif j == S_BK:
            r = gnorm(r, DA_QK, G_BK)
        elif j == S_MQ:
            r = gnorm(r, MEM_HD, G_MQ)
        elif j in (S_AZ, S_BZ, S_CZ, S_DZ, S_MZ):
            r = _silu(r)
        elif j == S_DU:
            r = _gelu_tanh(r)
        elif j == S_DV:
            v = _gelu_tanh(r)
            mu = jnp.mean(v, axis=-1, keepdims=True)
            vc = v - mu
            var = jnp.mean(vc * vc, axis=-1, keepdims=True)
            r = vc * lax.rsqrt(var + EPS) * gains_ref[G_LNG:G_LNG + 1, :] + gains_ref[G_LNB:G_LNB + 1, :]
        if j in P_POS:
            p_ref[:, P_POS[j] * BRANCH_W:(P_POS[j] + 1) * BRANCH_W] = r.astype(BF)
        elif j == S_MZ:
            ym_ref[...] = _mem_attend(kept.pop(S_MQ), r, mk_ref[0], mv_ref[0]).astype(BF)
        elif j == S_DZ:
            _spatial_gate(kept.pop(S_DU), kept.pop(S_DV), r, ws_ref, sgb_ref, yd_ref)
        elif j == S_BV:
            _da_layouts(kept.pop(S_BQ), kept.pop(S_BK), r.astype(BF), t0, qt_ref, k4_ref, vt_ref)
        elif j in (S_MQ, S_DV, S_BQ, S_BK):
            kept[j] = r.astype(BF)
        else:
            kept[j] = r


def _proj(x2, norm_g, w_slab, layer, gains, ws, sg_bias, mk, mv, seq_len, tm=512):
    n = x2.shape[0]
    b = n // seq_len
    tps = seq_len // tm
    m = mk.shape[1]
    return pl.pallas_call(
        functools.partial(_proj_kernel, tps),
        grid=(n // tm,),
        in_specs=[
            pl.BlockSpec((tm, D_MODEL), lambda i: (i, 0)),
            _resident((1, D_MODEL), lambda i: (0, 0)),
            _layer_resident((D_MODEL, SLAB_COLS), layer),
            _resident((8, BRANCH_W), lambda i: (0, 0)),
            _layer_resident((SG_GROUPS, SG_CHUNK, SG_CHUNK), layer),
            _resident((SG_CHUNK, BRANCH_W), lambda i: (0, 0)),
            pl.BlockSpec((1, m, BRANCH_W), lambda i: (i // tps, 0, 0)),
            pl.BlockSpec((1, m, BRANCH_W), lambda i: (i // tps, 0, 0)),
        ],
        out_specs=[
            pl.BlockSpec((tm, P_COLS), lambda i: (i, 0)),
            pl.BlockSpec((tm, BRANCH_W), lambda i: (i, 0)),
            pl.BlockSpec((tm, BRANCH_W), lambda i: (i, 0)),
            pl.BlockSpec((1, DA_HEADS, DA_V, tm), lambda i: (i // tps, 0, 0, i % tps)),
            pl.BlockSpec((1, DA_HEADS, tm, DA_LANES), lambda i: (i // tps, 0, i % tps, 0)),
            pl.BlockSpec((1, DA_HEADS, DA_VT_ROWS, tm), lambda i: (i // tps, 0, 0, i % tps)),
        ],
        out_shape=[
            jax.ShapeDtypeStruct((n, P_COLS), BF),
            jax.ShapeDtypeStruct((n, BRANCH_W), BF),
            jax.ShapeDtypeStruct((n, BRANCH_W), BF),
            jax.ShapeDtypeStruct((b, DA_HEADS, DA_V, seq_len), BF),
            jax.ShapeDtypeStruct((b, DA_HEADS, seq_len, DA_LANES), BF),
            jax.ShapeDtypeStruct((b, DA_HEADS, DA_VT_ROWS, seq_len), BF),
        ],
        compiler_params=_cparams("parallel"),
        name="proj",
    )(x2, norm_g, w_slab, gains, ws, sg_bias, mk, mv)


def _memkv_kernel(mem_ref, g_ref, w_ref, kg_ref, k_ref, v_ref):
    h = _rmsnorm_rows(mem_ref[0], g_ref[...]).astype(BF)
    k = _dot(h, w_ref[:, :BRANCH_W])
    v = _dot(h, w_ref[:, BRANCH_W:])
    k = k * lax.rsqrt(_group_mean_sq(k, MEM_HD) + EPS) * kg_ref[...]
    k_ref[0] = k.astype(BF)
    v_ref[0] = v.astype(BF)


def _memkv(mem, m_norm_g, m_wkv, layer, kg):
    b, m, _ = mem.shape
    return pl.pallas_call(
        _memkv_kernel,
        grid=(b,),
        in_specs=[
            pl.BlockSpec((1, m, D_MODEL), lambda i: (i, 0, 0)),
            _resident((1, D_MODEL), lambda i: (0, 0)),
            _layer_resident((D_MODEL, 2 * BRANCH_W), layer),
            _resident((1, BRANCH_W), lambda i: (0, 0)),
        ],
        out_specs=[pl.BlockSpec((1, m, BRANCH_W), lambda i: (i, 0, 0)),
                   pl.BlockSpec((1, m, BRANCH_W), lambda i: (i, 0, 0))],
        out_shape=[jax.ShapeDtypeStruct((b, m, BRANCH_W), BF)] * 2,
        compiler_params=_cparams("parallel"),
        name="memkv",
    )(mem, m_norm_g, m_wkv, kg)


NA_ROWS_PER_STEP = 8
NA_WIN = NA_KH * GRID_W
NA_LOOKAHEAD = 3


def _na_kernel(q_ref, k_ref, v_ref, z_ref, bias_ref, o_ref):
    n_rows = k_ref.shape[1] // GRID_W
    rb = pl.program_id(1)
    head_of_lane = _lane_group((GRID_W, BRANCH_W), NA_HD)
    masks = [_lane_mask((GRID_W, BRANCH_W), h * NA_HD, (h + 1) * NA_HD, BF) for h in range(NA_HEADS)]

    def window(rr):
        r = rb * NA_ROWS_PER_STEP + rr
        rs = jnp.clip(r - NA_KH // 2, 0, n_rows - NA_KH)
        cls = jnp.where(r < NA_KH // 2, r, jnp.where(r > n_rows - NA_KH // 2, r - (n_rows - NA_KH), NA_KH // 2))
        return pl.multiple_of(rs * GRID_W, GRID_W), cls

    def scores(rr):
        w0, cls = window(rr)
        q = q_ref[0, rr * GRID_W:(rr + 1) * GRID_W, :]
        qs = jnp.concatenate([q * masks[h] for h in range(NA_HEADS)], axis=0)
        return _dot_nt(qs, k_ref[0, pl.ds(w0, NA_WIN), :]) + bias_ref[cls]

    pending = [scores(rr) for rr in range(NA_LOOKAHEAD)]
    for rr in range(NA_ROWS_PER_STEP):
        if rr + NA_LOOKAHEAD < NA_ROWS_PER_STEP:
            pending.append(scores(rr + NA_LOOKAHEAD))
        s = pending.pop(0)
        w0, _ = window(rr)
        m = jnp.max(s, axis=-1, keepdims=True)
        p = jnp.exp2(s - m)
        l = jnp.sum(p, axis=-1, keepdims=True)
        pv = _dot(p.astype(BF), v_ref[0, pl.ds(w0, NA_WIN), :]) / l
        o = jnp.zeros((GRID_W, BRANCH_W), F32)
        for h in range(NA_HEADS):
            o = jnp.where(head_of_lane == h, pv[h * GRID_W:(h + 1) * GRID_W], o)
        rows = slice(rr * GRID_W, (rr + 1) * GRID_W)
        o_ref[0, rows, :] = (o * z_ref[0, rows, :].astype(F32)).astype(BF)


def _na(p3, bias):
    b, t, _ = p3.shape
    tq = NA_ROWS_PER_STEP * GRID_W
    return pl.pallas_call(
        _na_kernel,
        grid=(b, t // tq),
        in_specs=[
            pl.BlockSpec((1, tq, BRANCH_W), lambda i, j: (i, j, P_POS[S_AQ])),
            pl.BlockSpec((1, t, BRANCH_W), lambda i, j: (i, 0, P_POS[S_AK])),
            pl.BlockSpec((1, t, BRANCH_W), lambda i, j: (i, 0, P_POS[S_AV])),
            pl.BlockSpec((1, tq, BRANCH_W), lambda i, j: (i, j, P_POS[S_AZ])),
            _resident(bias.shape, lambda i, j: (0, 0, 0)),
        ],
        out_specs=pl.BlockSpec((1, tq, BRANCH_W), lambda i, j: (i, j, 0)),
        out_shape=jax.ShapeDtypeStruct((b, t, BRANCH_W), BF),
        compiler_params=_cparams("parallel", "arbitrary"),
        name="na",
    )(p3, p3, p3, p3, bias)


NA_REL_ROWS = 2 * NA_KH - 1
NA_REL_COLS = 2 * NA_KW - 1
NA_CLASSES = NA_KH


def _na_class_roff(n_rows):
    half = NA_KH // 2
    rows = np.array(list(range(half)) + [half] + [n_rows - half + 1 + i for i in range(half - 1)])
    rs = np.clip(rows - half, 0, n_rows - NA_KH)
    return [int(v) for v in rs - rows + NA_KH - 1]


def _nabias_kernel(roff0, rpb_ref, o_ref, f_scr):
    h = pl.program_id(0)
    shape = (GRID_W, 2 * GRID_W)
    q = lax.broadcasted_iota(jnp.int32, shape, 0)
    lane = lax.broadcasted_iota(jnp.int32, shape, 1)
    k = lane & (GRID_W - 1)
    coff = k - q + (NA_KW - 1)
    cs = jnp.clip(q - NA_KW // 2, 0, GRID_W - NA_KW)
    valid = (k >= cs) & (k < cs + NA_KW)
    first = lane < GRID_W

    def tile(r, carry):
        base = (h * NA_REL_ROWS + r) * NA_REL_COLS
        acc = jnp.zeros(shape, F32)
        for s in range(NA_REL_COLS):
            acc = jnp.where(coff == s, jnp.where(first, rpb_ref[base + s], rpb_ref[base + NA_REL_COLS + s]), acc)
        f_scr[r] = jnp.where(valid, acc * LOG2E, NEG_BIG)
        return carry

    lax.fori_loop(0, NA_REL_ROWS - 1, tile, 0)
    for c in range(NA_CLASSES):
        for p in range(NA_KH // 2):
            o_ref[c, 0, :, p * 2 * GRID_W:(p + 1) * 2 * GRID_W] = f_scr[roff0[c] + 2 * p]


def _na_bias_table(rpb, n_rows):
    return pl.pallas_call(
        functools.partial(_nabias_kernel, _na_class_roff(n_rows)),
        grid=(NA_HEADS,),
        in_specs=[pl.BlockSpec(memory_space=pltpu.SMEM)],
        out_specs=pl.BlockSpec((NA_CLASSES, 1, GRID_W, NA_WIN), lambda h: (0, h, 0, 0)),
        out_shape=jax.ShapeDtypeStruct((NA_CLASSES, NA_HEADS, GRID_W, NA_WIN), F32),
        scratch_shapes=[pltpu.VMEM((NA_REL_ROWS - 1, GRID_W, 2 * GRID_W), F32)],
        compiler_params=_cparams("arbitrary"),
        name="nabias",
    )(rpb.astype(F32).reshape(-1)).reshape(NA_CLASSES, NA_HEADS * GRID_W, NA_WIN)


DA_VT_ROWS = 80
X_SUM = DA_V


def _da_layouts(q, k, v, t0, qt_ref, k4_ref, vt_ref):
    tt = q.shape[0]
    src = lax.broadcasted_iota(jnp.int32, (BRANCH_W, DA_HEADS * DA_LANES), 0)
    dst = lax.broadcasted_iota(jnp.int32, (BRANCH_W, DA_HEADS * DA_LANES), 1)
    dst_head, dst_lane = dst >> _log2(DA_LANES), dst & (DA_LANES - 1)
    spread = jnp.where((src == dst_head * DA_V + dst_lane) & (dst_lane < DA_V), 1.0, 0.0).astype(BF)
    k_all = _dot(k, spread)
    src_t = lax.broadcasted_iota(jnp.int32, (DA_LANES, BRANCH_W), 1)
    dst_t = lax.broadcasted_iota(jnp.int32, (DA_LANES, BRANCH_W), 0)
    klane = lax.broadcasted_iota(jnp.int32, (tt, DA_LANES), 1)
    kpos = t0 + lax.broadcasted_iota(jnp.int32, (tt, DA_LANES), 0)
    vrow = lax.broadcasted_iota(jnp.int32, (DA_VT_ROWS, tt), 0)
    for h in range(DA_HEADS):
        sel_t = jnp.where((src_t == h * DA_V + dst_t) & (dst_t < DA_V), 1.0, 0.0).astype(BF)
        qt_ref[0, h] = _dot_nt(sel_t[:DA_V], q).astype(BF)
        vt = _dot_nt(sel_t[:DA_VT_ROWS], v)
        vt_ref[0, h] = jnp.where(vrow == X_SUM, 1.0, vt).astype(BF)
        slope = ALIBI_SLOPES[h]
        k4 = k_all[:, h * DA_LANES:(h + 1) * DA_LANES]
        jh, jl = _alibi_parts(kpos, slope)
        k4 = jnp.where((klane >= X_KHI) & (klane < X_KLO), jh, k4)
        k4 = jnp.where((klane >= X_KLO) & (klane < X_QHI), jl, k4)
        k4 = jnp.where((klane >= X_QHI) & (klane < X_END), _piece_of(klane), k4)
        k4_ref[0, h] = k4.astype(BF)


DA_QCOLS = 256
DA_KROWS = 256
DA_LOOKAHEAD = 6
Q_LEFT, Q_DIAG, Q_RIGHT = range(3)


def _da_kernel(lam_init, qt_ref, k_ref, vt_ref, z_ref, lamv_ref, subg_ref, o_ref,
               qv_scr, dist_scr, spre_scr, m_scr, acc_scr):
    bq = qt_ref.shape[3]
    bk = bq
    nk = k_ref.shape[2] // bk
    qi = pl.program_id(1)
    frow = lax.broadcasted_iota(jnp.int32, (DA_V, bq), 0)
    xrow = DA_V + lax.broadcasted_iota(jnp.int32, (X_ROWS, bq), 0)
    qpos = qi * bq + lax.broadcasted_iota(jnp.int32, (X_ROWS, bq), 1)

    lv = lamv_ref[...]
    lam = (jnp.exp(jnp.sum(lv[0:1] * lv[1:2], axis=-1, keepdims=True))
           - jnp.exp(jnp.sum(lv[2:3] * lv[3:4], axis=-1, keepdims=True)) + lam_init)

    cols = [slice(c * DA_QCOLS, (c + 1) * DA_QCOLS) for c in range(bq // DA_QCOLS)]
    chains = [(h, m, c) for h in range(DA_HEADS) for m in range(2) for c in range(len(cols))]

    def tiles_of(blk):
        return [(blk, kc * DA_KROWS, ch) for kc in range(bk // DA_KROWS) for ch in chains]

    def scores(tile):
        (k0, side, diag), r0, (h, m, c) = tile
        if diag:
            c0 = c * DA_QCOLS
            side = Q_DIAG if r0 == c0 else (Q_LEFT if r0 < c0 else Q_RIGHT)
        s = _dot(k_ref[0, h, pl.ds(k0 + r0, DA_KROWS), :], qv_scr[h, side, m, :, cols[c]])
        if diag and r0 == c0:
            s = s - dist_scr[...] * (ALIBI_SLOPES[h] * LOG2E)
        return s

    def run_block(blk, nxt):
        tiles = tiles_of(blk)
        pending = []
        for i, ((k0, _, _), r0, (h, m, c)) in enumerate(tiles):
            j = i + DA_LOOKAHEAD
            if j < len(tiles):
                pending.append(scores(tiles[j]))
            elif nxt is not None:
                spre_scr[j - len(tiles)] = scores(tiles_of(nxt)[j - len(tiles)])
            s = spre_scr[i] if i < DA_LOOKAHEAD else pending.pop(0)
            m_old = m_scr[h, m, :, cols[c]]
            m_new = jnp.maximum(m_old, jnp.max(s, axis=0, keepdims=True))
            alpha = jnp.exp2(m_old - m_new)
            p = jnp.exp2(s - m_new).astype(BF)
            pv = _dot(vt_ref[0, h, :, pl.ds(k0 + r0, DA_KROWS)], p)
            acc_scr[h, m, :, cols[c]] = alpha * acc_scr[h, m, :, cols[c]] + pv
            m_scr[h, m, :, cols[c]] = m_new

    assert DA_KROWS == DA_QCOLS, "diagonal tiles are taken to be square"
    rel = (lax.broadcasted_iota(jnp.int32, (DA_KROWS, DA_QCOLS), 0)
           - lax.broadcasted_iota(jnp.int32, (DA_KROWS, DA_QCOLS), 1))
    dist_scr[...] = jnp.abs(rel).astype(F32)
    pieces = _piece_of(xrow)
    tail = jnp.zeros((DA_LANES - DA_V - X_ROWS, bq), BF)
    for h in range(DA_HEADS):
        slope = ALIBI_SLOPES[h]
        ih, il = _alibi_parts(qpos, slope)
        ex =jnp.where(xrow < X_QHI, pieces,
                       jnp.where(xrow < X_QLO, -ih, jnp.where(xrow < X_END, -il, 0.0))).astype(BF)
        for m in range(2):
            qm = qt_ref[0, h] * jnp.where(frow >> _log2(DA_QK) == m, 1.0, 0.0).astype(BF)
            for side, extra in ((Q_LEFT, ex), (Q_DIAG, jnp.zeros_like(ex)), (Q_RIGHT, -ex)):
                qv_scr[h, side, m, :DA_V, :] = qm
                qv_scr[h, side, m, DA_V:DA_V + X_ROWS, :] = extra
                qv_scr[h, side, m, DA_V + X_ROWS:, :] = tail
    m_scr[...] = jnp.full(m_scr.shape, NEG_BIG, F32)
    acc_scr[...] = jnp.zeros(acc_scr.shape, F32)

    def off_diagonal(t):
        kb = jnp.where(t >= qi, t + 1, t)
        return pl.multiple_of(kb * bk, bk), jnp.where(t >= qi, Q_RIGHT, Q_LEFT), False

    diagonal = (pl.multiple_of(qi * bk, bk), Q_DIAG, True)

    for i, tile in enumerate(tiles_of(off_diagonal(0))[:DA_LOOKAHEAD]):
        spre_scr[i] = scores(tile)

    def body(t, carry):
        run_block(off_diagonal(t), off_diagonal(t + 1))
        return carry

    lax.fori_loop(0, nk - 2, body, 0)
    run_block(off_diagonal(nk - 2), diagonal)
    run_block(diagonal, None)

    heads = []
    for h in range(DA_HEADS):
        o = [acc_scr[h, m, :DA_V, :] / acc_scr[h, m, X_SUM:X_SUM + 1, :] for m in range(2)]
        heads.append(o[0] - lam * o[1])

    full = jnp.concatenate(heads, axis=0).T
    y = full * lax.rsqrt(_group_mean_sq(full, DA_V) + EPS) * subg_ref[...]
    o_ref[0] = (y * z_ref[0].astype(F32)).astype(BF)


def _da(qt4, k4, vt4, p3, lamv, subg, lam_init, bq=512):
    b, _, t, _ = k4.shape
    assert all(math.frexp(s)[0] == 0.5 for s in ALIBI_SLOPES), "ALiBi slopes must be powers of two"
    assert t <= POS_SPLIT * 256 and POS_SPLIT <= 256
    return pl.pallas_call(
        functools.partial(_da_kernel, lam_init),
        grid=(b, t // bq),
        in_specs=[
            pl.BlockSpec((1, DA_HEADS, DA_V, bq), lambda i, j: (i, 0, 0, j)),
            pl.BlockSpec((1, DA_HEADS, t, DA_LANES), lambda i, j: (i, 0, 0, 0)),
            pl.BlockSpec((1, DA_HEADS, DA_VT_ROWS, t), lambda i, j: (i, 0, 0, 0)),
            pl.BlockSpec((1, bq, BRANCH_W), lambda i, j: (i, j, P_POS[S_BZ])),
            _resident((4, DA_QK), lambda i, j: (0, 0)),
            _resident((1, BRANCH_W), lambda i, j: (0, 0)),
        ],
        out_specs=pl.BlockSpec((1, bq, BRANCH_W), lambda i, j: (i, j, 0)),
        out_shape=jax.ShapeDtypeStruct((b, t, BRANCH_W), BF),
        scratch_shapes=[
            pltpu.VMEM((DA_HEADS, 3, 2, DA_LANES, bq), BF),
            pltpu.VMEM((DA_KROWS, DA_QCOLS), F32),
            pltpu.VMEM((DA_LOOKAHEAD, DA_KROWS, DA_QCOLS), F32),
            pltpu.VMEM((DA_HEADS, 2, 1, bq), F32),
            pltpu.VMEM((DA_HEADS, 2, DA_VT_ROWS, bq), F32),
        ],
        compiler_params=_cparams("parallel", "arbitrary"),
        name="da",
    )(qt4, k4, vt4, p3, lamv, subg)


POOL_HALO = 16


def _pool_mix(x_ref, z_ref, ti, band_ref, cw_ref, cs_ref):
    tt = z_ref.shape[0]
    t_total = x_ref.shape[1]
    nt = t_total // tt
    t0 = pl.multiple_of(ti * tt, tt)
    xm = x_ref[0, pl.ds(t0, tt), :]
    p0 = pl.multiple_of(jnp.maximum(t0 - POOL_HALO, 0), POOL_HALO)
    n0 = pl.multiple_of(jnp.minimum(t0 + tt, t_total - POOL_HALO), POOL_HALO)
    has_prev = (ti > 0).astype(F32)
    has_next = (ti < nt - 1).astype(F32)
    prev = (x_ref[0, pl.ds(p0, POOL_HALO), :].astype(F32) * has_prev).astype(BF)
    nxt = (x_ref[0, pl.ds(n0, POOL_HALO), :].astype(F32) * has_next).astype(BF)
    xcat = jnp.concatenate([prev, xm, nxt], axis=0)

    sub = band_ref.shape[1]
    group = _lane_group((sub, BRANCH_W), POOL_GC)
    half = jnp.left_shift(1, group)
    sums = [[_dot(band_ref[g], xcat[r0:r0 + sub + 2 * POOL_HALO]) for g in range(len(POOL_WINDOWS))]
            for r0 in range(0, tt, sub)]
    out = []
    for i, r0 in enumerate(range(0, tt, sub)):
        wsum = jnp.zeros((sub, BRANCH_W), F32)
        for g in range(len(POOL_WINDOWS)):
            wsum = jnp.where(group == g, sums[i][g], wsum)
        tpos = t0 + r0 + lax.broadcasted_iota(jnp.int32, (sub, BRANCH_W), 0)
        lo = jnp.maximum(tpos - half, 0)
        hi = jnp.minimum(tpos + half - 1, t_total - 1)
        cnt = (hi - lo + 1).astype(F32)
        dlt = wsum / cnt - xm[r0:r0 + sub].astype(F32)
        y = _dot(dlt.astype(BF), cw_ref[...]) * cs_ref[...]
        out.append((y * z_ref[r0:r0 + sub, :].astype(F32)).astype(BF))
    return jnp.concatenate(out, axis=0)


def _pool_band(tt):
    band = np.zeros((len(POOL_WINDOWS), tt, tt + 2 * POOL_HALO), np.float32)
    t = np.arange(tt)[:, None]
    u = np.arange(tt + 2 * POOL_HALO)[None, :] - POOL_HALO
    for g, w in enumerate(POOL_WINDOWS):
        band[g] = ((u >= t - w // 2) & (u <= t - w // 2 + w - 1)).astype(np.float32)
    return jnp.asarray(band, BF)


POOL_SUB = 256


def _spatial_gate(u, vn, z, ws_ref, bias_ref, o_ref):
    group = _lane_group((SG_CHUNK, BRANCH_W), SG_GC)
    for c in range(u.shape[0] // SG_CHUNK):
        rows = slice(c * SG_CHUNK, (c + 1) * SG_CHUNK)
        mix = jnp.zeros((SG_CHUNK, BRANCH_W), F32)
        for g in range(SG_GROUPS):
            mix = jnp.where(group == g, _dot(ws_ref[g], vn[rows]), mix)
        o_ref[rows, :] = (u[rows] * (mix + bias_ref[...]) * z[rows]).astype(BF)


def _mem_attend(q, z, k, v):
    head_of_lane = _lane_group(q.shape, MEM_HD)
    scores = [_dot_nt(q * _lane_mask(q.shape, h * MEM_HD, (h + 1) * MEM_HD, BF), k) for h in range(MEM_HEADS)]
    o = jnp.zeros(q.shape, F32)
    for h in range(MEM_HEADS):
        s = scores[h]
        m = jnp.max(s, axis=-1, keepdims=True)
        p = jnp.exp2(s - m)
        l = jnp.sum(p, axis=-1, keepdims=True)
        oh = _dot(p.astype(BF), v) / l
        o = jnp.where(head_of_lane == h, oh, o)
    return o * z


def _final_kernel(tiles_per_seq, x_ref, ng_ref, ya_ref, yb_ref, cx_ref, cz_ref, yd_ref, ym_ref,
                  band_ref, cw_ref, cs_ref, wg_ref, bg_ref, wb_ref, wo_ref, o_ref):
    x = x_ref[...]
    y_c = _pool_mix(cx_ref, cz_ref, pl.program_id(0) % tiles_per_seq, band_ref, cw_ref, cs_ref)
    h = _rmsnorm_rows(x, ng_ref[...]).astype(BF)
    merged = jnp.zeros(x.shape, F32)
    for i, y in enumerate((ya_ref[...], yb_ref[...], y_c, yd_ref[...], ym_ref[...])):
        logit = _dot(h, wg_ref[:, i * D_MODEL:(i + 1) * D_MODEL]) + bg_ref[i:i + 1, :]
        merged = merged + _sigmoid(logit) * _dot(y, wb_ref[i])
    o_ref[...] = x + _dot(merged.astype(BF), wo_ref[...])


def _final(x2, norm_g, y_a, y_b, p3, y_d, y_m, cw_bd, c_scale, w_gate, b_gate, w_branch, w_out, layer, tm=512):
    n = x2.shape[0]
    b, t, _ = p3.shape
    tps = t // tm
    band = _pool_band(POOL_SUB)
    yspec = pl.BlockSpec((tm, BRANCH_W), lambda i: (i, 0))
    return pl.pallas_call(
        functools.partial(_final_kernel, tps),
        grid=(n // tm,),
        in_specs=[
            pl.BlockSpec((tm, D_MODEL), lambda i: (i, 0)),
            _resident((1, D_MODEL), lambda i: (0, 0)),
            yspec, yspec,
            pl.BlockSpec((1, t, BRANCH_W), lambda i: (i // tps, 0, P_POS[S_CX])),
            pl.BlockSpec((tm, BRANCH_W), lambda i: (i, P_POS[S_CZ])),
            yspec, yspec,
            _resident(band.shape, lambda i: (0, 0, 0)),
            _resident((BRANCH_W, BRANCH_W), lambda i: (0, 0)),
            _resident((1, BRANCH_W), lambda i: (0, 0)),
            _layer_resident((D_MODEL, N_BRANCH * D_MODEL), layer),
            _resident((8, D_MODEL), lambda i: (0, 0)),
            _layer_resident((N_BRANCH, BRANCH_W, D_MODEL), layer),
            _layer_resident((D_MODEL, D_MODEL), layer),
        ],
        out_specs=pl.BlockSpec((tm, D_MODEL), lambda i: (i, 0)),
        out_shape=jax.ShapeDtypeStruct((n, D_MODEL), F32),
        compiler_params=_cparams("parallel"),
        name="final",
    )(x2, norm_g, y_a, y_b, p3, p3.reshape(n, P_COLS), y_d, y_m, band, cw_bd, c_scale,
      w_gate, b_gate, w_branch, w_out)


W_IN_CAST_ROWS = 128


def _cast_w_in_kernel(w_ref, slab_ref, gate_ref):
    w = w_ref[0]
    slab_ref[0] = w[:, :SLAB_COLS].astype(BF)
    gate_ref[0] = w[:, SLAB_COLS:].astype(BF)


def _cast_w_in(w_in):
    depth, d, cols = w_in.shape
    gate_cols = cols - SLAB_COLS
    return pl.pallas_call(
        _cast_w_in_kernel,
        grid=(depth, d // W_IN_CAST_ROWS),
        in_specs=[pl.BlockSpec((1, W_IN_CAST_ROWS, cols), lambda l, r: (l, r, 0))],
        out_specs=[pl.BlockSpec((1, W_IN_CAST_ROWS, SLAB_COLS), lambda l, r: (l, r, 0)),
                   pl.BlockSpec((1, W_IN_CAST_ROWS, gate_cols), lambda l, r: (l, r, 0))],
        out_shape=[jax.ShapeDtypeStruct((depth, d, SLAB_COLS), BF),
                   jax.ShapeDtypeStruct((depth, d, gate_cols), BF)],
        compiler_params=_cparams("parallel", "parallel"),
        name="cast_w_in",
    )(w_in)


def _tile4(g):
    return jnp.tile(g.astype(F32), BRANCH_W // g.shape[0])


def _layer(x, mem, layer_idx, stacked, norm_g, b_gate, a_qn_g, a_kn_g, a_rpb,
           b_qn_g, b_kn_g, b_lam_q1, b_lam_k1, b_lam_q2, b_lam_k2, b_sub_g,
           c_w, c_scale, d_ln_g, d_ln_b, d_bs, m_norm_g, m_qn_g, m_kn_g):
    b, t, d = x.shape
    n = b * t
    x2 = x.reshape(n, d)
    ng = norm_g.reshape(1, d).astype(F32)

    gains = jnp.zeros((8, BRANCH_W), F32)
    gains = gains.at[G_AQ].set(_tile4(a_qn_g) * (NA_HD ** -0.5 * LOG2E))
    gains = gains.at[G_AK].set(_tile4(a_kn_g))
    gains = gains.at[G_BQ].set(_tile4(b_qn_g) * (DA_QK ** -0.5 * LOG2E))
    gains = gains.at[G_BK].set(_tile4(b_kn_g))
    gains = gains.at[G_MQ].set(_tile4(m_qn_g) * (MEM_HD ** -0.5 * LOG2E))
    gains = gains.at[G_LNG].set(d_ln_g.astype(F32))
    gains = gains.at[G_LNB].set(d_ln_b.astype(F32))

    mk, mv = _memkv(mem, m_norm_g.reshape(1, d).astype(F32), stacked["m_wkv"], layer_idx,
                    _tile4(m_kn_g).reshape(1, BRANCH_W))
    sg_bias = jnp.repeat(d_bs.astype(F32).T, SG_GC, axis=1)
    p2, y_d, y_m, qt4, k4, vt4 = _proj(x2, ng, stacked["w_slab"], layer_idx, gains,
                                       stacked["d_ws"], sg_bias, mk, mv, t)
    p3 = p2.reshape(b, t, P_COLS)

    y_a = _na(p3, _na_bias_table(a_rpb, t // GRID_W))

    lam_init = 0.8 - 0.6 * math.exp(-0.3 * layer_idx)
    lamv = jnp.stack([b_lam_q1, b_lam_k1, b_lam_q2, b_lam_k2]).astype(F32)
    subg = (_tile4(b_sub_g) * (1.0 - lam_init)).reshape(1, BRANCH_W)
    y_b = _da(qt4, k4, vt4, p3, lamv, subg, lam_init)

    cw_bd = jnp.zeros((BRANCH_W, BRANCH_W), F32)
    for g in range(len(POOL_WINDOWS)):
        cw_bd = cw_bd.at[g * POOL_GC:(g + 1) * POOL_GC, g * POOL_GC:(g + 1) * POOL_GC].set(c_w[g].astype(F32))

    bg = jnp.zeros((8, d), F32).at[:N_BRANCH].set(b_gate.astype(F32))
    out = _final(x2, ng, y_a.reshape(n, BRANCH_W), y_b.reshape(n, BRANCH_W), p3, y_d, y_m,
                 cw_bd.astype(BF), c_scale.reshape(1, BRANCH_W).astype(F32),
                 stacked["w_gate"], bg, stacked["w_branch"], stacked["w_out"], layer_idx)
    return out.reshape(b, t, d)


def kernel(x, mem, norm_g, w_in, b_gate, a_qn_g, a_kn_g, a_rpb, b_qn_g, b_kn_g, b_lam_q1, b_lam_k1, b_lam_q2, b_lam_k2, b_sub_g, c_w, c_scale, d_ln_g, d_ln_b, d_ws, d_bs, m_norm_g, m_wkv, m_qn_g, m_kn_g, w_branch, w_out):
    w_slab, w_gate = _cast_w_in(w_in)
    stacked = {
        "w_slab": w_slab,
        "w_gate": w_gate,
        "w_branch": w_branch.astype(BF),
        "w_out": w_out.astype(BF),
        "m_wkv": m_wkv.astype(BF),
        "d_ws": d_ws.astype(BF),
    }
    for l in range(DEPTH):
        x = _layer(x, mem, l, stacked, norm_g[l], b_gate[l], a_qn_g[l], a_kn_g[l], a_rpb[l],
                   b_qn_g[l], b_kn_g[l], b_lam_q1[l], b_lam_k1[l], b_lam_q2[l], b_lam_k2[l], b_sub_g[l],
                   c_w[l], c_scale[l], d_ln_g[l], d_ln_b[l], d_bs[l], m_norm_g[l], m_qn_g[l], m_kn_g[l])
    return x
```

```python
import functools
import math

import numpy as np
import jax
import jax.numpy as jnp
from jax import lax
from jax.experimental import pallas as pl
from jax.experimental.pallas import tpu as pltpu

F32 = jnp.float32
BF = jnp.bfloat16

D_MODEL = 1024
DEPTH = 2
GRID_W = 64
BRANCH_W = 256
N_BRANCH = 5
NA_HEADS = 4
NA_HD = 64
NA_KH = 8
NA_KW = 16
DA_HEADS = 4
DA_QK = 32
DA_V = 64
ALIBI_BASE = 8.0
POOL_WINDOWS = (2, 4, 8, 16)
POOL_GC = 64
SG_CHUNK = 128
SG_GROUPS = 4
SG_GC = 64
MEM_HEADS = 4
MEM_HD = 64
N_SLABS = 15
SLAB_COLS = N_SLABS * BRANCH_W
EPS = 1e-6

(S_AQ, S_AK, S_AV, S_AZ, S_BQ, S_BK, S_BV, S_BZ, S_CX, S_CZ,
 S_DU, S_DV, S_DZ, S_MQ, S_MZ) = range(N_SLABS)

(G_AQ, G_AK, G_BQ, G_BK, G_MQ, G_LNG, G_LNB) = range(7)

VMEM_LIMIT_BYTES = 56 * 1024 * 1024
NEG_BIG = -1e30

ALIBI_SLOPES = tuple(2.0 ** (-ALIBI_BASE * (h + 1) / DA_HEADS) for h in range(DA_HEADS))
DA_LANES = 128


def _bf16_pieces(x, n):
    out = []
    for _ in range(n):
        piece = float(np.asarray(x, np.float32).astype(jnp.bfloat16).astype(np.float32))
        out.append(piece)
        x -= piece
    return tuple(out)


LOG2E = math.log2(math.e)
LOG2E_PIECES = _bf16_pieces(LOG2E, 3)
N_PIECES = len(LOG2E_PIECES)
X_KHI, X_KLO, X_QHI, X_QLO = (DA_V + g * N_PIECES for g in range(4))
X_END = DA_V + 4 * N_PIECES
X_ROWS = 16
POS_SPLIT = 64


def _alibi_parts(pos, slope):
    hi = (pos >> _log2(POS_SPLIT)).astype(F32) * (slope * POS_SPLIT)
    lo = (pos & (POS_SPLIT - 1)).astype(F32) * slope
    return hi, lo


def _piece_of(idx):
    out = jnp.zeros(idx.shape, F32)
    for g in range(4):
        for p, piece in enumerate(LOG2E_PIECES):
            out = jnp.where(idx == DA_V + g * N_PIECES + p, piece, out)
    return out


def _cparams(*sem):
    return pltpu.CompilerParams(dimension_semantics=sem, vmem_limit_bytes=VMEM_LIMIT_BYTES)


def _resident(shape, index_map):
    return pl.BlockSpec(shape, index_map, pipeline_mode=pl.Buffered(1))


def _layer_resident(shape, layer):
    return pl.BlockSpec((None,) + tuple(shape), lambda *_: (layer,) + (0,) * len(shape),
                        pipeline_mode=pl.Buffered(1))


def _dot(a, b):
    return jnp.dot(a, b, preferred_element_type=F32)


def _dot_nt(a, b):
    return lax.dot_general(a, b, (((1,), (1,)), ((), ())), preferred_element_type=F32)


def _log2(n):
    assert n & (n - 1) == 0, "power of two expected"
    return n.bit_length() - 1


def _lane_group(shape, width):
    return lax.broadcasted_iota(jnp.int32, shape, len(shape) - 1) >> _log2(width)


def _lane_mask(shape, lo, hi, dtype):
    lane = lax.broadcasted_iota(jnp.int32, shape, len(shape) - 1)
    return jnp.where((lane >= lo) & (lane < hi), 1.0, 0.0).astype(dtype)


def _group_mean_sq(xf, group):
    n = xf.shape[-1]
    r = _lane_group((n, n), group)
    c = lax.broadcasted_iota(jnp.int32, (n, n), 0) >> _log2(group)
    pm = jnp.where(r == c, 1.0 / group, 0.0).astype(BF)
    return _dot((xf * xf).astype(BF), pm)


def _sigmoid(x):
    return 1.0 / (1.0 + jnp.exp(-x))


def _silu(x):
    return x * _sigmoid(x)


def _gelu_tanh(x):
    return 0.5 * x * (1.0 + jnp.tanh(math.sqrt(2.0 / math.pi) * (x + 0.044715 * (x * x * x))))


def _rmsnorm_rows(x, g):
    ms = jnp.mean(x * x, axis=-1, keepdims=True)
    return x * lax.rsqrt(ms + EPS) * g


PROJ_LOOKAHEAD = 3
P_SLABS = (S_AQ, S_AK, S_AV, S_AZ, S_BZ, S_CX, S_CZ)
P_POS = {s: i for i, s in enumerate(P_SLABS)}
P_COLS = len(P_SLABS) * BRANCH_W
PROJ_ORDER = (S_MQ, S_MZ, S_DU, S_DV, S_DZ, S_BQ, S_BK, S_BV) + P_SLABS


def _proj_kernel(tiles_per_seq, x_ref, ng_ref, w_ref, gains_ref, ws_ref, sgb_ref, mk_ref, mv_ref,
                 p_ref, yd_ref, ym_ref, qt_ref, k4_ref, vt_ref):
    tm = x_ref.shape[0]
    t0 = (pl.program_id(0) % tiles_per_seq) * tm
    h = _rmsnorm_rows(x_ref[...], ng_ref[...]).astype(BF)

    def gnorm(r, group, row):
        return r * lax.rsqrt(_group_mean_sq(r, group) + EPS) * gains_ref[row:row + 1, :]

    def slab(j):
        return _dot(h, w_ref[:, j * BRANCH_W:(j + 1) * BRANCH_W])

    kept = {}
    pending = [slab(j) for j in PROJ_ORDER[:PROJ_LOOKAHEAD]]
    for idx, j in enumerate(PROJ_ORDER):
        if idx + PROJ_LOOKAHEAD < N_SLABS:
            pending.append(slab(PROJ_ORDER[idx + PROJ_LOOKAHEAD]))
        r = pending.pop(0)
        if j == S_AQ:
            r = gnorm(r, NA_HD, G_AQ)
        elif j == S_AK:
            r = gnorm(r, NA_HD, G_AK)
        elif j == S_BQ:
            r = gnorm(r, DA_QK, G_BQ)
        elif j == S_BK:
            r = gnorm(r, DA_QK, G_BK)
        elif j == S_MQ:
            r = gnorm(r, MEM_HD, G_MQ)
        elif j in (S_AZ, S_BZ, S_CZ, S_DZ, S_MZ):
            r = _silu(r)
        elif j == S_DU:
            r = _gelu_tanh(r)
        elif j == S_DV:
            v = _gelu_tanh(r)
            mu = jnp.mean(v, axis=-1, keepdims=True)
            vc = v - mu
            var = jnp.mean(vc * vc, axis=-1, keepdims=True)
            r = vc * lax.rsqrt(var + EPS) * gains_ref[G_LNG:G_LNG + 1, :] + gains_ref[G_LNB:G_LNB + 1, :]
        if j in P_POS:
            p_ref[:, P_POS[j] * BRANCH_W:(P_POS[j] + 1) * BRANCH_W] = r.astype(BF)
        elif j == S_MZ:
            ym_ref[...] = _mem_attend(kept.pop(S_MQ), r, mk_ref[0], mv_ref[0]).astype(BF)
        elif j == S_DZ:
            _spatial_gate(kept.pop(S_DU), kept.pop(S_DV), r, ws_ref, sgb_ref, yd_ref)
        elif j == S_BV:
            _da_layouts(kept.pop(S_BQ), kept.pop(S_BK), r.astype(BF), t0, qt_ref, k4_ref, vt_ref)
        elif j in (S_MQ, S_DV, S_BQ, S_BK):
            kept[j] = r.astype(BF)
        else:
            kept[j] = r


def _proj(x2, norm_g, w_slab, layer, gains, ws, sg_bias, mk, mv, seq_len, tm=512):
    n = x2.shape[0]
    b = n // seq_len
    tps = seq_len // tm
    m = mk.shape[1]
    return pl.pallas_call(
        functools.partial(_proj_kernel, tps),
        grid=(n // tm,),
        in_specs=[
            pl.BlockSpec((tm, D_MODEL), lambda i: (i, 0)),
            _resident((1, D_MODEL), lambda i: (0, 0)),
            _layer_resident((D_MODEL, SLAB_COLS), layer),
            _resident((8, BRANCH_W), lambda i: (0, 0)),
            _layer_resident((SG_GROUPS, SG_CHUNK, SG_CHUNK), layer),
            _resident((SG_CHUNK, BRANCH_W), lambda i: (0, 0)),
            pl.BlockSpec((1, m, BRANCH_W), lambda i: (i // tps, 0, 0)),
            pl.BlockSpec((1, m, BRANCH_W), lambda i: (i // tps, 0, 0)),
        ],
        out_specs=[
            pl.BlockSpec((tm, P_COLS), lambda i: (i, 0)),
            pl.BlockSpec((tm, BRANCH_W), lambda i: (i, 0)),
            pl.BlockSpec((tm, BRANCH_W), lambda i: (i, 0)),
            pl.BlockSpec((1, DA_HEADS, DA_V, tm), lambda i: (i // tps, 0, 0, i % tps)),
            pl.BlockSpec((1, DA_HEADS, tm, DA_LANES), lambda i: (i // tps, 0, i % tps, 0)),
            pl.BlockSpec((1, DA_HEADS, DA_VT_ROWS, tm), lambda i: (i // tps, 0, 0, i % tps)),
        ],
        out_shape=[
            jax.ShapeDtypeStruct((n, P_COLS), BF),
            jax.ShapeDtypeStruct((n, BRANCH_W), BF),
            jax.ShapeDtypeStruct((n, BRANCH_W), BF),
            jax.ShapeDtypeStruct((b, DA_HEADS, DA_V, seq_len), BF),
            jax.ShapeDtypeStruct((b, DA_HEADS, seq_len, DA_LANES), BF),
            jax.ShapeDtypeStruct((b, DA_HEADS, DA_VT_ROWS, seq_len), BF),
        ],
        compiler_params=_cparams("parallel"),
        name="proj",
    )(x2, norm_g, w_slab, gains, ws, sg_bias, mk, mv)


def _memkv_kernel(mem_ref, g_ref, w_ref, kg_ref, k_ref, v_ref):
    h = _rmsnorm_rows(mem_ref[0], g_ref[...]).astype(BF)
    k = _dot(h, w_ref[:, :BRANCH_W])
    v = _dot(h, w_ref[:, BRANCH_W:])
    k = k * lax.rsqrt(_group_mean_sq(k, MEM_HD) + EPS) * kg_ref[...]
    k_ref[0] = k.astype(BF)
    v_ref[0] = v.astype(BF)


def _memkv(mem, m_norm_g, m_wkv, layer, kg):
    b, m, _ = mem.shape
    return pl.pallas_call(
        _memkv_kernel,
        grid=(b,),
        in_specs=[
            pl.BlockSpec((1, m, D_MODEL), lambda i: (i, 0, 0)),
            _resident((1, D_MODEL), lambda i: (0, 0)),
            _layer_resident((D_MODEL, 2 * BRANCH_W), layer),
            _resident((1, BRANCH_W), lambda i: (0, 0)),
        ],
        out_specs=[pl.BlockSpec((1, m, BRANCH_W), lambda i: (i, 0, 0)),
                   pl.BlockSpec((1, m, BRANCH_W), lambda i: (i, 0, 0))],
        out_shape=[jax.ShapeDtypeStruct((b, m, BRANCH_W), BF)] * 2,
        compiler_params=_cparams("parallel"),
        name="memkv",
    )(mem, m_norm_g, m_wkv, kg)


NA_ROWS_PER_STEP = 8
NA_WIN = NA_KH * GRID_W
NA_LOOKAHEAD = 3


def _na_kernel(q_ref, k_ref, v_ref, z_ref, bias_ref, o_ref):
    n_rows = k_ref.shape[1] // GRID_W
    rb = pl.program_id(1)
    head_of_lane = _lane_group((GRID_W, BRANCH_W), NA_HD)
    masks = [_lane_mask((GRID_W, BRANCH_W), h * NA_HD, (h + 1) * NA_HD, BF) for h in range(NA_HEADS)]

    def window(rr):
        r = rb * NA_ROWS_PER_STEP + rr
        rs = jnp.clip(r - NA_KH // 2, 0, n_rows - NA_KH)
        cls = jnp.where(r < NA_KH // 2, r, jnp.where(r > n_rows - NA_KH // 2, r - (n_rows - NA_KH), NA_KH // 2))
        return pl.multiple_of(rs * GRID_W, GRID_W), cls

    def scores(rr):
        w0, cls = window(rr)
        q = q_ref[0, rr * GRID_W:(rr + 1) * GRID_W, :]
        qs = jnp.concatenate([q * masks[h] for h in range(NA_HEADS)], axis=0)
        return _dot_nt(qs, k_ref[0, pl.ds(w0, NA_WIN), :]) + bias_ref[cls]

    pending = [scores(rr) for rr in range(NA_LOOKAHEAD)]
    for rr in range(NA_ROWS_PER_STEP):
        if rr + NA_LOOKAHEAD < NA_ROWS_PER_STEP:
            pending.append(scores(rr + NA_LOOKAHEAD))
        s = pending.pop(0)
        w0, _ = window(rr)
        m = jnp.max(s, axis=-1, keepdims=True)
        p = jnp.exp2(s - m)
        l = jnp.sum(p, axis=-1, keepdims=True)
        pv = _dot(p.astype(BF), v_ref[0, pl.ds(w0, NA_WIN), :]) / l
        o = jnp.zeros((GRID_W, BRANCH_W), F32)
        for h in range(NA_HEADS):
            o = jnp.where(head_of_lane == h, pv[h * GRID_W:(h + 1) * GRID_W], o)
        rows = slice(rr * GRID_W, (rr + 1) * GRID_W)
        o_ref[0, rows, :] = (o * z_ref[0, rows, :].astype(F32)).astype(BF)


def _na(p3, bias):
    b, t, _ = p3.shape
    tq = NA_ROWS_PER_STEP * GRID_W
    return pl.pallas_call(
        _na_kernel,
        grid=(b, t // tq),
        in_specs=[
            pl.BlockSpec((1, tq, BRANCH_W), lambda i, j: (i, j, P_POS[S_AQ])),
            pl.BlockSpec((1, t, BRANCH_W), lambda i, j: (i, 0, P_POS[S_AK])),
            pl.BlockSpec((1, t, BRANCH_W), lambda i, j: (i, 0, P_POS[S_AV])),
            pl.BlockSpec((1, tq, BRANCH_W), lambda i, j: (i, j, P_POS[S_AZ])),
            _resident(bias.shape, lambda i, j: (0, 0, 0)),
        ],
        out_specs=pl.BlockSpec((1, tq, BRANCH_W), lambda i, j: (i, j, 0)),
        out_shape=jax.ShapeDtypeStruct((b, t, BRANCH_W), BF),
        compiler_params=_cparams("parallel", "arbitrary"),
        name="na",
    )(p3, p3, p3, p3, bias)


NA_REL_ROWS = 2 * NA_KH - 1
NA_REL_COLS = 2 * NA_KW - 1
NA_CLASSES = NA_KH


def _na_class_roff(n_rows):
    half = NA_KH // 2
    rows = np.array(list(range(half)) + [half] + [n_rows - half + 1 + i for i in range(half - 1)])
    rs = np.clip(rows - half, 0, n_rows - NA_KH)
    return [int(v) for v in rs - rows + NA_KH - 1]


def _nabias_kernel(roff0, rpb_ref, o_ref, f_scr):
    h = pl.program_id(0)
    shape = (GRID_W, 2 * GRID_W)
    q = lax.broadcasted_iota(jnp.int32, shape, 0)
    lane = lax.broadcasted_iota(jnp.int32, shape, 1)
    k = lane & (GRID_W - 1)
    coff = k - q + (NA_KW - 1)
    cs = jnp.clip(q - NA_KW // 2, 0, GRID_W - NA_KW)
    valid = (k >= cs) & (k < cs + NA_KW)
    first = lane < GRID_W

    def tile(r, carry):
        base = (h * NA_REL_ROWS + r) * NA_REL_COLS
        acc = jnp.zeros(shape, F32)
        for s in range(NA_REL_COLS):
            acc = jnp.where(coff == s, jnp.where(first, rpb_ref[base + s], rpb_ref[base + NA_REL_COLS + s]), acc)
        f_scr[r] = jnp.where(valid, acc * LOG2E, NEG_BIG)
        return carry

    lax.fori_loop(0, NA_REL_ROWS - 1, tile, 0)
    for c in range(NA_CLASSES):
        for p in range(NA_KH // 2):
            o_ref[c, 0, :, p * 2 * GRID_W:(p + 1) * 2 * GRID_W] = f_scr[roff0[c] + 2 * p]


def _na_bias_table(rpb, n_rows):
    return pl.pallas_call(
        functools.partial(_nabias_kernel, _na_class_roff(n_rows)),
        grid=(NA_HEADS,),
        in_specs=[pl.BlockSpec(memory_space=pltpu.SMEM)],
        out_specs=pl.BlockSpec((NA_CLASSES, 1, GRID_W, NA_WIN), lambda h: (0, h, 0, 0)),
        out_shape=jax.ShapeDtypeStruct((NA_CLASSES, NA_HEADS, GRID_W, NA_WIN), F32),
        scratch_shapes=[pltpu.VMEM((NA_REL_ROWS - 1, GRID_W, 2 * GRID_W), F32)],
        compiler_params=_cparams("arbitrary"),
        name="nabias",
    )(rpb.astype(F32).reshape(-1)).reshape(NA_CLASSES, NA_HEADS * GRID_W, NA_WIN)


DA_VT_ROWS = 80
X_SUM = DA_V


def _da_layouts(q, k, v, t0, qt_ref, k4_ref, vt_ref):
    tt = q.shape[0]
    src = lax.broadcasted_iota(jnp.int32, (BRANCH_W, DA_HEADS * DA_LANES), 0)
    dst = lax.broadcasted_iota(jnp.int32, (BRANCH_W, DA_HEADS * DA_LANES), 1)
    dst_head, dst_lane = dst >> _log2(DA_LANES), dst & (DA_LANES - 1)
    spread = jnp.where((src == dst_head * DA_V + dst_lane) & (dst_lane < DA_V), 1.0, 0.0).astype(BF)
    k_all = _dot(k, spread)
    src_t = lax.broadcasted_iota(jnp.int32, (DA_LANES, BRANCH_W), 1)
    dst_t = lax.broadcasted_iota(jnp.int32, (DA_LANES, BRANCH_W), 0)
    klane = lax.broadcasted_iota(jnp.int32, (tt, DA_LANES), 1)
    kpos = t0 + lax.broadcasted_iota(jnp.int32, (tt, DA_LANES), 0)
    vrow = lax.broadcasted_iota(jnp.int32, (DA_VT_ROWS, tt), 0)
    for h in range(DA_HEADS):
        sel_t = jnp.where((src_t == h * DA_V + dst_t) & (dst_t < DA_V), 1.0, 0.0).astype(BF)
        qt_ref[0, h] = _dot_nt(sel_t[:DA_V], q).astype(BF)
        vt = _dot_nt(sel_t[:DA_VT_ROWS], v)
        vt_ref[0, h] = jnp.where(vrow == X_SUM, 1.0, vt).astype(BF)
        slope = ALIBI_SLOPES[h]
        k4 = k_all[:, h * DA_LANES:(h + 1) * DA_LANES]
        jh, jl = _alibi_parts(kpos, slope)
        k4 = jnp.where((klane >= X_KHI) & (klane < X_KLO), jh, k4)
        k4 = jnp.where((klane >= X_KLO) & (klane < X_QHI), jl, k4)
        k4 = jnp.where((klane >= X_QHI) & (klane < X_END), _piece_of(klane), k4)
        k4_ref[0, h] = k4.astype(BF)


DA_QCOLS = 256
DA_KROWS = 256
DA_LOOKAHEAD = 6
Q_LEFT, Q_DIAG, Q_RIGHT = range(3)


def _da_kernel(lam_init, qt_ref, k_ref, vt_ref, z_ref, lamv_ref, subg_ref, o_ref,
               qv_scr, dist_scr, spre_scr, m_scr, acc_scr):
    bq = qt_ref.shape[3]
    bk = bq
    nk = k_ref.shape[2] // bk
    qi = pl.program_id(1)
    frow = lax.broadcasted_iota(jnp.int32, (DA_V, bq), 0)
    xrow = DA_V + lax.broadcasted_iota(jnp.int32, (X_ROWS, bq), 0)
    qpos = qi * bq + lax.broadcasted_iota(jnp.int32, (X_ROWS, bq), 1)

    lv = lamv_ref[...]
    lam = (jnp.exp(jnp.sum(lv[0:1] * lv[1:2], axis=-1, keepdims=True))
           - jnp.exp(jnp.sum(lv[2:3] * lv[3:4], axis=-1, keepdims=True)) + lam_init)

    cols = [slice(c * DA_QCOLS, (c + 1) * DA_QCOLS) for c in range(bq // DA_QCOLS)]
    chains = [(h, m, c) for h in range(DA_HEADS) for m in range(2) for c in range(len(cols))]

    def tiles_of(blk):
        return [(blk, kc * DA_KROWS, ch) for kc in range(bk // DA_KROWS) for ch in chains]

    def scores(tile):
        (k0, side, diag), r0, (h, m, c) = tile
        if diag:
            c0 = c * DA_QCOLS
            side = Q_DIAG if r0 == c0 else (Q_LEFT if r0 < c0 else Q_RIGHT)
        s = _dot(k_ref[0, h, pl.ds(k0 + r0, DA_KROWS), :], qv_scr[h, side, m, :, cols[c]])
        if diag and r0 == c0:
            s = s - dist_scr[...] * (ALIBI_SLOPES[h] * LOG2E)
        return s

    def run_block(blk, nxt):
        tiles = tiles_of(blk)
        for i, ((k0, _, _), r0, (h, m, c)) in enumerate(tiles):
            j = i + DA_LOOKAHEAD
            if j < len(tiles):
                spre_scr[j] = scores(tiles[j])
            elif nxt is not None:
                spre_scr[j - len(tiles)] = scores(tiles_of(nxt)[j - len(tiles)])
            s = spre_scr[i]
            m_old = m_scr[h, m, :, cols[c]]
            m_new = jnp.maximum(m_old, jnp.max(s, axis=0, keepdims=True))
            alpha = jnp.exp2(m_old - m_new)
            p = jnp.exp2(s - m_new).astype(BF)
            pv = _dot(vt_ref[0, h, :, pl.ds(k0 + r0, DA_KROWS)], p)
            acc_scr[h, m, :, cols[c]] = alpha * acc_scr[h, m, :, cols[c]] + pv
            m_scr[h, m, :, cols[c]] = m_new

    assert DA_KROWS == DA_QCOLS, "diagonal tiles are taken to be square"
    rel = (lax.broadcasted_iota(jnp.int32, (DA_KROWS, DA_QCOLS), 0)
           - lax.broadcasted_iota(jnp.int32, (DA_KROWS, DA_QCOLS), 1))
    dist_scr[...] = jnp.abs(rel).astype(F32)
    pieces = _piece_of(xrow)
    tail = jnp.zeros((DA_LANES - DA_V - X_ROWS, bq), BF)
    for h in range(DA_HEADS):
        slope = ALIBI_SLOPES[h]
        ih, il = _alibi_parts(qpos, slope)
        ex =jnp.where(xrow < X_QHI, pieces,
                       jnp.where(xrow < X_QLO, -ih, jnp.where(xrow < X_END, -il, 0.0))).astype(BF)
        for m in range(2):
            qm = qt_ref[0, h] * jnp.where(frow >> _log2(DA_QK) == m, 1.0, 0.0).astype(BF)
            for side, extra in ((Q_LEFT, ex), (Q_DIAG, jnp.zeros_like(ex)), (Q_RIGHT, -ex)):
                qv_scr[h, side, m, :DA_V, :] = qm
                qv_scr[h, side, m, DA_V:DA_V + X_ROWS, :] = extra
                qv_scr[h, side, m, DA_V + X_ROWS:, :] = tail
    m_scr[...] = jnp.full(m_scr.shape, NEG_BIG, F32)
    acc_scr[...] = jnp.zeros(acc_scr.shape, F32)

    def off_diagonal(t):
        kb = jnp.where(t >= qi, t + 1, t)
        return pl.multiple_of(kb * bk, bk), jnp.where(t >= qi, Q_RIGHT, Q_LEFT), False

    diagonal = (pl.multiple_of(qi * bk, bk), Q_DIAG, True)

    for i, tile in enumerate(tiles_of(off_diagonal(0))[:DA_LOOKAHEAD]):
        spre_scr[i] = scores(tile)

    def body(t, carry):
        run_block(off_diagonal(t), off_diagonal(t + 1))
        return carry

    lax.fori_loop(0, nk - 2, body, 0)
    run_block(off_diagonal(nk - 2), diagonal)
    run_block(diagonal, None)

    heads = []
    for h in range(DA_HEADS):
        o = [acc_scr[h, m, :DA_V, :] / acc_scr[h, m, X_SUM:X_SUM + 1, :] for m in range(2)]
        heads.append(o[0] - lam * o[1])

    full = jnp.concatenate(heads, axis=0).T
    y = full * lax.rsqrt(_group_mean_sq(full, DA_V) + EPS) * subg_ref[...]
    o_ref[0] = (y * z_ref[0].astype(F32)).astype(BF)


def _da(qt4, k4, vt4, p3, lamv, subg, lam_init, bq=512):
    b, _, t, _ = k4.shape
    assert all(math.frexp(s)[0] == 0.5 for s in ALIBI_SLOPES), "ALiBi slopes must be powers of two"
    assert t <= POS_SPLIT * 256 and POS_SPLIT <= 256
    return pl.pallas_call(
        functools.partial(_da_kernel, lam_init),
        grid=(b, t // bq),
        in_specs=[
            pl.BlockSpec((1, DA_HEADS, DA_V, bq), lambda i, j: (i, 0, 0, j)),
            pl.BlockSpec((1, DA_HEADS, t, DA_LANES), lambda i, j: (i, 0, 0, 0)),
            pl.BlockSpec((1, DA_HEADS, DA_VT_ROWS, t), lambda i, j: (i, 0, 0, 0)),
            pl.BlockSpec((1, bq, BRANCH_W), lambda i, j: (i, j, P_POS[S_BZ])),
            _resident((4, DA_QK), lambda i, j: (0, 0)),
            _resident((1, BRANCH_W), lambda i, j: (0, 0)),
        ],
        out_specs=pl.BlockSpec((1, bq, BRANCH_W), lambda i, j: (i, j, 0)),
        out_shape=jax.ShapeDtypeStruct((b, t, BRANCH_W), BF),
        scratch_shapes=[
            pltpu.VMEM((DA_HEADS, 3, 2, DA_LANES, bq), BF),
            pltpu.VMEM((DA_KROWS, DA_QCOLS), F32),
            pltpu.VMEM((2 * DA_HEADS * 2 * (bq // DA_QCOLS), DA_KROWS, DA_QCOLS), F32),
            pltpu.VMEM((DA_HEADS, 2, 1, bq), F32),
            pltpu.VMEM((DA_HEADS, 2, DA_VT_ROWS, bq), F32),
        ],
        compiler_params=_cparams("parallel", "arbitrary"),
        name="da",
    )(qt4, k4, vt4, p3, lamv, subg)


POOL_HALO = 16


def _pool_window_sums(x_ref, tt, ti, band_ref):
    t_total = x_ref.shape[1]
    nt = t_total // tt
    t0 = pl.multiple_of(ti * tt, tt)
    xm = x_ref[0, pl.ds(t0, tt), :]
    p0 = pl.multiple_of(jnp.maximum(t0 - POOL_HALO, 0), POOL_HALO)
    n0 = pl.multiple_of(jnp.minimum(t0 + tt, t_total - POOL_HALO), POOL_HALO)
    has_prev = (ti > 0).astype(F32)
    has_next = (ti < nt - 1).astype(F32)
    prev = (x_ref[0, pl.ds(p0, POOL_HALO), :].astype(F32) * has_prev).astype(BF)
    nxt = (x_ref[0, pl.ds(n0, POOL_HALO), :].astype(F32) * has_next).astype(BF)
    xcat = jnp.concatenate([prev, xm, nxt], axis=0)

    sub = band_ref.shape[1]
    sums = [[_dot(band_ref[g], xcat[r0:r0 + sub + 2 * POOL_HALO]) for g in range(len(POOL_WINDOWS))]
            for r0 in range(0, tt, sub)]
    return xm, sums


def _pool_finish(xm, sums, ti, t_total, z_ref, cw_ref, cs_ref):
    tt = xm.shape[0]
    sub = tt // len(sums)
    t0 = ti * tt
    group = _lane_group((sub, BRANCH_W), POOL_GC)
    half = jnp.left_shift(1, group)
    out = []
    for i, r0 in enumerate(range(0, tt, sub)):
        wsum = jnp.zeros((sub, BRANCH_W), F32)
        for g in range(len(POOL_WINDOWS)):
            wsum = jnp.where(group == g, sums[i][g], wsum)
        tpos = t0 + r0 + lax.broadcasted_iota(jnp.int32, (sub, BRANCH_W), 0)
        lo = jnp.maximum(tpos - half, 0)
        hi = jnp.minimum(tpos + half - 1, t_total - 1)
        cnt = (hi - lo + 1).astype(F32)
        dlt = wsum / cnt - xm[r0:r0 + sub].astype(F32)
        y = _dot(dlt.astype(BF), cw_ref[...]) * cs_ref[...]
        out.append((y * z_ref[r0:r0 + sub, :].astype(F32)).astype(BF))
    return jnp.concatenate(out, axis=0)


def _pool_band(tt):
    band = np.zeros((len(POOL_WINDOWS), tt, tt + 2 * POOL_HALO), np.float32)
    t = np.arange(tt)[:, None]
    u = np.arange(tt + 2 * POOL_HALO)[None, :] - POOL_HALO
    for g, w in enumerate(POOL_WINDOWS):
        band[g] = ((u >= t - w // 2) & (u <= t - w // 2 + w - 1)).astype(np.float32)
    return jnp.asarray(band, BF)


POOL_SUB = 256


def _spatial_gate(u, vn, z, ws_ref, bias_ref, o_ref):
    group = _lane_group((SG_CHUNK, BRANCH_W), SG_GC)
    for c in range(u.shape[0] // SG_CHUNK):
        rows = slice(c * SG_CHUNK, (c + 1) * SG_CHUNK)
        mix = jnp.zeros((SG_CHUNK, BRANCH_W), F32)
        for g in range(SG_GROUPS):
            mix = jnp.where(group == g, _dot(ws_ref[g], vn[rows]), mix)
        o_ref[rows, :] = (u[rows] * (mix + bias_ref[...]) * z[rows]).astype(BF)


def _mem_attend(q, z, k, v):
    head_of_lane = _lane_group(q.shape, MEM_HD)
    scores = [_dot_nt(q * _lane_mask(q.shape, h * MEM_HD, (h + 1) * MEM_HD, BF), k) for h in range(MEM_HEADS)]
    o = jnp.zeros(q.shape, F32)
    for h in range(MEM_HEADS):
        s = scores[h]
        m = jnp.max(s, axis=-1, keepdims=True)
        p = jnp.exp2(s - m)
        l = jnp.sum(p, axis=-1, keepdims=True)
        oh = _dot(p.astype(BF), v) / l
        o = jnp.where(head_of_lane == h, oh, o)
    return o * z


def _final_kernel(tiles_per_seq, x_ref, ng_ref, ya_ref, yb_ref, cx_ref, cz_ref, yd_ref, ym_ref,
                  band_ref, cw_ref, cs_ref, wg_ref, bg_ref, wb_ref, wo_ref, o_ref):
    x = x_ref[...]
    ti = pl.program_id(0) % tiles_per_seq
    xm, sums = _pool_window_sums(cx_ref, x.shape[0], ti, band_ref)
    h = _rmsnorm_rows(x, ng_ref[...]).astype(BF)

    def branch(i, y):
        return _dot(h, wg_ref[:, i * D_MODEL:(i + 1) * D_MODEL]), _dot(y, wb_ref[i])

    first = branch(0, ya_ref[...])
    y_c = _pool_finish(xm, sums, ti, cx_ref.shape[1], cz_ref, cw_ref, cs_ref)
    merged = jnp.zeros(x.shape, F32)
    for i, y in enumerate((None, yb_ref[...], y_c, yd_ref[...], ym_ref[...])):
        logit, proj = first if i == 0 else branch(i, y)
        merged = merged + _sigmoid(logit + bg_ref[i:i + 1, :]) * proj
    o_ref[...] = x + _dot(merged.astype(BF), wo_ref[...])


def _final(x2, norm_g, y_a, y_b, p3, y_d, y_m, cw_bd, c_scale, w_gate, b_gate, w_branch, w_out, layer, tm=512):
    n = x2.shape[0]
    b, t, _ = p3.shape
    tps = t // tm
    band = _pool_band(POOL_SUB)
    yspec = pl.BlockSpec((tm, BRANCH_W), lambda i: (i, 0))
    return pl.pallas_call(
        functools.partial(_final_kernel, tps),
        grid=(n // tm,),
        in_specs=[
            pl.BlockSpec((tm, D_MODEL), lambda i: (i, 0)),
            _resident((1, D_MODEL), lambda i: (0, 0)),
            yspec, yspec,
            pl.BlockSpec((1, t, BRANCH_W), lambda i: (i // tps, 0, P_POS[S_CX])),
            pl.BlockSpec((tm, BRANCH_W), lambda i: (i, P_POS[S_CZ])),
            yspec, yspec,
            _resident(band.shape, lambda i: (0, 0, 0)),
            _resident((BRANCH_W, BRANCH_W), lambda i: (0, 0)),
            _resident((1, BRANCH_W), lambda i: (0, 0)),
            _layer_resident((D_MODEL, N_BRANCH * D_MODEL), layer),
            _resident((8, D_MODEL), lambda i: (0, 0)),
            _layer_resident((N_BRANCH, BRANCH_W, D_MODEL), layer),
            _layer_resident((D_MODEL, D_MODEL), layer),
        ],
        out_specs=pl.BlockSpec((tm, D_MODEL), lambda i: (i, 0)),
        out_shape=jax.ShapeDtypeStruct((n, D_MODEL), F32),
        compiler_params=_cparams("parallel"),
        name="final",
    )(x2, norm_g, y_a, y_b, p3, p3.reshape(n, P_COLS), y_d, y_m, band, cw_bd, c_scale,
      w_gate, b_gate, w_branch, w_out)


W_IN_CAST_ROWS = 128


def _cast_w_in_kernel(w_ref, slab_ref, gate_ref):
    w = w_ref[0]
    slab_ref[0] = w[:, :SLAB_COLS].astype(BF)
    gate_ref[0] = w[:, SLAB_COLS:].astype(BF)


def _cast_w_in(w_in):
    depth, d, cols = w_in.shape
    gate_cols = cols - SLAB_COLS
    return pl.pallas_call(
        _cast_w_in_kernel,
        grid=(depth, d // W_IN_CAST_ROWS),
        in_specs=[pl.BlockSpec((1, W_IN_CAST_ROWS, cols), lambda l, r: (l, r, 0))],
        out_specs=[pl.BlockSpec((1, W_IN_CAST_ROWS, SLAB_COLS), lambda l, r: (l, r, 0)),
                   pl.BlockSpec((1, W_IN_CAST_ROWS, gate_cols), lambda l, r: (l, r, 0))],
        out_shape=[jax.ShapeDtypeStruct((depth, d, SLAB_COLS), BF),
                   jax.ShapeDtypeStruct((depth, d, gate_cols), BF)],
        compiler_params=_cparams("parallel", "parallel"),
        name="cast_w_in",
    )(w_in)


def _tile4(g):
    return jnp.tile(g.astype(F32), BRANCH_W // g.shape[0])


def _layer(x, mem, layer_idx, stacked, norm_g, b_gate, a_qn_g, a_kn_g, a_rpb,
           b_qn_g, b_kn_g, b_lam_q1, b_lam_k1, b_lam_q2, b_lam_k2, b_sub_g,
           c_w, c_scale, d_ln_g, d_ln_b, d_bs, m_norm_g, m_qn_g, m_kn_g):
    b, t, d = x.shape
    n = b * t
    x2 = x.reshape(n, d)
    ng = norm_g.reshape(1, d).astype(F32)

    gains = jnp.zeros((8, BRANCH_W), F32)
    gains = gains.at[G_AQ].set(_tile4(a_qn_g) * (NA_HD ** -0.5 * LOG2E))
    gains = gains.at[G_AK].set(_tile4(a_kn_g))
    gains = gains.at[G_BQ].set(_tile4(b_qn_g) * (DA_QK ** -0.5 * LOG2E))
    gains = gains.at[G_BK].set(_tile4(b_kn_g))
    gains = gains.at[G_MQ].set(_tile4(m_qn_g) * (MEM_HD ** -0.5 * LOG2E))
    gains = gains.at[G_LNG].set(d_ln_g.astype(F32))
    gains = gains.at[G_LNB].set(d_ln_b.astype(F32))

    mk, mv = _memkv(mem, m_norm_g.reshape(1, d).astype(F32), stacked["m_wkv"], layer_idx,
                    _tile4(m_kn_g).reshape(1, BRANCH_W))
    sg_bias = jnp.repeat(d_bs.astype(F32).T, SG_GC, axis=1)
    p2, y_d, y_m, qt4, k4, vt4 = _proj(x2, ng, stacked["w_slab"], layer_idx, gains,
                                       stacked["d_ws"], sg_bias, mk, mv, t)
    p3 = p2.reshape(b, t, P_COLS)

    y_a = _na(p3, _na_bias_table(a_rpb, t // GRID_W))

    lam_init = 0.8 - 0.6 * math.exp(-0.3 * layer_idx)
    lamv = jnp.stack([b_lam_q1, b_lam_k1, b_lam_q2, b_lam_k2]).astype(F32)
    subg = (_tile4(b_sub_g) * (1.0 - lam_init)).reshape(1, BRANCH_W)
    y_b = _da(qt4, k4, vt4, p3, lamv, subg, lam_init)

    cw_bd = jnp.zeros((BRANCH_W, BRANCH_W), F32)
    for g in range(len(POOL_WINDOWS)):
        cw_bd = cw_bd.at[g * POOL_GC:(g + 1) * POOL_GC, g * POOL_GC:(g + 1) * POOL_GC].set(c_w[g].astype(F32))

    bg = jnp.zeros((8, d), F32).at[:N_BRANCH].set(b_gate.astype(F32))
    out = _final(x2, ng, y_a.reshape(n, BRANCH_W), y_b.reshape(n, BRANCH_W), p3, y_d, y_m,
                 cw_bd.astype(BF), c_scale.reshape(1, BRANCH_W).astype(F32),
                 stacked["w_gate"], bg, stacked["w_branch"], stacked["w_out"], layer_idx)
    return out.reshape(b, t, d)


def kernel(x, mem, norm_g, w_in, b_gate, a_qn_g, a_kn_g, a_rpb, b_qn_g, b_kn_g, b_lam_q1, b_lam_k1, b_lam_q2, b_lam_k2, b_sub_g, c_w, c_scale, d_ln_g, d_ln_b, d_ws, d_bs, m_norm_g, m_wkv, m_qn_g, m_kn_g, w_branch, w_out):
    w_slab, w_gate = _cast_w_in(w_in)
    stacked = {
        "w_slab": w_slab,
        "w_gate": w_gate,
        "w_branch": w_branch.astype(BF),
        "w_out": w_out.astype(BF),
        "m_wkv": m_wkv.astype(BF),
        "d_ws": d_ws.astype(BF),
    }
    for l in range(DEPTH):
        x = _layer(x, mem, l, stacked, norm_g[l], b_gate[l], a_qn_g[l], a_kn_g[l], a_rpb[l],
                   b_qn_g[l], b_kn_g[l], b_lam_q1[l], b_lam_k1[l], b_lam_q2[l], b_lam_k2[l], b_sub_g[l],
                   c_w[l], c_scale[l], d_ln_g[l], d_ln_b[l], d_bs[l], m_norm_g[l], m_qn_g[l], m_kn_g[l])
    return x
```

```python
import functools
import math

import numpy as np
import jax
import jax.numpy as jnp
from jax import lax
from jax.experimental import pallas as pl
from jax.experimental.pallas import tpu as pltpu

F32 = jnp.float32
BF = jnp.bfloat16

D_MODEL = 1024
DEPTH = 2
GRID_W = 64
BRANCH_W = 256
N_BRANCH = 5
NA_HEADS = 4
NA_HD = 64
NA_KH = 8
NA_KW = 16
DA_HEADS = 4
DA_QK = 32
DA_V = 64
ALIBI_BASE = 8.0
POOL_WINDOWS = (2, 4, 8, 16)
POOL_GC = 64
SG_CHUNK = 128
SG_GROUPS = 4
SG_GC = 64
MEM_HEADS = 4
MEM_HD = 64
N_SLABS = 15
SLAB_COLS = N_SLABS * BRANCH_W
EPS = 1e-6

(S_AQ, S_AK, S_AV, S_AZ, S_BQ, S_BK, S_BV, S_BZ, S_CX, S_CZ,
 S_DU, S_DV, S_DZ, S_MQ, S_MZ) = range(N_SLABS)

(G_AQ, G_AK, G_BQ, G_BK, G_MQ, G_LNG, G_LNB) = range(7)

VMEM_LIMIT_BYTES = 56 * 1024 * 1024
NEG_BIG = -1e30

ALIBI_SLOPES = tuple(2.0 ** (-ALIBI_BASE * (h + 1) / DA_HEADS) for h in range(DA_HEADS))
DA_LANES = 128


def _bf16_pieces(x, n):
    out = []
    for _ in range(n):
        piece = float(np.asarray(x, np.float32).astype(jnp.bfloat16).astype(np.float32))
        out.append(piece)
        x -= piece
    return tuple(out)


LOG2E = math.log2(math.e)
LOG2E_PIECES = _bf16_pieces(LOG2E, 3)
N_PIECES = len(LOG2E_PIECES)
X_KHI, X_KLO, X_QHI, X_QLO = (DA_V + g * N_PIECES for g in range(4))
X_END = DA_V + 4 * N_PIECES
X_ROWS = 16
POS_SPLIT = 64


def _alibi_parts(pos, slope):
    hi = (pos >> _log2(POS_SPLIT)).astype(F32) * (slope * POS_SPLIT)
    lo = (pos & (POS_SPLIT - 1)).astype(F32) * slope
    return hi, lo


def _piece_of(idx):
    out = jnp.zeros(idx.shape, F32)
    for g in range(4):
        for p, piece in enumerate(LOG2E_PIECES):
            out = jnp.where(idx == DA_V + g * N_PIECES + p, piece, out)
    return out


def _cparams(*sem):
    return pltpu.CompilerParams(dimension_semantics=sem, vmem_limit_bytes=VMEM_LIMIT_BYTES)


def _resident(shape, index_map):
    return pl.BlockSpec(shape, index_map, pipeline_mode=pl.Buffered(1))


def _layer_resident(shape, layer):
    return pl.BlockSpec((None,) + tuple(shape), lambda *_: (layer,) + (0,) * len(shape),
                        pipeline_mode=pl.Buffered(1))


def _dot(a, b):
    return jnp.dot(a, b, preferred_element_type=F32)


def _dot_nt(a, b):
    return lax.dot_general(a, b, (((1,), (1,)), ((), ())), preferred_element_type=F32)


def _log2(n):
    assert n & (n - 1) == 0, "power of two expected"
    return n.bit_length() - 1


def _lane_group(shape, width):
    return lax.broadcasted_iota(jnp.int32, shape, len(shape) - 1) >> _log2(width)


def _lane_mask(shape, lo, hi, dtype):
    lane = lax.broadcasted_iota(jnp.int32, shape, len(shape) - 1)
    return jnp.where((lane >= lo) & (lane < hi), 1.0, 0.0).astype(dtype)


def _group_mean_sq(xf, group):
    n = xf.shape[-1]
    r = _lane_group((n, n), group)
    c = lax.broadcasted_iota(jnp.int32, (n, n), 0) >> _log2(group)
    pm = jnp.where(r == c, 1.0 / group, 0.0).astype(BF)
    return _dot((xf * xf).astype(BF), pm)


def _sigmoid(x):
    return 1.0 / (1.0 + jnp.exp(-x))


def _silu(x):
    return x * _sigmoid(x)


def _gelu_tanh(x):
    return 0.5 * x * (1.0 + jnp.tanh(math.sqrt(2.0 / math.pi) * (x + 0.044715 * (x * x * x))))


def _rmsnorm_rows(x, g):
    ms = jnp.mean(x * x, axis=-1, keepdims=True)
    return x * lax.rsqrt(ms + EPS) * g


PROJ_LOOKAHEAD = 3
P_SLABS = (S_AQ, S_AK, S_AV, S_AZ, S_BZ, S_CX, S_CZ)
P_POS = {s: i for i, s in enumerate(P_SLABS)}
P_COLS = len(P_SLABS) * BRANCH_W
PROJ_ORDER = (S_MQ, S_MZ, S_DU, S_DV, S_DZ, S_BQ, S_BK, S_BV) + P_SLABS


def _proj_kernel(tiles_per_seq, x_ref, ng_ref, w_ref, gains_ref, ws_ref, sgb_ref, mk_ref, mv_ref,
                 p_ref, yd_ref, ym_ref, qt_ref, k4_ref, vt_ref):
    tm = x_ref.shape[0]
    t0 = (pl.program_id(0) % tiles_per_seq) * tm
    h = _rmsnorm_rows(x_ref[...], ng_ref[...]).astype(BF)

    def gnorm(r, group, row):
        return r * lax.rsqrt(_group_mean_sq(r, group) + EPS) * gains_ref[row:row + 1, :]

    def slab(j):
        return _dot(h, w_ref[:, j * BRANCH_W:(j + 1) * BRANCH_W])

    kept = {}
    pending = [slab(j) for j in PROJ_ORDER[:PROJ_LOOKAHEAD]]
    for idx, j in enumerate(PROJ_ORDER):
        if idx + PROJ_LOOKAHEAD < N_SLABS:
            pending.append(slab(PROJ_ORDER[idx + PROJ_LOOKAHEAD]))
        r = pending.pop(0)
        if j == S_AQ:
            r = gnorm(r, NA_HD, G_AQ)
        elif j == S_AK:
            r = gnorm(r, NA_HD, G_AK)
        elif j == S_BQ:
            r = gnorm(r, DA_QK, G_BQ)
        elif j == S_BK:
            r = gnorm(r, DA_QK, G_BK)
        elif j == S_MQ:
            r = gnorm(r, MEM_HD, G_MQ)
        elif j in (S_AZ, S_BZ, S_CZ, S_DZ, S_MZ):
            r = _silu(r)
        elif j == S_DU:
            r = _gelu_tanh(r)
        elif j == S_DV:
            v = _gelu_tanh(r)
            mu = jnp.mean(v, axis=-1, keepdims=True)
            vc = v - mu
            var = jnp.mean(vc * vc, axis=-1, keepdims=True)
            r = vc * lax.rsqrt(var + EPS) * gains_ref[G_LNG:G_LNG + 1, :] + gains_ref[G_LNB:G_LNB + 1, :]
        if j in P_POS:
            p_ref[:, P_POS[j] * BRANCH_W:(P_POS[j] + 1) * BRANCH_W] = r.astype(BF)
        elif j == S_MZ:
            ym_ref[...] = _mem_attend(kept.pop(S_MQ), r, mk_ref[0], mv_ref[0]).astype(BF)
        elif j == S_DZ:
            _spatial_gate(kept.pop(S_DU), kept.pop(S_DV), r, ws_ref, sgb_ref, yd_ref)
        elif j == S_BV:
            _da_layouts(kept.pop(S_BQ), kept.pop(S_BK), r.astype(BF), t0, qt_ref, k4_ref, vt_ref)
        elif j in (S_MQ, S_DV, S_BQ, S_BK):
            kept[j] = r.astype(BF)
        else:
            kept[j] = r


def _proj(x2, norm_g, w_slab, layer, gains, ws, sg_bias, mk, mv, seq_len, tm=512):
    n = x2.shape[0]
    b = n // seq_len
    tps = seq_len // tm
    m = mk.shape[1]
    return pl.pallas_call(
        functools.partial(_proj_kernel, tps),
        grid=(n // tm,),
        in_specs=[
            pl.BlockSpec((tm, D_MODEL), lambda i: (i, 0)),
            _resident((1, D_MODEL), lambda i: (0, 0)),
            _layer_resident((D_MODEL, SLAB_COLS), layer),
            _resident((8, BRANCH_W), lambda i: (0, 0)),
            _layer_resident((SG_GROUPS, SG_CHUNK, SG_CHUNK), layer),
            _resident((SG_CHUNK, BRANCH_W), lambda i: (0, 0)),
            pl.BlockSpec((1, m, BRANCH_W), lambda i: (i // tps, 0, 0)),
            pl.BlockSpec((1, m, BRANCH_W), lambda i: (i // tps, 0, 0)),
        ],
        out_specs=[
            pl.BlockSpec((tm, P_COLS), lambda i: (i, 0)),
            pl.BlockSpec((tm, BRANCH_W), lambda i: (i, 0)),
            pl.BlockSpec((tm, BRANCH_W), lambda i: (i, 0)),
            pl.BlockSpec((1, DA_HEADS, DA_V, tm), lambda i: (i // tps, 0, 0, i % tps)),
            pl.BlockSpec((1, DA_HEADS, tm, DA_LANES), lambda i: (i // tps, 0, i % tps, 0)),
            pl.BlockSpec((1, DA_HEADS, DA_VT_ROWS, tm), lambda i: (i // tps, 0, 0, i % tps)),
        ],
        out_shape=[
            jax.ShapeDtypeStruct((n, P_COLS), BF),
            jax.ShapeDtypeStruct((n, BRANCH_W), BF),
            jax.ShapeDtypeStruct((n, BRANCH_W), BF),
            jax.ShapeDtypeStruct((b, DA_HEADS, DA_V, seq_len), BF),
            jax.ShapeDtypeStruct((b, DA_HEADS, seq_len, DA_LANES), BF),
            jax.ShapeDtypeStruct((b, DA_HEADS, DA_VT_ROWS, seq_len), BF),
        ],
        compiler_params=_cparams("parallel"),
        name="proj",
    )(x2, norm_g, w_slab, gains, ws, sg_bias, mk, mv)


def _memkv_kernel(mem_ref, g_ref, w_ref, kg_ref, k_ref, v_ref):
    h = _rmsnorm_rows(mem_ref[0], g_ref[...]).astype(BF)
    k = _dot(h, w_ref[:, :BRANCH_W])
    v = _dot(h, w_ref[:, BRANCH_W:])
    k = k * lax.rsqrt(_group_mean_sq(k, MEM_HD) + EPS) * kg_ref[...]
    k_ref[0] = k.astype(BF)
    v_ref[0] = v.astype(BF)


def _memkv(mem, m_norm_g, m_wkv, layer, kg):
    b, m, _ = mem.shape
    return pl.pallas_call(
        _memkv_kernel,
        grid=(b,),
        in_specs=[
            pl.BlockSpec((1, m, D_MODEL), lambda i: (i, 0, 0)),
            _resident((1, D_MODEL), lambda i: (0, 0)),
            _layer_resident((D_MODEL, 2 * BRANCH_W), layer),
            _resident((1, BRANCH_W), lambda i: (0, 0)),
        ],
        out_specs=[pl.BlockSpec((1, m, BRANCH_W), lambda i: (i, 0, 0)),
                   pl.BlockSpec((1, m, BRANCH_W), lambda i: (i, 0, 0))],
        out_shape=[jax.ShapeDtypeStruct((b, m, BRANCH_W), BF)] * 2,
        compiler_params=_cparams("parallel"),
        name="memkv",
    )(mem, m_norm_g, m_wkv, kg)


NA_ROWS_PER_STEP = 8
NA_WIN = NA_KH * GRID_W
NA_LOOKAHEAD = 3


def _na_kernel(q_ref, k_ref, v_ref, z_ref, bias_ref, o_ref):
    n_rows = k_ref.shape[1] // GRID_W
    rb = pl.program_id(1)
    head_of_lane = _lane_group((GRID_W, BRANCH_W), NA_HD)
    masks = [_lane_mask((GRID_W, BRANCH_W), h * NA_HD, (h + 1) * NA_HD, BF) for h in range(NA_HEADS)]

    def window(rr):
        r = rb * NA_ROWS_PER_STEP + rr
        rs = jnp.clip(r - NA_KH // 2, 0, n_rows - NA_KH)
        cls = jnp.where(r < NA_KH // 2, r, jnp.where(r > n_rows - NA_KH // 2, r - (n_rows - NA_KH), NA_KH // 2))
        return pl.multiple_of(rs * GRID_W, GRID_W), cls

    def scores(rr):
        w0, cls = window(rr)
        q = q_ref[0, rr * GRID_W:(rr + 1) * GRID_W, :]
        qs = jnp.concatenate([q * masks[h] for h in range(NA_HEADS)], axis=0)
        return _dot_nt(qs, k_ref[0, pl.ds(w0, NA_WIN), :]) + bias_ref[cls]

    pending = [scores(rr) for rr in range(NA_LOOKAHEAD)]
    for rr in range(NA_ROWS_PER_STEP):
        if rr + NA_LOOKAHEAD < NA_ROWS_PER_STEP:
            pending.append(scores(rr + NA_LOOKAHEAD))
        s = pending.pop(0)
        w0, _ = window(rr)
        m = jnp.max(s, axis=-1, keepdims=True)
        p = jnp.exp2(s - m)
        l = jnp.sum(p, axis=-1, keepdims=True)
        pv = _dot(p.astype(BF), v_ref[0, pl.ds(w0, NA_WIN), :]) / l
        o = jnp.zeros((GRID_W, BRANCH_W), F32)
        for h in range(NA_HEADS):
            o = jnp.where(head_of_lane == h, pv[h * GRID_W:(h + 1) * GRID_W], o)
        rows = slice(rr * GRID_W, (rr + 1) * GRID_W)
        o_ref[0, rows, :] = (o * z_ref[0, rows, :].astype(F32)).astype(BF)


def _na(p3, bias):
    b, t, _ = p3.shape
    tq = NA_ROWS_PER_STEP * GRID_W
    return pl.pallas_call(
        _na_kernel,
        grid=(b, t // tq),
        in_specs=[
            pl.BlockSpec((1, tq, BRANCH_W), lambda i, j: (i, j, P_POS[S_AQ])),
            pl.BlockSpec((1, t, BRANCH_W), lambda i, j: (i, 0, P_POS[S_AK])),
            pl.BlockSpec((1, t, BRANCH_W), lambda i, j: (i, 0, P_POS[S_AV])),
            pl.BlockSpec((1, tq, BRANCH_W), lambda i, j: (i, j, P_POS[S_AZ])),
            _resident(bias.shape, lambda i, j: (0, 0, 0)),
        ],
        out_specs=pl.BlockSpec((1, tq, BRANCH_W), lambda i, j: (i, j, 0)),
        out_shape=jax.ShapeDtypeStruct((b, t, BRANCH_W), BF),
        compiler_params=_cparams("parallel", "arbitrary"),
        name="na",
    )(p3, p3, p3, p3, bias)


NA_REL_ROWS = 2 * NA_KH - 1
NA_REL_COLS = 2 * NA_KW - 1
NA_CLASSES = NA_KH


def _na_class_roff(n_rows):
    half = NA_KH // 2
    rows = np.array(list(range(half)) + [half] + [n_rows - half + 1 + i for i in range(half - 1)])
    rs = np.clip(rows - half, 0, n_rows - NA_KH)
    return [int(v) for v in rs - rows + NA_KH - 1]


def _nabias_kernel(roff0, rpb_ref, o_ref, f_scr):
    h = pl.program_id(0)
    shape = (GRID_W, 2 * GRID_W)
    q = lax.broadcasted_iota(jnp.int32, shape, 0)
    lane = lax.broadcasted_iota(jnp.int32, shape, 1)
    k = lane & (GRID_W - 1)
    coff = k - q + (NA_KW - 1)
    cs = jnp.clip(q - NA_KW // 2, 0, GRID_W - NA_KW)
    valid = (k >= cs) & (k < cs + NA_KW)
    first = lane < GRID_W

    def tile(r, carry):
        base = (h * NA_REL_ROWS + r) * NA_REL_COLS
        acc = jnp.zeros(shape, F32)
        for s in range(NA_REL_COLS):
            acc = jnp.where(coff == s, jnp.where(first, rpb_ref[base + s], rpb_ref[base + NA_REL_COLS + s]), acc)
        f_scr[r] = jnp.where(valid, acc * LOG2E, NEG_BIG)
        return carry

    lax.fori_loop(0, NA_REL_ROWS - 1, tile, 0)
    for c in range(NA_CLASSES):
        for p in range(NA_KH // 2):
            o_ref[c, 0, :, p * 2 * GRID_W:(p + 1) * 2 * GRID_W] = f_scr[roff0[c] + 2 * p]


def _na_bias_table(rpb, n_rows):
    return pl.pallas_call(
        functools.partial(_nabias_kernel, _na_class_roff(n_rows)),
        grid=(NA_HEADS,),
        in_specs=[pl.BlockSpec(memory_space=pltpu.SMEM)],
        out_specs=pl.BlockSpec((NA_CLASSES, 1, GRID_W, NA_WIN), lambda h: (0, h, 0, 0)),
        out_shape=jax.ShapeDtypeStruct((NA_CLASSES, NA_HEADS, GRID_W, NA_WIN), F32),
        scratch_shapes=[pltpu.VMEM((NA_REL_ROWS - 1, GRID_W, 2 * GRID_W), F32)],
        compiler_params=_cparams("arbitrary"),
        name="nabias",
    )(rpb.astype(F32).reshape(-1)).reshape(NA_CLASSES, NA_HEADS * GRID_W, NA_WIN)


DA_VT_ROWS = 80
X_SUM = DA_V


def _da_layouts(q, k, v, t0, qt_ref, k4_ref, vt_ref):
    tt = q.shape[0]
    src = lax.broadcasted_iota(jnp.int32, (BRANCH_W, DA_HEADS * DA_LANES), 0)
    dst = lax.broadcasted_iota(jnp.int32, (BRANCH_W, DA_HEADS * DA_LANES), 1)
    dst_head, dst_lane = dst >> _log2(DA_LANES), dst & (DA_LANES - 1)
    spread = jnp.where((src == dst_head * DA_V + dst_lane) & (dst_lane < DA_V), 1.0, 0.0).astype(BF)
    k_all = _dot(k, spread)
    src_t = lax.broadcasted_iota(jnp.int32, (DA_LANES, BRANCH_W), 1)
    dst_t = lax.broadcasted_iota(jnp.int32, (DA_LANES, BRANCH_W), 0)
    klane = lax.broadcasted_iota(jnp.int32, (tt, DA_LANES), 1)
    kpos = t0 + lax.broadcasted_iota(jnp.int32, (tt, DA_LANES), 0)
    vrow = lax.broadcasted_iota(jnp.int32, (DA_VT_ROWS, tt), 0)
    for h in range(DA_HEADS):
        sel_t = jnp.where((src_t == h * DA_V + dst_t) & (dst_t < DA_V), 1.0, 0.0).astype(BF)
        qt_ref[0, h] = _dot_nt(sel_t[:DA_V], q).astype(BF)
        vt = _dot_nt(sel_t[:DA_VT_ROWS], v)
        vt_ref[0, h] = jnp.where(vrow == X_SUM, 1.0, vt).astype(BF)
        slope = ALIBI_SLOPES[h]
        k4 = k_all[:, h * DA_LANES:(h + 1) * DA_LANES]
        jh, jl = _alibi_parts(kpos, slope)
        k4 = jnp.where((klane >= X_KHI) & (klane < X_KLO), jh, k4)
        k4 = jnp.where((klane >= X_KLO) & (klane < X_QHI), jl, k4)
        k4 = jnp.where((klane >= X_QHI) & (klane < X_END), _piece_of(klane), k4)
        k4_ref[0, h] = k4.astype(BF)


DA_QCOLS = 256
DA_KROWS = 256
DA_LOOKAHEAD = 6
Q_LEFT, Q_DIAG, Q_RIGHT = range(3)


def _da_kernel(lam_init, qt_ref, k_ref, vt_ref, z_ref, lamv_ref, subg_ref, o_ref,
               qv_scr, dist_scr, spre_scr, m_scr, acc_scr):
    bq = qt_ref.shape[3]
    bk = bq
    nk = k_ref.shape[2] // bk
    qi = pl.program_id(1)
    frow = lax.broadcasted_iota(jnp.int32, (DA_V, bq), 0)
    xrow = DA_V + lax.broadcasted_iota(jnp.int32, (X_ROWS, bq), 0)
    qpos = qi * bq + lax.broadcasted_iota(jnp.int32, (X_ROWS, bq), 1)

    lv = lamv_ref[...]
    lam = (jnp.exp(jnp.sum(lv[0:1] * lv[1:2], axis=-1, keepdims=True))
           - jnp.exp(jnp.sum(lv[2:3] * lv[3:4], axis=-1, keepdims=True)) + lam_init)

    cols = [slice(c * DA_QCOLS, (c + 1) * DA_QCOLS) for c in range(bq // DA_QCOLS)]
    chains = [(h, m, c) for h in range(DA_HEADS) for m in range(2) for c in range(len(cols))]

    def tiles_of(blk):
        return [(blk, kc * DA_KROWS, ch) for kc in range(bk // DA_KROWS) for ch in chains]

    def scores(tile):
        (k0, side, diag), r0, (h, m, c) = tile
        if diag:
            c0 = c * DA_QCOLS
            side = Q_DIAG if r0 == c0 else (Q_LEFT if r0 < c0 else Q_RIGHT)
        s = _dot(k_ref[0, h, pl.ds(k0 + r0, DA_KROWS), :], qv_scr[h, side, m, :, cols[c]])
        if diag and r0 == c0:
            s = s - dist_scr[...] * (ALIBI_SLOPES[h] * LOG2E)
        return s

    def run_block(blk, nxt):
        tiles = tiles_of(blk)
        for i, ((k0, _, _), r0, (h, m, c)) in enumerate(tiles):
            j = i + DA_LOOKAHEAD
            if j < len(tiles):
                spre_scr[j] = scores(tiles[j])
            elif nxt is not None:
                spre_scr[j - len(tiles)] = scores(tiles_of(nxt)[j - len(tiles)])
            s = spre_scr[i]
            m_old = m_scr[h, m, :, cols[c]]
            m_new = jnp.maximum(m_old, jnp.max(s, axis=0, keepdims=True))
            alpha = jnp.exp2(m_old - m_new)
            p = jnp.exp2(s - m_new).astype(BF)
            pv = _dot(vt_ref[0, h, :, pl.ds(k0 + r0, DA_KROWS)], p)
            acc_scr[h, m, :, cols[c]] = alpha * acc_scr[h, m, :, cols[c]] + pv
            m_scr[h, m, :, cols[c]] = m_new

    assert DA_KROWS == DA_QCOLS, "diagonal tiles are taken to be square"
    rel = (lax.broadcasted_iota(jnp.int32, (DA_KROWS, DA_QCOLS), 0)
           - lax.broadcasted_iota(jnp.int32, (DA_KROWS, DA_QCOLS), 1))
    dist_scr[...] = jnp.abs(rel).astype(F32)
    pieces = _piece_of(xrow)
    tail = jnp.zeros((DA_LANES - DA_V - X_ROWS, bq), BF)
    for h in range(DA_HEADS):
        slope = ALIBI_SLOPES[h]
        ih, il = _alibi_parts(qpos, slope)
        ex =jnp.where(xrow < X_QHI, pieces,
                       jnp.where(xrow < X_QLO, -ih, jnp.where(xrow < X_END, -il, 0.0))).astype(BF)
        for m in range(2):
            qm = qt_ref[0, h] * jnp.where(frow >> _log2(DA_QK) == m, 1.0, 0.0).astype(BF)
            for side, extra in ((Q_LEFT, ex), (Q_DIAG, jnp.zeros_like(ex)), (Q_RIGHT, -ex)):
                qv_scr[h, side, m, :DA_V, :] = qm
                qv_scr[h, side, m, DA_V:DA_V + X_ROWS, :] = extra
                qv_scr[h, side, m, DA_V + X_ROWS:, :] = tail
    m_scr[...] = jnp.full(m_scr.shape, NEG_BIG, F32)
    acc_scr[...] = jnp.zeros(acc_scr.shape, F32)

    def off_diagonal(t):
        kb = jnp.where(t >= qi, t + 1, t)
        return pl.multiple_of(kb * bk, bk), jnp.where(t >= qi, Q_RIGHT, Q_LEFT), False

    diagonal = (pl.multiple_of(qi * bk, bk), Q_DIAG, True)

    for i, tile in enumerate(tiles_of(off_diagonal(0))[:DA_LOOKAHEAD]):
        spre_scr[i] = scores(tile)

    def body(t, carry):
        run_block(off_diagonal(t), off_diagonal(t + 1))
        return carry

    lax.fori_loop(0, nk - 2, body, 0)
    run_block(off_diagonal(nk - 2), diagonal)
    run_block(diagonal, None)

    heads = []
    for h in range(DA_HEADS):
        o = [acc_scr[h, m, :DA_V, :] / acc_scr[h, m, X_SUM:X_SUM + 1, :] for m in range(2)]
        heads.append(o[0] - lam * o[1])

    full = jnp.concatenate(heads, axis=0).T
    y = full * lax.rsqrt(_group_mean_sq(full, DA_V) + EPS) * subg_ref[...]
    o_ref[0] = (y * z_ref[0].astype(F32)).astype(BF)


def _da(qt4, k4, vt4, p3, lamv, subg, lam_init, bq=512):
    b, _, t, _ = k4.shape
    assert all(math.frexp(s)[0] == 0.5 for s in ALIBI_SLOPES), "ALiBi slopes must be powers of two"
    assert t <= POS_SPLIT * 256 and POS_SPLIT <= 256
    return pl.pallas_call(
        functools.partial(_da_kernel, lam_init),
        grid=(b, t // bq),
        in_specs=[
            pl.BlockSpec((1, DA_HEADS, DA_V, bq), lambda i, j: (i, 0, 0, j)),
            pl.BlockSpec((1, DA_HEADS, t, DA_LANES), lambda i, j: (i, 0, 0, 0)),
            pl.BlockSpec((1, DA_HEADS, DA_VT_ROWS, t), lambda i, j: (i, 0, 0, 0)),
            pl.BlockSpec((1, bq, BRANCH_W), lambda i, j: (i, j, P_POS[S_BZ])),
            _resident((4, DA_QK), lambda i, j: (0, 0)),
            _resident((1, BRANCH_W), lambda i, j: (0, 0)),
        ],
        out_specs=pl.BlockSpec((1, bq, BRANCH_W), lambda i, j: (i, j, 0)),
        out_shape=jax.ShapeDtypeStruct((b, t, BRANCH_W), BF),
        scratch_shapes=[
            pltpu.VMEM((DA_HEADS, 3, 2, DA_LANES, bq), BF),
            pltpu.VMEM((DA_KROWS, DA_QCOLS), F32),
            pltpu.VMEM((2 * DA_HEADS * 2 * (bq // DA_QCOLS), DA_KROWS, DA_QCOLS), F32),
            pltpu.VMEM((DA_HEADS, 2, 1, bq), F32),
            pltpu.VMEM((DA_HEADS, 2, DA_VT_ROWS, bq), F32),
        ],
        compiler_params=_cparams("parallel", "arbitrary"),
        name="da",
    )(qt4, k4, vt4, p3, lamv, subg)


POOL_HALO = 16


def _pool_window_sums(x_ref, tt, ti, band_ref):
    t_total = x_ref.shape[1]
    nt = t_total // tt
    t0 = pl.multiple_of(ti * tt, tt)
    xm = x_ref[0, pl.ds(t0, tt), :]
    p0 = pl.multiple_of(jnp.maximum(t0 - POOL_HALO, 0), POOL_HALO)
    n0 = pl.multiple_of(jnp.minimum(t0 + tt, t_total - POOL_HALO), POOL_HALO)
    has_prev = (ti > 0).astype(F32)
    has_next = (ti < nt - 1).astype(F32)
    prev = (x_ref[0, pl.ds(p0, POOL_HALO), :].astype(F32) * has_prev).astype(BF)
    nxt = (x_ref[0, pl.ds(n0, POOL_HALO), :].astype(F32) * has_next).astype(BF)
    xcat = jnp.concatenate([prev, xm, nxt], axis=0)

    sub = band_ref.shape[1]
    sums = [[_dot(band_ref[g], xcat[r0:r0 + sub + 2 * POOL_HALO]) for g in range(len(POOL_WINDOWS))]
            for r0 in range(0, tt, sub)]
    return xm, sums


def _pool_finish(xm, sums, ti, t_total, z_ref, cw_ref, cs_ref):
    tt = xm.shape[0]
    sub = tt // len(sums)
    t0 = ti * tt
    group = _lane_group((sub, BRANCH_W), POOL_GC)
    half = jnp.left_shift(1, group)
    out = []
    for i, r0 in enumerate(range(0, tt, sub)):
        wsum = jnp.zeros((sub, BRANCH_W), F32)
        for g in range(len(POOL_WINDOWS)):
            wsum = jnp.where(group == g, sums[i][g], wsum)
        tpos = t0 + r0 + lax.broadcasted_iota(jnp.int32, (sub, BRANCH_W), 0)
        lo = jnp.maximum(tpos - half, 0)
        hi = jnp.minimum(tpos + half - 1, t_total - 1)
        cnt = (hi - lo + 1).astype(F32)
        dlt = wsum / cnt - xm[r0:r0 + sub].astype(F32)
        y = _dot(dlt.astype(BF), cw_ref[...]) * cs_ref[...]
        out.append((y * z_ref[r0:r0 + sub, :].astype(F32)).astype(BF))
    return jnp.concatenate(out, axis=0)


def _pool_band(tt):
    band = np.zeros((len(POOL_WINDOWS), tt, tt + 2 * POOL_HALO), np.float32)
    t = np.arange(tt)[:, None]
    u = np.arange(tt + 2 * POOL_HALO)[None, :] - POOL_HALO
    for g, w in enumerate(POOL_WINDOWS):
        band[g] = ((u >= t - w // 2) & (u <= t - w // 2 + w - 1)).astype(np.float32)
    return jnp.asarray(band, BF)


POOL_SUB = 256


def _spatial_gate(u, vn, z, ws_ref, bias_ref, o_ref):
    group = _lane_group((SG_CHUNK, BRANCH_W), SG_GC)
    for c in range(u.shape[0] // SG_CHUNK):
        rows = slice(c * SG_CHUNK, (c + 1) * SG_CHUNK)
        mix = jnp.zeros((SG_CHUNK, BRANCH_W), F32)
        for g in range(SG_GROUPS):
            mix = jnp.where(group == g, _dot(ws_ref[g], vn[rows]), mix)
        o_ref[rows, :] = (u[rows] * (mix + bias_ref[...]) * z[rows]).astype(BF)


def _mem_attend(q, z, k, v):
    head_of_lane = _lane_group(q.shape, MEM_HD)
    scores = [_dot_nt(q * _lane_mask(q.shape, h * MEM_HD, (h + 1) * MEM_HD, BF), k) for h in range(MEM_HEADS)]
    o = jnp.zeros(q.shape, F32)
    for h in range(MEM_HEADS):
        s = scores[h]
        m = jnp.max(s, axis=-1, keepdims=True)
        p = jnp.exp2(s - m)
        l = jnp.sum(p, axis=-1, keepdims=True)
        oh = _dot(p.astype(BF), v) / l
        o = jnp.where(head_of_lane == h, oh, o)
    return o * z


def _final_kernel(tiles_per_seq, x_ref, ng_ref, ya_ref, yb_ref, cx_ref, cz_ref, yd_ref, ym_ref,
                  band_ref, cw_ref, cs_ref, wg_ref, bg_ref, wb_ref, wo_ref, o_ref):
    x = x_ref[...]
    ti = pl.program_id(0) % tiles_per_seq
    xm, sums = _pool_window_sums(cx_ref, x.shape[0], ti, band_ref)
    y_c = _pool_finish(xm, sums, ti, cx_ref.shape[1], cz_ref, cw_ref, cs_ref)
    h = _rmsnorm_rows(x, ng_ref[...]).astype(BF)
    merged = jnp.zeros(x.shape, F32)
    for i, y in enumerate((ya_ref[...], yb_ref[...], y_c, yd_ref[...], ym_ref[...])):
        logit = _dot(h, wg_ref[:, i * D_MODEL:(i + 1) * D_MODEL]) + bg_ref[i:i + 1, :]
        merged = merged + _sigmoid(logit) * _dot(y, wb_ref[i])
    o_ref[...] = x + _dot(merged.astype(BF), wo_ref[...])


def _final(x2, norm_g, y_a, y_b, p3, y_d, y_m, cw_bd, c_scale, w_gate, b_gate, w_branch, w_out, layer, tm=512):
    n = x2.shape[0]
    b, t, _ = p3.shape
    tps = t // tm
    band = _pool_band(POOL_SUB)
    yspec = pl.BlockSpec((tm, BRANCH_W), lambda i: (i, 0))
    return pl.pallas_call(
        functools.partial(_final_kernel, tps),
        grid=(n // tm,),
        in_specs=[
            pl.BlockSpec((tm, D_MODEL), lambda i: (i, 0)),
            _resident((1, D_MODEL), lambda i: (0, 0)),
            yspec, yspec,
            pl.BlockSpec((1, t, BRANCH_W), lambda i: (i // tps, 0, P_POS[S_CX])),
            pl.BlockSpec((tm, BRANCH_W), lambda i: (i, P_POS[S_CZ])),
            yspec, yspec,
            _resident(band.shape, lambda i: (0, 0, 0)),
            _resident((BRANCH_W, BRANCH_W), lambda i: (0, 0)),
            _resident((1, BRANCH_W), lambda i: (0, 0)),
            _layer_resident((D_MODEL, N_BRANCH * D_MODEL), layer),
            _resident((8, D_MODEL), lambda i: (0, 0)),
            _layer_resident((N_BRANCH, BRANCH_W, D_MODEL), layer),
            _layer_resident((D_MODEL, D_MODEL), layer),
        ],
        out_specs=pl.BlockSpec((tm, D_MODEL), lambda i: (i, 0)),
        out_shape=jax.ShapeDtypeStruct((n, D_MODEL), F32),
        compiler_params=_cparams("parallel"),
        name="final",
    )(x2, norm_g, y_a, y_b, p3, p3.reshape(n, P_COLS), y_d, y_m, band, cw_bd, c_scale,
      w_gate, b_gate, w_branch, w_out)


W_IN_CAST_ROWS = 128


def _cast_w_in_kernel(w_ref, slab_ref, gate_ref):
    w = w_ref[0]
    slab_ref[0] = w[:, :SLAB_COLS].astype(BF)
    gate_ref[0] = w[:, SLAB_COLS:].astype(BF)


def _cast_w_in(w_in):
    depth, d, cols = w_in.shape
    gate_cols = cols - SLAB_COLS
    return pl.pallas_call(
        _cast_w_in_kernel,
        grid=(depth, d // W_IN_CAST_ROWS),
        in_specs=[pl.BlockSpec((1, W_IN_CAST_ROWS, cols), lambda l, r: (l, r, 0))],
        out_specs=[pl.BlockSpec((1, W_IN_CAST_ROWS, SLAB_COLS), lambda l, r: (l, r, 0)),
                   pl.BlockSpec((1, W_IN_CAST_ROWS, gate_cols), lambda l, r: (l, r, 0))],
        out_shape=[jax.ShapeDtypeStruct((depth, d, SLAB_COLS), BF),
                   jax.ShapeDtypeStruct((depth, d, gate_cols), BF)],
        compiler_params=_cparams("parallel", "parallel"),
        name="cast_w_in",
    )(w_in)


def _tile4(g):
    return jnp.tile(g.astype(F32), BRANCH_W // g.shape[0])


def _layer(x, mem, layer_idx, stacked, norm_g, b_gate, a_qn_g, a_kn_g, a_rpb,
           b_qn_g, b_kn_g, b_lam_q1, b_lam_k1, b_lam_q2, b_lam_k2, b_sub_g,
           c_w, c_scale, d_ln_g, d_ln_b, d_bs, m_norm_g, m_qn_g, m_kn_g):
    b, t, d = x.shape
    n = b * t
    x2 = x.reshape(n, d)
    ng = norm_g.reshape(1, d).astype(F32)

    gains = jnp.zeros((8, BRANCH_W), F32)
    gains = gains.at[G_AQ].set(_tile4(a_qn_g) * (NA_HD ** -0.5 * LOG2E))
    gains = gains.at[G_AK].set(_tile4(a_kn_g))
    gains = gains.at[G_BQ].set(_tile4(b_qn_g) * (DA_QK ** -0.5 * LOG2E))
    gains = gains.at[G_BK].set(_tile4(b_kn_g))
    gains = gains.at[G_MQ].set(_tile4(m_qn_g) * (MEM_HD ** -0.5 * LOG2E))
    gains = gains.at[G_LNG].set(d_ln_g.astype(F32))
    gains = gains.at[G_LNB].set(d_ln_b.astype(F32))

    mk, mv = _memkv(mem, m_norm_g.reshape(1, d).astype(F32), stacked["m_wkv"], layer_idx,
                    _tile4(m_kn_g).reshape(1, BRANCH_W))
    sg_bias = jnp.repeat(d_bs.astype(F32).T, SG_GC, axis=1)
    p2, y_d, y_m, qt4, k4, vt4 = _proj(x2, ng, stacked["w_slab"], layer_idx, gains,
                                       stacked["d_ws"], sg_bias, mk, mv, t)
    p3 = p2.reshape(b, t, P_COLS)

    y_a = _na(p3, _na_bias_table(a_rpb, t // GRID_W))

    lam_init = 0.8 - 0.6 * math.exp(-0.3 * layer_idx)
    lamv = jnp.stack([b_lam_q1, b_lam_k1, b_lam_q2, b_lam_k2]).astype(F32)
    subg = (_tile4(b_sub_g) * (1.0 - lam_init)).reshape(1, BRANCH_W)
    y_b = _da(qt4, k4, vt4, p3, lamv, subg, lam_init)

    cw_bd = jnp.zeros((BRANCH_W, BRANCH_W), F32)
    for g in range(len(POOL_WINDOWS)):
        cw_bd = cw_bd.at[g * POOL_GC:(g + 1) * POOL_GC, g * POOL_GC:(g + 1) * POOL_GC].set(c_w[g].astype(F32))

    bg = jnp.zeros((8, d), F32).at[:N_BRANCH].set(b_gate.astype(F32))
    out = _final(x2, ng, y_a.reshape(n, BRANCH_W), y_b.reshape(n, BRANCH_W), p3, y_d, y_m,
                 cw_bd.astype(BF), c_scale.reshape(1, BRANCH_W).astype(F32),
                 stacked["w_gate"], bg, stacked["w_branch"], stacked["w_out"], layer_idx)
    return out.reshape(b, t, d)


def kernel(x, mem, norm_g, w_in, b_gate, a_qn_g, a_kn_g, a_rpb, b_qn_g, b_kn_g, b_lam_q1, b_lam_k1, b_lam_q2, b_lam_k2, b_sub_g, c_w, c_scale, d_ln_g, d_ln_b, d_ws, d_bs, m_norm_g, m_wkv, m_qn_g, m_kn_g, w_branch, w_out):
    w_slab, w_gate = _cast_w_in(w_in)
    stacked = {
        "w_slab": w_slab,
        "w_gate": w_gate,
        "w_branch": w_branch.astype(BF),
        "w_out": w_out.astype(BF),
        "m_wkv": m_wkv.astype(BF),
        "d_ws": d_ws.astype(BF),
    }
    for l in range(DEPTH):
        x = _layer(x, mem, l, stacked, norm_g[l], b_gate[l], a_qn_g[l], a_kn_g[l], a_rpb[l],
                   b_qn_g[l], b_kn_g[l], b_lam_q1[l], b_lam_k1[l], b_lam_q2[l], b_lam_k2[l], b_sub_g[l],
                   c_w[l], c_scale[l], d_ln_g[l], d_ln_b[l], d_bs[l], m_norm_g[l], m_qn_g[l], m_kn_g[l])
    return x
```

```python
import functools
import math

import numpy as np
import jax
import jax.numpy as jnp
from jax import lax
from jax.experimental import pallas as pl
from jax.experimental.pallas import tpu as pltpu

F32 = jnp.float32
BF = jnp.bfloat16

D_MODEL = 1024
DEPTH = 2
GRID_W = 64
BRANCH_W = 256
N_BRANCH = 5
NA_HEADS = 4
NA_HD = 64
NA_KH = 8
NA_KW = 16
DA_HEADS = 4
DA_QK = 32
DA_V = 64
ALIBI_BASE = 8.0
POOL_WINDOWS = (2, 4, 8, 16)
POOL_GC = 64
SG_CHUNK = 128
SG_GROUPS = 4
SG_GC = 64
MEM_HEADS = 4
MEM_HD = 64
N_SLABS = 15
SLAB_COLS = N_SLABS * BRANCH_W
EPS = 1e-6

(S_AQ, S_AK, S_AV, S_AZ, S_BQ, S_BK, S_BV, S_BZ, S_CX, S_CZ,
 S_DU, S_DV, S_DZ, S_MQ, S_MZ) = range(N_SLABS)

(G_AQ, G_AK, G_BQ, G_BK, G_MQ, G_LNG, G_LNB) = range(7)

VMEM_LIMIT_BYTES = 56 * 1024 * 1024
NEG_BIG = -1e30

ALIBI_SLOPES = tuple(2.0 ** (-ALIBI_BASE * (h + 1) / DA_HEADS) for h in range(DA_HEADS))
DA_LANES = 128


def _bf16_pieces(x, n):
    out = []
    for _ in range(n):
        piece = float(np.asarray(x, np.float32).astype(jnp.bfloat16).astype(np.float32))
        out.append(piece)
        x -= piece
    return tuple(out)


LOG2E = math.log2(math.e)
LOG2E_PIECES = _bf16_pieces(LOG2E, 3)
N_PIECES = len(LOG2E_PIECES)
X_KHI, X_KLO, X_QHI, X_QLO = (DA_V + g * N_PIECES for g in range(4))
X_END = DA_V + 4 * N_PIECES
X_ROWS = 16
POS_SPLIT = 64


def _alibi_parts(pos, slope):
    hi = (pos >> _log2(POS_SPLIT)).astype(F32) * (slope * POS_SPLIT)
    lo = (pos & (POS_SPLIT - 1)).astype(F32) * slope
    return hi, lo


def _piece_of(idx):
    out = jnp.zeros(idx.shape, F32)
    for g in range(4):
        for p, piece in enumerate(LOG2E_PIECES):
            out = jnp.where(idx == DA_V + g * N_PIECES + p, piece, out)
    return out


def _cparams(*sem):
    return pltpu.CompilerParams(dimension_semantics=sem, vmem_limit_bytes=VMEM_LIMIT_BYTES)


def _resident(shape, index_map):
    return pl.BlockSpec(shape, index_map, pipeline_mode=pl.Buffered(1))


def _layer_resident(shape, layer):
    return pl.BlockSpec((None,) + tuple(shape), lambda *_: (layer,) + (0,) * len(shape),
                        pipeline_mode=pl.Buffered(1))


def _dot(a, b):
    return jnp.dot(a, b, preferred_element_type=F32)


def _dot_nt(a, b):
    return lax.dot_general(a, b, (((1,), (1,)), ((), ())), preferred_element_type=F32)


def _log2(n):
    assert n & (n - 1) == 0, "power of two expected"
    return n.bit_length() - 1


def _lane_group(shape, width):
    return lax.broadcasted_iota(jnp.int32, shape, len(shape) - 1) >> _log2(width)


def _lane_mask(shape, lo, hi, dtype):
    lane = lax.broadcasted_iota(jnp.int32, shape, len(shape) - 1)
    return jnp.where((lane >= lo) & (lane < hi), 1.0, 0.0).astype(dtype)


def _group_mean_sq(xf, group):
    n = xf.shape[-1]
    r = _lane_group((n, n), group)
    c = lax.broadcasted_iota(jnp.int32, (n, n), 0) >> _log2(group)
    pm = jnp.where(r == c, 1.0 / group, 0.0).astype(BF)
    return _dot((xf * xf).astype(BF), pm)


def _sigmoid(x):
    return 1.0 / (1.0 + jnp.exp(-x))


def _silu(x):
    return x * _sigmoid(x)


def _gelu_tanh(x):
    return 0.5 * x * (1.0 + jnp.tanh(math.sqrt(2.0 / math.pi) * (x + 0.044715 * (x * x * x))))


def _rmsnorm_rows(x, g):
    ms = jnp.mean(x * x, axis=-1, keepdims=True)
    return x * lax.rsqrt(ms + EPS) * g


PROJ_LOOKAHEAD = 3
P_SLABS = (S_AQ, S_AK, S_AV, S_AZ, S_BZ, S_CX, S_CZ)
P_POS = {s: i for i, s in enumerate(P_SLABS)}
P_COLS = len(P_SLABS) * BRANCH_W
PROJ_ORDER = (S_MQ, S_MZ, S_DU, S_DV, S_DZ, S_BQ, S_BK, S_BV) + P_SLABS


def _proj_kernel(tiles_per_seq, x_ref, ng_ref, w_ref, gains_ref, ws_ref, sgb_ref, mk_ref, mv_ref,
                 p_ref, yd_ref, ym_ref, qt_ref, k4_ref, vt_ref):
    tm = x_ref.shape[0]
    t0 = (pl.program_id(0) % tiles_per_seq) * tm
    h = _rmsnorm_rows(x_ref[...], ng_ref[...]).astype(BF)

    def gnorm(r, group, row):
        return r * lax.rsqrt(_group_mean_sq(r, group) + EPS) * gains_ref[row:row + 1, :]

    def slab(j):
        return _dot(h, w_ref[:, j * BRANCH_W:(j + 1) * BRANCH_W])

    kept = {}
    pending = [slab(j) for j in PROJ_ORDER[:PROJ_LOOKAHEAD]]
    for idx, j in enumerate(PROJ_ORDER):
        if idx + PROJ_LOOKAHEAD < N_SLABS:
            pending.append(slab(PROJ_ORDER[idx + PROJ_LOOKAHEAD]))
        r = pending.pop(0)
        if j == S_AQ:
            r = gnorm(r, NA_HD, G_AQ)
        elif j == S_AK:
            r = gnorm(r, NA_HD, G_AK)
        elif j == S_BQ:
            r = gnorm(r, DA_QK, G_BQ)
        elif j == S_BK:
            r = gnorm(r, DA_QK, G_BK)
        elif j == S_MQ:
            r = gnorm(r, MEM_HD, G_MQ)
        elif j in (S_AZ, S_BZ, S_CZ, S_DZ, S_MZ):
            r = _silu(r)
        elif j == S_DU:
            r = _gelu_tanh(r)
        elif j == S_DV:
            v = _gelu_tanh(r)
            mu = jnp.mean(v, axis=-1, keepdims=True)
            vc = v - mu
            var = jnp.mean(vc * vc, axis=-1, keepdims=True)
            r = vc * lax.rsqrt(var + EPS) * gains_ref[G_LNG:G_LNG + 1, :] + gains_ref[G_LNB:G_LNB + 1, :]
        if j in P_POS:
            p_ref[:, P_POS[j] * BRANCH_W:(P_POS[j] + 1) * BRANCH_W] = r.astype(BF)
        elif j == S_MZ:
            ym_ref[...] = _mem_attend(kept.pop(S_MQ), r, mk_ref[0], mv_ref[0]).astype(BF)
        elif j == S_DZ:
            _spatial_gate(kept.pop(S_DU), kept.pop(S_DV), r, ws_ref, sgb_ref, yd_ref)
        elif j == S_BV:
            _da_layouts(kept.pop(S_BQ), kept.pop(S_BK), r.astype(BF), t0, qt_ref, k4_ref, vt_ref)
        elif j in (S_MQ, S_DV, S_BQ, S_BK):
            kept[j] = r.astype(BF)
        else:
            kept[j] = r


def _proj(x2, norm_g, w_slab, layer, gains, ws, sg_bias, mk, mv, seq_len, tm=512):
    n = x2.shape[0]
    b = n // seq_len
    tps = seq_len // tm
    m = mk.shape[1]
    return pl.pallas_call(
        functools.partial(_proj_kernel, tps),
        grid=(n // tm,),
        in_specs=[
            pl.BlockSpec((tm, D_MODEL), lambda i: (i, 0)),
            _resident((1, D_MODEL), lambda i: (0, 0)),
            _layer_resident((D_MODEL, SLAB_COLS), layer),
            _resident((8, BRANCH_W), lambda i: (0, 0)),
            _layer_resident((SG_GROUPS, SG_CHUNK, SG_CHUNK), layer),
            _resident((SG_CHUNK, BRANCH_W), lambda i: (0, 0)),
            pl.BlockSpec((1, m, BRANCH_W), lambda i: (i // tps, 0, 0)),
            pl.BlockSpec((1, m, BRANCH_W), lambda i: (i // tps, 0, 0)),
        ],
        out_specs=[
            pl.BlockSpec((tm, P_COLS), lambda i: (i, 0)),
            pl.BlockSpec((tm, BRANCH_W), lambda i: (i, 0)),
            pl.BlockSpec((tm, BRANCH_W), lambda i: (i, 0)),
            pl.BlockSpec((1, DA_HEADS, DA_V, tm), lambda i: (i // tps, 0, 0, i % tps)),
            pl.BlockSpec((1, DA_HEADS, tm, DA_LANES), lambda i: (i // tps, 0, i % tps, 0)),
            pl.BlockSpec((1, DA_HEADS, DA_VT_ROWS, tm), lambda i: (i // tps, 0, 0, i % tps)),
        ],
        out_shape=[
            jax.ShapeDtypeStruct((n, P_COLS), BF),
            jax.ShapeDtypeStruct((n, BRANCH_W), BF),
            jax.ShapeDtypeStruct((n, BRANCH_W), BF),
            jax.ShapeDtypeStruct((b, DA_HEADS, DA_V, seq_len), BF),
            jax.ShapeDtypeStruct((b, DA_HEADS, seq_len, DA_LANES), BF),
            jax.ShapeDtypeStruct((b, DA_HEADS, DA_VT_ROWS, seq_len), BF),
        ],
        compiler_params=_cparams("parallel"),
        name="proj",
    )(x2, norm_g, w_slab, gains, ws, sg_bias, mk, mv)


def _memkv_kernel(mem_ref, g_ref, w_ref, kg_ref, k_ref, v_ref):
    h = _rmsnorm_rows(mem_ref[0], g_ref[...]).astype(BF)
    k = _dot(h, w_ref[:, :BRANCH_W])
    v = _dot(h, w_ref[:, BRANCH_W:])
    k = k * lax.rsqrt(_group_mean_sq(k, MEM_HD) + EPS) * kg_ref[...]
    k_ref[0] = k.astype(BF)
    v_ref[0] = v.astype(BF)


def _memkv(mem, m_norm_g, m_wkv, layer, kg):
    b, m, _ = mem.shape
    return pl.pallas_call(
        _memkv_kernel,
        grid=(b,),
        in_specs=[
            pl.BlockSpec((1, m, D_MODEL), lambda i: (i, 0, 0)),
            _resident((1, D_MODEL), lambda i: (0, 0)),
            _layer_resident((D_MODEL, 2 * BRANCH_W), layer),
            _resident((1, BRANCH_W), lambda i: (0, 0)),
        ],
        out_specs=[pl.BlockSpec((1, m, BRANCH_W), lambda i: (i, 0, 0)),
                   pl.BlockSpec((1, m, BRANCH_W), lambda i: (i, 0, 0))],
        out_shape=[jax.ShapeDtypeStruct((b, m, BRANCH_W), BF)] * 2,
        compiler_params=_cparams("parallel"),
        name="memkv",
    )(mem, m_norm_g, m_wkv, kg)


NA_ROWS_PER_STEP = 8
NA_WIN = NA_KH * GRID_W
NA_LOOKAHEAD = 3


def _na_kernel(q_ref, k_ref, v_ref, z_ref, bias_ref, o_ref):
    n_rows = k_ref.shape[1] // GRID_W
    rb = pl.program_id(1)
    head_of_lane = _lane_group((GRID_W, BRANCH_W), NA_HD)
    masks = [_lane_mask((GRID_W, BRANCH_W), h * NA_HD, (h + 1) * NA_HD, BF) for h in range(NA_HEADS)]

    def window(rr):
        r = rb * NA_ROWS_PER_STEP + rr
        rs = jnp.clip(r - NA_KH // 2, 0, n_rows - NA_KH)
        cls = jnp.where(r < NA_KH // 2, r, jnp.where(r > n_rows - NA_KH // 2, r - (n_rows - NA_KH), NA_KH // 2))
        return pl.multiple_of(rs * GRID_W, GRID_W), cls

    def scores(rr):
        w0, cls = window(rr)
        q = q_ref[0, rr * GRID_W:(rr + 1) * GRID_W, :]
        qs = jnp.concatenate([q * masks[h] for h in range(NA_HEADS)], axis=0)
        return _dot_nt(qs, k_ref[0, pl.ds(w0, NA_WIN), :]) + bias_ref[cls]

    pending = [scores(rr) for rr in range(NA_LOOKAHEAD)]
    for rr in range(NA_ROWS_PER_STEP):
        if rr + NA_LOOKAHEAD < NA_ROWS_PER_STEP:
            pending.append(scores(rr + NA_LOOKAHEAD))
        s = pending.pop(0)
        w0, _ = window(rr)
        m = jnp.max(s, axis=-1, keepdims=True)
        p = jnp.exp2(s - m)
        l = jnp.sum(p, axis=-1, keepdims=True)
        pv = _dot(p.astype(BF), v_ref[0, pl.ds(w0, NA_WIN), :]) / l
        o = jnp.zeros((GRID_W, BRANCH_W), F32)
        for h in range(NA_HEADS):
            o = jnp.where(head_of_lane == h, pv[h * GRID_W:(h + 1) * GRID_W], o)
        rows = slice(rr * GRID_W, (rr + 1) * GRID_W)
        o_ref[0, rows, :] = (o * z_ref[0, rows, :].astype(F32)).astype(BF)


def _na(p3, bias):
    b, t, _ = p3.shape
    tq = NA_ROWS_PER_STEP * GRID_W
    return pl.pallas_call(
        _na_kernel,
        grid=(b, t // tq),
        in_specs=[
            pl.BlockSpec((1, tq, BRANCH_W), lambda i, j: (i, j, P_POS[S_AQ])),
            pl.BlockSpec((1, t, BRANCH_W), lambda i, j: (i, 0, P_POS[S_AK])),
            pl.BlockSpec((1, t, BRANCH_W), lambda i, j: (i, 0, P_POS[S_AV])),
            pl.BlockSpec((1, tq, BRANCH_W), lambda i, j: (i, j, P_POS[S_AZ])),
            _resident(bias.shape, lambda i, j: (0, 0, 0)),
        ],
        out_specs=pl.BlockSpec((1, tq, BRANCH_W), lambda i, j: (i, j, 0)),
        out_shape=jax.ShapeDtypeStruct((b, t, BRANCH_W), BF),
        compiler_params=_cparams("parallel", "arbitrary"),
        name="na",
    )(p3, p3, p3, p3, bias)


NA_REL_ROWS = 2 * NA_KH - 1
NA_REL_COLS = 2 * NA_KW - 1
NA_CLASSES = NA_KH


def _na_class_roff(n_rows):
    half = NA_KH // 2
    rows = np.array(list(range(half)) + [half] + [n_rows - half + 1 + i for i in range(half - 1)])
    rs = np.clip(rows - half, 0, n_rows - NA_KH)
    return [int(v) for v in rs - rows + NA_KH - 1]


def _nabias_kernel(roff0, rpb_ref, o_ref, f_scr):
    h = pl.program_id(0)
    shape = (GRID_W, 2 * GRID_W)
    q = lax.broadcasted_iota(jnp.int32, shape, 0)
    lane = lax.broadcasted_iota(jnp.int32, shape, 1)
    k = lane & (GRID_W - 1)
    coff = k - q + (NA_KW - 1)
    cs = jnp.clip(q - NA_KW // 2, 0, GRID_W - NA_KW)
    valid = (k >= cs) & (k < cs + NA_KW)
    first = lane < GRID_W

    def tile(r, carry):
        base = (h * NA_REL_ROWS + r) * NA_REL_COLS
        acc = jnp.zeros(shape, F32)
        for s in range(NA_REL_COLS):
            acc = jnp.where(coff == s, jnp.where(first, rpb_ref[base + s], rpb_ref[base + NA_REL_COLS + s]), acc)
        f_scr[r] = jnp.where(valid, acc * LOG2E, NEG_BIG)
        return carry

    lax.fori_loop(0, NA_REL_ROWS - 1, tile, 0)
    for c in range(NA_CLASSES):
        for p in range(NA_KH // 2):
            o_ref[c, 0, :, p * 2 * GRID_W:(p + 1) * 2 * GRID_W] = f_scr[roff0[c] + 2 * p]


def _na_bias_table(rpb, n_rows):
    return pl.pallas_call(
        functools.partial(_nabias_kernel, _na_class_roff(n_rows)),
        grid=(NA_HEADS,),
        in_specs=[pl.BlockSpec(memory_space=pltpu.SMEM)],
        out_specs=pl.BlockSpec((NA_CLASSES, 1, GRID_W, NA_WIN), lambda h: (0, h, 0, 0)),
        out_shape=jax.ShapeDtypeStruct((NA_CLASSES, NA_HEADS, GRID_W, NA_WIN), F32),
        scratch_shapes=[pltpu.VMEM((NA_REL_ROWS - 1, GRID_W, 2 * GRID_W), F32)],
        compiler_params=_cparams("arbitrary"),
        name="nabias",
    )(rpb.astype(F32).reshape(-1)).reshape(NA_CLASSES, NA_HEADS * GRID_W, NA_WIN)


DA_VT_ROWS = 80
X_SUM = DA_V


def _da_layouts(q, k, v, t0, qt_ref, k4_ref, vt_ref):
    tt = q.shape[0]
    src = lax.broadcasted_iota(jnp.int32, (BRANCH_W, DA_HEADS * DA_LANES), 0)
    dst = lax.broadcasted_iota(jnp.int32, (BRANCH_W, DA_HEADS * DA_LANES), 1)
    dst_head, dst_lane = dst >> _log2(DA_LANES), dst & (DA_LANES - 1)
    spread = jnp.where((src == dst_head * DA_V + dst_lane) & (dst_lane < DA_V), 1.0, 0.0).astype(BF)
    k_all = _dot(k, spread)
    src_t = lax.broadcasted_iota(jnp.int32, (DA_LANES, BRANCH_W), 1)
    dst_t = lax.broadcasted_iota(jnp.int32, (DA_LANES, BRANCH_W), 0)
    klane = lax.broadcasted_iota(jnp.int32, (tt, DA_LANES), 1)
    kpos = t0 + lax.broadcasted_iota(jnp.int32, (tt, DA_LANES), 0)
    vrow = lax.broadcasted_iota(jnp.int32, (DA_VT_ROWS, tt), 0)
    for h in range(DA_HEADS):
        sel_t = jnp.where((src_t == h * DA_V + dst_t) & (dst_t < DA_V), 1.0, 0.0).astype(BF)
        qt_ref[0, h] = _dot_nt(sel_t[:DA_V], q).astype(BF)
        vt = _dot_nt(sel_t[:DA_VT_ROWS], v)
        vt_ref[0, h] = jnp.where(vrow == X_SUM, 1.0, vt).astype(BF)
        slope = ALIBI_SLOPES[h]
        k4 = k_all[:, h * DA_LANES:(h + 1) * DA_LANES]
        jh, jl = _alibi_parts(kpos, slope)
        k4 = jnp.where((klane >= X_KHI) & (klane < X_KLO), jh, k4)
        k4 = jnp.where((klane >= X_KLO) & (klane < X_QHI), jl, k4)
        k4 = jnp.where((klane >= X_QHI) & (klane < X_END), _piece_of(klane), k4)
        k4_ref[0, h] = k4.astype(BF)


DA_QCOLS = 256
DA_KROWS = 256
DA_LOOKAHEAD = 8
DA_GROUP = 2
Q_LEFT, Q_DIAG, Q_RIGHT = range(3)


def _da_kernel(lam_init, qt_ref, k_ref, vt_ref, z_ref, lamv_ref, subg_ref, o_ref,
               qv_scr, dist_scr, spre_scr, m_scr, acc_scr):
    bq = qt_ref.shape[3]
    bk = bq
    nk = k_ref.shape[2] // bk
    qi = pl.program_id(1)
    frow = lax.broadcasted_iota(jnp.int32, (DA_V, bq), 0)
    xrow = DA_V + lax.broadcasted_iota(jnp.int32, (X_ROWS, bq), 0)
    qpos = qi * bq + lax.broadcasted_iota(jnp.int32, (X_ROWS, bq), 1)

    lv = lamv_ref[...]
    lam = (jnp.exp(jnp.sum(lv[0:1] * lv[1:2], axis=-1, keepdims=True))
           - jnp.exp(jnp.sum(lv[2:3] * lv[3:4], axis=-1, keepdims=True)) + lam_init)

    cols = [slice(c * DA_QCOLS, (c + 1) * DA_QCOLS) for c in range(bq // DA_QCOLS)]
    chains = [(h, m, c) for h in range(DA_HEADS) for m in range(2) for c in range(len(cols))]

    def tiles_of(blk):
        return [(blk, kc * DA_KROWS, ch) for ch in chains for kc in range(bk // DA_KROWS)]

    def scores(tile):
        (k0, side, diag), r0, (h, m, c) = tile
        if diag:
            c0 = c * DA_QCOLS
            side = Q_DIAG if r0 == c0 else (Q_LEFT if r0 < c0 else Q_RIGHT)
        s = _dot(k_ref[0, h, pl.ds(k0 + r0, DA_KROWS), :], qv_scr[h, side, m, :, cols[c]])
        if diag and r0 == c0:
            s = s - dist_scr[...] * (ALIBI_SLOPES[h] * LOG2E)
        return s

    def run_block(blk, nxt):
        tiles = tiles_of(blk)
        for i, ((k0, _, _), r0, (h, m, c)) in enumerate(tiles):
            j = i + DA_LOOKAHEAD
            if j < len(tiles):
                spre_scr[j] = scores(tiles[j])
            elif nxt is not None:
                spre_scr[j - len(tiles)] = scores(tiles_of(nxt)[j - len(tiles)])
            if (i + 1) % DA_GROUP:
                continue
            first = i + 1 - DA_GROUP
            m_old = m_scr[h, m, :, cols[c]]
            m_new = m_old
            for t in range(first, i + 1):
                m_new = jnp.maximum(m_new, jnp.max(spre_scr[t], axis=0, keepdims=True))
            alpha = jnp.exp2(m_old - m_new)
            p = jnp.concatenate([jnp.exp2(spre_scr[t] - m_new).astype(BF) for t in range(first, i + 1)], axis=0)
            pv = _dot(vt_ref[0, h, :, pl.ds(k0, bk)], p)
            acc_scr[h, m, :, cols[c]] = alpha * acc_scr[h, m, :, cols[c]] + pv
            m_scr[h, m, :, cols[c]] = m_new

    assert DA_KROWS == DA_QCOLS, "diagonal tiles are taken to be square"
    rel = (lax.broadcasted_iota(jnp.int32, (DA_KROWS, DA_QCOLS), 0)
           - lax.broadcasted_iota(jnp.int32, (DA_KROWS, DA_QCOLS), 1))
    dist_scr[...] = jnp.abs(rel).astype(F32)
    pieces = _piece_of(xrow)
    tail = jnp.zeros((DA_LANES - DA_V - X_ROWS, bq), BF)
    for h in range(DA_HEADS):
        slope = ALIBI_SLOPES[h]
        ih, il = _alibi_parts(qpos, slope)
        ex =jnp.where(xrow < X_QHI, pieces,
                       jnp.where(xrow < X_QLO, -ih, jnp.where(xrow < X_END, -il, 0.0))).astype(BF)
        for m in range(2):
            qm = qt_ref[0, h] * jnp.where(frow >> _log2(DA_QK) == m, 1.0, 0.0).astype(BF)
            for side, extra in ((Q_LEFT, ex), (Q_DIAG, jnp.zeros_like(ex)), (Q_RIGHT, -ex)):
                qv_scr[h, side, m, :DA_V, :] = qm
                qv_scr[h, side, m, DA_V:DA_V + X_ROWS, :] = extra
                qv_scr[h, side, m, DA_V + X_ROWS:, :] = tail
    m_scr[...] = jnp.full(m_scr.shape, NEG_BIG, F32)
    acc_scr[...] = jnp.zeros(acc_scr.shape, F32)

    def off_diagonal(t):
        kb = jnp.where(t >= qi, t + 1, t)
        return pl.multiple_of(kb * bk, bk), jnp.where(t >= qi, Q_RIGHT, Q_LEFT), False

    diagonal = (pl.multiple_of(qi * bk, bk), Q_DIAG, True)

    for i, tile in enumerate(tiles_of(off_diagonal(0))[:DA_LOOKAHEAD]):
        spre_scr[i] = scores(tile)

    def body(t, carry):
        run_block(off_diagonal(t), off_diagonal(t + 1))
        return carry

    lax.fori_loop(0, nk - 2, body, 0)
    run_block(off_diagonal(nk - 2), diagonal)
    run_block(diagonal, None)

    heads = []
    for h in range(DA_HEADS):
        o = [acc_scr[h, m, :DA_V, :] / acc_scr[h, m, X_SUM:X_SUM + 1, :] for m in range(2)]
        heads.append(o[0] - lam * o[1])

    full = jnp.concatenate(heads, axis=0).T
    y = full * lax.rsqrt(_group_mean_sq(full, DA_V) + EPS) * subg_ref[...]
    o_ref[0] = (y * z_ref[0].astype(F32)).astype(BF)


def _da(qt4, k4, vt4, p3, lamv, subg, lam_init, bq=512):
    b, _, t, _ = k4.shape
    assert all(math.frexp(s)[0] == 0.5 for s in ALIBI_SLOPES), "ALiBi slopes must be powers of two"
    assert t <= POS_SPLIT * 256 and POS_SPLIT <= 256
    return pl.pallas_call(
        functools.partial(_da_kernel, lam_init),
        grid=(b, t // bq),
        in_specs=[
            pl.BlockSpec((1, DA_HEADS, DA_V, bq), lambda i, j: (i, 0, 0, j)),
            pl.BlockSpec((1, DA_HEADS, t, DA_LANES), lambda i, j: (i, 0, 0, 0)),
            pl.BlockSpec((1, DA_HEADS, DA_VT_ROWS, t), lambda i, j: (i, 0, 0, 0)),
            pl.BlockSpec((1, bq, BRANCH_W), lambda i, j: (i, j, P_POS[S_BZ])),
            _resident((4, DA_QK), lambda i, j: (0, 0)),
            _resident((1, BRANCH_W), lambda i, j: (0, 0)),
        ],
        out_specs=pl.BlockSpec((1, bq, BRANCH_W), lambda i, j: (i, j, 0)),
        out_shape=jax.ShapeDtypeStruct((b, t, BRANCH_W), BF),
        scratch_shapes=[
            pltpu.VMEM((DA_HEADS, 3, 2, DA_LANES, bq), BF),
            pltpu.VMEM((DA_KROWS, DA_QCOLS), F32),
            pltpu.VMEM((2 * DA_HEADS * 2 * (bq // DA_QCOLS), DA_KROWS, DA_QCOLS), F32),
            pltpu.VMEM((DA_HEADS, 2, 1, bq), F32),
            pltpu.VMEM((DA_HEADS, 2, DA_VT_ROWS, bq), F32),
        ],
        compiler_params=_cparams("parallel", "arbitrary"),
        name="da",
    )(qt4, k4, vt4, p3, lamv, subg)


POOL_HALO = 16


def _pool_window_sums(x_ref, tt, ti, band_ref):
    t_total = x_ref.shape[1]
    nt = t_total // tt
    t0 = pl.multiple_of(ti * tt, tt)
    xm = x_ref[0, pl.ds(t0, tt), :]
    p0 = pl.multiple_of(jnp.maximum(t0 - POOL_HALO, 0), POOL_HALO)
    n0 = pl.multiple_of(jnp.minimum(t0 + tt, t_total - POOL_HALO), POOL_HALO)
    has_prev = (ti > 0).astype(F32)
    has_next = (ti < nt - 1).astype(F32)
    prev = (x_ref[0, pl.ds(p0, POOL_HALO), :].astype(F32) * has_prev).astype(BF)
    nxt = (x_ref[0, pl.ds(n0, POOL_HALO), :].astype(F32) * has_next).astype(BF)
    xcat = jnp.concatenate([prev, xm, nxt], axis=0)

    sub = band_ref.shape[1]
    sums = [[_dot(band_ref[g], xcat[r0:r0 + sub + 2 * POOL_HALO]) for g in range(len(POOL_WINDOWS))]
            for r0 in range(0, tt, sub)]
    return xm, sums


def _pool_finish(xm, sums, ti, t_total, z_ref, cw_ref, cs_ref):
    tt = xm.shape[0]
    sub = tt // len(sums)
    t0 = ti * tt
    group = _lane_group((sub, BRANCH_W), POOL_GC)
    half = jnp.left_shift(1, group)
    out = []
    for i, r0 in enumerate(range(0, tt, sub)):
        wsum = jnp.zeros((sub, BRANCH_W), F32)
        for g in range(len(POOL_WINDOWS)):
            wsum = jnp.where(group == g, sums[i][g], wsum)
        tpos = t0 + r0 + lax.broadcasted_iota(jnp.int32, (sub, BRANCH_W), 0)
        lo = jnp.maximum(tpos - half, 0)
        hi = jnp.minimum(tpos + half - 1, t_total - 1)
        cnt = (hi - lo + 1).astype(F32)
        dlt = wsum / cnt - xm[r0:r0 + sub].astype(F32)
        y = _dot(dlt.astype(BF), cw_ref[...]) * cs_ref[...]
        out.append((y * z_ref[r0:r0 + sub, :].astype(F32)).astype(BF))
    return jnp.concatenate(out, axis=0)


def _pool_band(tt):
    band = np.zeros((len(POOL_WINDOWS), tt, tt + 2 * POOL_HALO), np.float32)
    t = np.arange(tt)[:, None]
    u = np.arange(tt + 2 * POOL_HALO)[None, :] - POOL_HALO
    for g, w in enumerate(POOL_WINDOWS):
        band[g] = ((u >= t - w // 2) & (u <= t - w // 2 + w - 1)).astype(np.float32)
    return jnp.asarray(band, BF)


POOL_SUB = 256


def _spatial_gate(u, vn, z, ws_ref, bias_ref, o_ref):
    group = _lane_group((SG_CHUNK, BRANCH_W), SG_GC)
    for c in range(u.shape[0] // SG_CHUNK):
        rows = slice(c * SG_CHUNK, (c + 1) * SG_CHUNK)
        mix = jnp.zeros((SG_CHUNK, BRANCH_W), F32)
        for g in range(SG_GROUPS):
            mix = jnp.where(group == g, _dot(ws_ref[g], vn[rows]), mix)
        o_ref[rows, :] = (u[rows] * (mix + bias_ref[...]) * z[rows]).astype(BF)


def _mem_attend(q, z, k, v):
    head_of_lane = _lane_group(q.shape, MEM_HD)
    scores = [_dot_nt(q * _lane_mask(q.shape, h * MEM_HD, (h + 1) * MEM_HD, BF), k) for h in range(MEM_HEADS)]
    o = jnp.zeros(q.shape, F32)
    for h in range(MEM_HEADS):
        s = scores[h]
        m = jnp.max(s, axis=-1, keepdims=True)
        p = jnp.exp2(s - m)
        l = jnp.sum(p, axis=-1, keepdims=True)
        oh = _dot(p.astype(BF), v) / l
        o = jnp.where(head_of_lane == h, oh, o)
    return o * z


def _final_kernel(tiles_per_seq, x_ref, ng_ref, ya_ref, yb_ref, cx_ref, cz_ref, yd_ref, ym_ref,
                  band_ref, cw_ref, cs_ref, wg_ref, bg_ref, wb_ref, wo_ref, o_ref):
    x = x_ref[...]
    ti = pl.program_id(0) % tiles_per_seq
    xm, sums = _pool_window_sums(cx_ref, x.shape[0], ti, band_ref)
    y_c = _pool_finish(xm, sums, ti, cx_ref.shape[1], cz_ref, cw_ref, cs_ref)
    h = _rmsnorm_rows(x, ng_ref[...]).astype(BF)
    merged = jnp.zeros(x.shape, F32)
    for i, y in enumerate((ya_ref[...], yb_ref[...], y_c, yd_ref[...], ym_ref[...])):
        logit = _dot(h, wg_ref[:, i * D_MODEL:(i + 1) * D_MODEL]) + bg_ref[i:i + 1, :]
        merged = merged + _sigmoid(logit) * _dot(y, wb_ref[i])
    o_ref[...] = x + _dot(merged.astype(BF), wo_ref[...])


def _final(x2, norm_g, y_a, y_b, p3, y_d, y_m, cw_bd, c_scale, w_gate, b_gate, w_branch, w_out, layer, tm=512):
    n = x2.shape[0]
    b, t, _ = p3.shape
    tps = t // tm
    band = _pool_band(POOL_SUB)
    yspec = pl.BlockSpec((tm, BRANCH_W), lambda i: (i, 0))
    return pl.pallas_call(
        functools.partial(_final_kernel, tps),
        grid=(n // tm,),
        in_specs=[
            pl.BlockSpec((tm, D_MODEL), lambda i: (i, 0)),
            _resident((1, D_MODEL), lambda i: (0, 0)),
            yspec, yspec,
            pl.BlockSpec((1, t, BRANCH_W), lambda i: (i // tps, 0, P_POS[S_CX])),
            pl.BlockSpec((tm, BRANCH_W), lambda i: (i, P_POS[S_CZ])),
            yspec, yspec,
            _resident(band.shape, lambda i: (0, 0, 0)),
            _resident((BRANCH_W, BRANCH_W), lambda i: (0, 0)),
            _resident((1, BRANCH_W), lambda i: (0, 0)),
            _layer_resident((D_MODEL, N_BRANCH * D_MODEL), layer),
            _resident((8, D_MODEL), lambda i: (0, 0)),
            _layer_resident((N_BRANCH, BRANCH_W, D_MODEL), layer),
            _layer_resident((D_MODEL, D_MODEL), layer),
        ],
        out_specs=pl.BlockSpec((tm, D_MODEL), lambda i: (i, 0)),
        out_shape=jax.ShapeDtypeStruct((n, D_MODEL), F32),
        compiler_params=_cparams("parallel"),
        name="final",
    )(x2, norm_g, y_a, y_b, p3, p3.reshape(n, P_COLS), y_d, y_m, band, cw_bd, c_scale,
      w_gate, b_gate, w_branch, w_out)


W_IN_CAST_ROWS = 128


def _cast_w_in_kernel(w_ref, slab_ref, gate_ref):
    w = w_ref[0]
    slab_ref[0] = w[:, :SLAB_COLS].astype(BF)
    gate_ref[0] = w[:, SLAB_COLS:].astype(BF)


def _cast_w_in(w_in):
    depth, d, cols = w_in.shape
    gate_cols = cols - SLAB_COLS
    return pl.pallas_call(
        _cast_w_in_kernel,
        grid=(depth, d // W_IN_CAST_ROWS),
        in_specs=[pl.BlockSpec((1, W_IN_CAST_ROWS, cols), lambda l, r: (l, r, 0))],
        out_specs=[pl.BlockSpec((1, W_IN_CAST_ROWS, SLAB_COLS), lambda l, r: (l, r, 0)),
                   pl.BlockSpec((1, W_IN_CAST_ROWS, gate_cols), lambda l, r: (l, r, 0))],
        out_shape=[jax.ShapeDtypeStruct((depth, d, SLAB_COLS), BF),
                   jax.ShapeDtypeStruct((depth, d, gate_cols), BF)],
        compiler_params=_cparams("parallel", "parallel"),
        name="cast_w_in",
    )(w_in)


def _tile4(g):
    return jnp.tile(g.astype(F32), BRANCH_W // g.shape[0])


def _layer(x, mem, layer_idx, stacked, norm_g, b_gate, a_qn_g, a_kn_g, a_rpb,
           b_qn_g, b_kn_g, b_lam_q1, b_lam_k1, b_lam_q2, b_lam_k2, b_sub_g,
           c_w, c_scale, d_ln_g, d_ln_b, d_bs, m_norm_g, m_qn_g, m_kn_g):
    b, t, d = x.shape
    n = b * t
    x2 = x.reshape(n, d)
    ng = norm_g.reshape(1, d).astype(F32)

    gains = jnp.zeros((8, BRANCH_W), F32)
    gains = gains.at[G_AQ].set(_tile4(a_qn_g) * (NA_HD ** -0.5 * LOG2E))
    gains = gains.at[G_AK].set(_tile4(a_kn_g))
    gains = gains.at[G_BQ].set(_tile4(b_qn_g) * (DA_QK ** -0.5 * LOG2E))
    gains = gains.at[G_BK].set(_tile4(b_kn_g))
    gains = gains.at[G_MQ].set(_tile4(m_qn_g) * (MEM_HD ** -0.5 * LOG2E))
    gains = gains.at[G_LNG].set(d_ln_g.astype(F32))
    gains = gains.at[G_LNB].set(d_ln_b.astype(F32))

    mk, mv = _memkv(mem, m_norm_g.reshape(1, d).astype(F32), stacked["m_wkv"], layer_idx,
                    _tile4(m_kn_g).reshape(1, BRANCH_W))
    sg_bias = jnp.repeat(d_bs.astype(F32).T, SG_GC, axis=1)
    p2, y_d, y_m, qt4, k4, vt4 = _proj(x2, ng, stacked["w_slab"], layer_idx, gains,
                                       stacked["d_ws"], sg_bias, mk, mv, t)
    p3 = p2.reshape(b, t, P_COLS)

    y_a = _na(p3, _na_bias_table(a_rpb, t // GRID_W))

    lam_init = 0.8 - 0.6 * math.exp(-0.3 * layer_idx)
    lamv = jnp.stack([b_lam_q1, b_lam_k1, b_lam_q2, b_lam_k2]).astype(F32)
    subg = (_tile4(b_sub_g) * (1.0 - lam_init)).reshape(1, BRANCH_W)
    y_b = _da(qt4, k4, vt4, p3, lamv, subg, lam_init)

    cw_bd = jnp.zeros((BRANCH_W, BRANCH_W), F32)
    for g in range(len(POOL_WINDOWS)):
        cw_bd = cw_bd.at[g * POOL_GC:(g + 1) * POOL_GC, g * POOL_GC:(g + 1) * POOL_GC].set(c_w[g].astype(F32))

    bg = jnp.zeros((8, d), F32).at[:N_BRANCH].set(b_gate.astype(F32))
    out = _final(x2, ng, y_a.reshape(n, BRANCH_W), y_b.reshape(n, BRANCH_W), p3, y_d, y_m,
                 cw_bd.astype(BF), c_scale.reshape(1, BRANCH_W).astype(F32),
                 stacked["w_gate"], bg, stacked["w_branch"], stacked["w_out"], layer_idx)
    return out.reshape(b, t, d)


def kernel(x, mem, norm_g, w_in, b_gate, a_qn_g, a_kn_g, a_rpb, b_qn_g, b_kn_g, b_lam_q1, b_lam_k1, b_lam_q2, b_lam_k2, b_sub_g, c_w, c_scale, d_ln_g, d_ln_b, d_ws, d_bs, m_norm_g, m_wkv, m_qn_g, m_kn_g, w_branch, w_out):
    w_slab, w_gate = _cast_w_in(w_in)
    stacked = {
        "w_slab": w_slab,
        "w_gate": w_gate,
        "w_branch": w_branch.astype(BF),
        "w_out": w_out.astype(BF),
        "m_wkv": m_wkv.astype(BF),
        "d_ws": d_ws.astype(BF),
    }
    for l in range(DEPTH):
        x = _layer(x, mem, l, stacked, norm_g[l], b_gate[l], a_qn_g[l], a_kn_g[l], a_rpb[l],
                   b_qn_g[l], b_kn_g[l], b_lam_q1[l], b_lam_k1[l], b_lam_q2[l], b_lam_k2[l], b_sub_g[l],
                   c_w[l], c_scale[l], d_ln_g[l], d_ln_b[l], d_bs[l], m_norm_g[l], m_qn_g[l], m_kn_g[l])
    return x
```
